```python
import jax, jax.numpy as jnp
from jax import lax
import numpy as np

D_MODEL = 1024
BATCH = 8
SEQ = 8192
DEPTH = 4

N_MIXERS = 2
N_MLA_LAYERS = (DEPTH + 1) // 2
N_RWKV_LAYERS = DEPTH // 2
NORM_EPS = 1e-6

MLA_HEADS = 8
QK_NOPE_DIM = 128
QK_ROPE_DIM = 64
V_HEAD_DIM = 128
Q_LORA = 768
KV_LORA = 256
MLA_WIDTH = MLA_HEADS * V_HEAD_DIM
MLA_IN = Q_LORA + KV_LORA + QK_ROPE_DIM + MLA_WIDTH
ROPE_THETA = 10000.0
BLOCK = 128
MAX_POS_OFFSET = 4096

RWKV_HEAD = 64
RWKV_HEADS = D_MODEL // RWKV_HEAD
RWKV_WIDTH = RWKV_HEADS * RWKV_HEAD
DECAY_LORA = 64
ICLR_LORA = 64
RWKV_IN = 4 * RWKV_WIDTH + DECAY_LORA + ICLR_LORA
GN_EPS = 64e-5

kernel_name = 'hybrid_mla_rwkv7_gated'


def rms_norm(x, g):
    xf = x.astype(jnp.float32)
    y = xf * lax.rsqrt(jnp.mean(xf * xf, axis=-1, keepdims=True) + NORM_EPS)
    return (y * g.astype(jnp.float32)).astype(x.dtype)


def apply_rope(t, positions):
    half = t.shape[-1] // 2
    inv_freq = 1.0 / (ROPE_THETA ** (jnp.arange(half, dtype=jnp.float32) / half))
    ang = positions.astype(jnp.float32)[:, :, None] * inv_freq
    cos = jnp.cos(ang)[:, :, None, :]
    sin = jnp.sin(ang)[:, :, None, :]
    tf = t.astype(jnp.float32)
    t1, t2 = tf[..., :half], tf[..., half:]
    return jnp.concatenate([t1 * cos - t2 * sin, t2 * cos + t1 * sin], axis=-1).astype(t.dtype)


def block_causal_attention(q, k, v):
    bsz, n_heads, seq, dqk = q.shape
    dv = v.shape[-1]
    n_blocks = seq // BLOCK
    scale = dqk ** -0.5
    offs = jnp.arange(BLOCK)

    def query_block(qi):
        q_start = qi * BLOCK
        qb = lax.dynamic_slice_in_dim(q, q_start, BLOCK, axis=2).astype(jnp.float32) * scale
        q_idx = q_start + offs

        def key_block(kb, carry):
            m, l, acc = carry
            k_start = kb * BLOCK
            kblk = lax.dynamic_slice_in_dim(k, k_start, BLOCK, axis=2).astype(jnp.float32)
            vblk = lax.dynamic_slice_in_dim(v, k_start, BLOCK, axis=2).astype(jnp.float32)
            s = jnp.einsum('bhqd,bhkd->bhqk', qb, kblk)
            s = jnp.where((k_start + offs)[None, :] <= q_idx[:, None], s, -jnp.inf)
            m_new = jnp.maximum(m, s.max(axis=-1))
            p = jnp.exp(s - m_new[..., None])
            corr = jnp.exp(m - m_new)
            l_new = l * corr + p.sum(axis=-1)
            acc_new = acc * corr[..., None] + jnp.einsum('bhqk,bhkd->bhqd', p, vblk)
            return (m_new, l_new, acc_new)

        init = (jnp.full((bsz, n_heads, BLOCK), -jnp.inf, jnp.float32),
                jnp.zeros((bsz, n_heads, BLOCK), jnp.float32),
                jnp.zeros((bsz, n_heads, BLOCK, dv), jnp.float32))
        _, l, acc = lax.fori_loop(0, qi + 1, key_block, init)
        return (acc / l[..., None]).astype(v.dtype)

    out = lax.map(query_block, jnp.arange(n_blocks))
    return out.transpose(1, 0, 3, 2, 4).reshape(bsz, seq, n_heads, dv)


def mla_mixer(h, positions, w_in, q_norm, w_uq, kv_norm, w_ukv, w_out):
    bsz, seq, _ = h.shape
    proj = h @ w_in
    q_lat, kv_lat, k_rope, gate = jnp.split(
        proj, [Q_LORA, Q_LORA + KV_LORA, Q_LORA + KV_LORA + QK_ROPE_DIM], axis=-1)
    q = (rms_norm(q_lat, q_norm) @ w_uq).reshape(bsz, seq, MLA_HEADS, QK_NOPE_DIM + QK_ROPE_DIM)
    q_nope, q_rope = q[..., :QK_NOPE_DIM], q[..., QK_NOPE_DIM:]
    q_rope = apply_rope(q_rope, positions)
    kv = (rms_norm(kv_lat, kv_norm) @ w_ukv).reshape(bsz, seq, MLA_HEADS, QK_NOPE_DIM + V_HEAD_DIM)
    k_nope, v = kv[..., :QK_NOPE_DIM], kv[..., QK_NOPE_DIM:]
    k_rope = apply_rope(k_rope[:, :, None, :], positions)
    q_full = jnp.concatenate([q_nope, q_rope], axis=-1)
    k_full = jnp.concatenate(
        [k_nope, jnp.broadcast_to(k_rope, (bsz, seq, MLA_HEADS, QK_ROPE_DIM))], axis=-1)
    o = block_causal_attention(q_full.transpose(0, 2, 1, 3), k_full.transpose(0, 2, 1, 3),
                               v.transpose(0, 2, 1, 3))
    o = o.reshape(bsz, seq, MLA_WIDTH) * jax.nn.silu(gate)
    return o @ w_out


def token_shift(y):
    return jnp.pad(y, ((0, 0), (1, 0), (0, 0)))[:, :-1]


def wkv7_scan(r, w, k, v, a_vec, b_vec):
    bsz, _, n_heads, n = r.shape

    def step(state, inp):
        r_t, w_t, k_t, v_t, a_t, b_t = inp
        sa = jnp.einsum('bhij,bhj->bhi', state, a_t)
        state = (state * w_t[:, :, None, :] + sa[..., None] * b_t[:, :, None, :]
                 + v_t[..., None] * k_t[:, :, None, :])
        return state, jnp.einsum('bhij,bhj->bhi', state, r_t)

    xs = tuple(jnp.moveaxis(t, 1, 0) for t in (r, w, k, v, a_vec, b_vec))
    state0 = jnp.zeros((bsz, n_heads, n, n), jnp.float32)
    _, ys = lax.scan(step, state0, xs)
    return jnp.moveaxis(ys, 0, 1)


def rwkv7_mixer(h, w_in, mu, w0, w_w2, a0, w_a2, k_k, k_a, r_k, ln_g, ln_b, w_out):
    bsz, seq, _ = h.shape
    proj = h @ w_in
    proj = proj + mu * (token_shift(proj) - proj)
    r, k, v, g, wd, ad = jnp.split(
        proj, [RWKV_WIDTH, 2 * RWKV_WIDTH, 3 * RWKV_WIDTH, 4 * RWKV_WIDTH,
               4 * RWKV_WIDTH + DECAY_LORA], axis=-1)
    w_log = -jax.nn.softplus(-(w0 + jnp.tanh(wd) @ w_w2)) - 0.5
    decay = jnp.exp(-jnp.exp(w_log.astype(jnp.float32)))
    a = jax.nn.sigmoid(a0 + ad @ w_a2)

    def heads(t):
        return t.reshape(bsz, seq, RWKV_HEADS, RWKV_HEAD).astype(jnp.float32)

    kk = heads(k * k_k)
    kk = kk / jnp.maximum(jnp.sqrt(jnp.sum(kk * kk, axis=-1, keepdims=True)), 1e-12)
    k = k * (1.0 + (a - 1.0) * k_a)
    r_h, k_h, v_h, a_h, w_h = heads(r), heads(k), heads(v), heads(a), heads(decay)
    y = wkv7_scan(r_h, w_h, k_h, v_h, -kk, kk * a_h)
    mean = jnp.mean(y, axis=-1, keepdims=True)
    var = jnp.mean(jnp.square(y - mean), axis=-1, keepdims=True)
    y = (y - mean) * lax.rsqrt(var + GN_EPS)
    y = (y * ln_g.reshape(RWKV_HEADS, RWKV_HEAD).astype(jnp.float32)
         + ln_b.reshape(RWKV_HEADS, RWKV_HEAD).astype(jnp.float32))
    bonus = jnp.sum(r_h * k_h * r_k.reshape(RWKV_HEADS, RWKV_HEAD).astype(jnp.float32),
                    axis=-1, keepdims=True)
    y = y + bonus * v_h
    y = y.reshape(bsz, seq, RWKV_WIDTH).astype(h.dtype) * jax.nn.silu(g)
    return y @ w_out


def setup_inputs(seed: int = 0) -> dict:
    key = jax.random.key(seed)
    ks = jax.random.split(key, 24)

    def nrm(k, shape, scale):
        return scale * jax.random.normal(k, shape, jnp.float32)

    nm, nr, w = N_MLA_LAYERS, N_RWKV_LAYERS, RWKV_WIDTH
    x = jax.random.normal(ks[0], (BATCH, SEQ, D_MODEL), jnp.float32)
    offsets = jax.random.randint(ks[1], (BATCH, 1), 0, MAX_POS_OFFSET, jnp.int32)
    positions = offsets + jnp.arange(SEQ, dtype=jnp.int32)[None, :]
    norm_g = 1.0 + nrm(ks[2], (DEPTH, D_MODEL), 0.02)
    mla_w_in = nrm(ks[3], (nm, D_MODEL, MLA_IN), D_MODEL ** -0.5)
    mla_q_norm = 1.0 + nrm(ks[4], (nm, Q_LORA), 0.02)
    mla_w_uq = nrm(ks[5], (nm, Q_LORA, MLA_HEADS * (QK_NOPE_DIM + QK_ROPE_DIM)), Q_LORA ** -0.5)
    mla_kv_norm = 1.0 + nrm(ks[6], (nm, KV_LORA), 0.02)
    mla_w_ukv = nrm(ks[7], (nm, KV_LORA, MLA_HEADS * (QK_NOPE_DIM + V_HEAD_DIM)), KV_LORA ** -0.5)
    mla_w_out = nrm(ks[8], (nm, MLA_WIDTH, D_MODEL), MLA_WIDTH ** -0.5)
    rwkv_w_in = nrm(ks[9], (nr, D_MODEL, RWKV_IN), D_MODEL ** -0.5)
    rwkv_mu = jax.random.uniform(ks[10], (nr, RWKV_IN), jnp.float32)
    decay_base = jnp.linspace(-6.0, -1.0, w, dtype=jnp.float32)
    rwkv_w0 = decay_base[None, :] + nrm(ks[11], (nr, w), 0.1)
    rwkv_w_w2 = nrm(ks[12], (nr, DECAY_LORA, w), 0.5 * DECAY_LORA ** -0.5)
    rwkv_a0 = nrm(ks[13], (nr, w), 0.1)
    rwkv_w_a2 = nrm(ks[14], (nr, ICLR_LORA, w), ICLR_LORA ** -0.5)
    rwkv_k_k = 0.85 + nrm(ks[15], (nr, w), 0.02)
    rwkv_k_a = 1.0 + nrm(ks[16], (nr, w), 0.02)
    rwkv_r_k = nrm(ks[17], (nr, w), 0.1)
    rwkv_ln_g = 1.0 + nrm(ks[18], (nr, w), 0.02)
    rwkv_ln_b = nrm(ks[19], (nr, w), 0.02)
    rwkv_w_out = nrm(ks[20], (nr, w, D_MODEL), w ** -0.5)
    final_g = 1.0 + nrm(ks[21], (D_MODEL,), 0.02)
    return {'x': x, 'positions': positions, 'norm_g': norm_g,
            'mla_w_in': mla_w_in, 'mla_q_norm': mla_q_norm, 'mla_w_uq': mla_w_uq,
            'mla_kv_norm': mla_kv_norm, 'mla_w_ukv': mla_w_ukv, 'mla_w_out': mla_w_out,
            'rwkv_w_in': rwkv_w_in, 'rwkv_mu': rwkv_mu, 'rwkv_w0': rwkv_w0,
            'rwkv_w_w2': rwkv_w_w2, 'rwkv_a0': rwkv_a0, 'rwkv_w_a2': rwkv_w_a2,
            'rwkv_k_k': rwkv_k_k, 'rwkv_k_a': rwkv_k_a, 'rwkv_r_k': rwkv_r_k,
            'rwkv_ln_g': rwkv_ln_g, 'rwkv_ln_b': rwkv_ln_b, 'rwkv_w_out': rwkv_w_out,
            'final_g': final_g}


def reference(x, positions, norm_g, mla_w_in, mla_q_norm, mla_w_uq, mla_kv_norm, mla_w_ukv,
              mla_w_out, rwkv_w_in, rwkv_mu, rwkv_w0, rwkv_w_w2, rwkv_a0, rwkv_w_a2,
              rwkv_k_k, rwkv_k_a, rwkv_r_k, rwkv_ln_g, rwkv_ln_b, rwkv_w_out, final_g):
    for i in range(DEPTH):
        h = rms_norm(x, norm_g[i])
        j = i // N_MIXERS
        if i % N_MIXERS == 0:
            x = x + mla_mixer(h, positions, mla_w_in[j], mla_q_norm[j], mla_w_uq[j],
                              mla_kv_norm[j], mla_w_ukv[j], mla_w_out[j])
        else:
            x = x + rwkv7_mixer(h, rwkv_w_in[j], rwkv_mu[j], rwkv_w0[j], rwkv_w_w2[j],
                                rwkv_a0[j], rwkv_w_a2[j], rwkv_k_k[j], rwkv_k_a[j],
                                rwkv_r_k[j], rwkv_ln_g[j], rwkv_ln_b[j], rwkv_w_out[j])
    return rms_norm(x, final_g)
```

```python
import functools

import jax
import jax.numpy as jnp
from jax import lax
from jax.experimental import pallas as pl
from jax.experimental.pallas import tpu as pltpu

F32 = jnp.float32
BF16 = jnp.bfloat16

NORM_EPS = 1e-6
GN_EPS = 64e-5
ROPE_THETA = 10000.0

MLA_HEADS = 8
QK_NOPE = 128
QK_ROPE = 64
QK_DIM = QK_NOPE + QK_ROPE
V_DIM = 128
Q_LORA = 768
KV_LORA = 256

RWKV_HEAD = 64
LORA = 64

LANES = 128
CHUNK = 64
GROUP_HEADS = 2
GW = GROUP_HEADS * RWKV_HEAD

TOKEN_TILE = 256
ATTN_TILE = 512
WKV_TILE = 256
VMEM_LIMIT = 56 * 1024 * 1024
MASK_VALUE = -1e30


def _params(*sem):
    return pltpu.CompilerParams(dimension_semantics=sem, vmem_limit_bytes=VMEM_LIMIT)


def _rms(x, g):
    return x * lax.rsqrt(jnp.mean(x * x, axis=-1, keepdims=True) + NORM_EPS) * g


def _dot(a, b):
    return jnp.dot(a.astype(BF16), b.astype(BF16), preferred_element_type=F32)


def _dot_nt(a, b):
    return lax.dot_general(a.astype(BF16), b.astype(BF16), (((1,), (1,)), ((), ())),
                           preferred_element_type=F32)


def _silu(x):
    return x / (1.0 + jnp.exp(-x))


def _rope(t, cos, sin_signed, first):
    partner = jnp.where(first, pltpu.roll(t, LANES - QK_ROPE // 2, 1), pltpu.roll(t, QK_ROPE // 2, 1))
    return t * cos + partner * sin_signed


def _mla_proj_kernel(x_ref, pos_ref, ng_ref, win_ref, qn_ref, wuq_ref, kvn_ref, wukv_ref, freq_ref,
                     q_ref, k_ref, v_ref, gate_ref):
    x = x_ref[...]
    h = _rms(x, ng_ref[...])
    proj = _dot(h, win_ref[...])
    q_lat = proj[:, :Q_LORA]
    kv_lat = proj[:, Q_LORA:Q_LORA + KV_LORA]
    gate_ref[...] = proj[:, 1024:2048].astype(gate_ref.dtype)
    k_rope = proj[:, 2048:2048 + LANES]

    q = _dot(_rms(q_lat, qn_ref[...]), wuq_ref[...]) * (QK_DIM ** -0.5)
    kv = _dot(_rms(kv_lat, kvn_ref[...]), wukv_ref[...])

    ang = pos_ref[...] * freq_ref[...]
    cos = jnp.cos(ang)
    sin = jnp.sin(ang)
    lane = lax.broadcasted_iota(jnp.int32, ang.shape, 1)
    first = (lane % QK_ROPE) < (QK_ROPE // 2)
    sin_signed = jnp.where(first, -sin, sin)

    k_rope = _rope(k_rope, cos, sin_signed, first)[:, :QK_ROPE].astype(k_ref.dtype)
    nope_w = MLA_HEADS * QK_NOPE
    for hp in range(MLA_HEADS // 2):
        q_rope = _rope(q[:, nope_w + hp * LANES: nope_w + (hp + 1) * LANES], cos, sin_signed, first)
        for hh in range(2):
            hd = 2 * hp + hh
            q_ref[hd, :, :QK_NOPE] = q[:, hd * QK_NOPE:(hd + 1) * QK_NOPE].astype(q_ref.dtype)
            q_ref[hd, :, QK_NOPE:] = q_rope[:, hh * QK_ROPE:(hh + 1) * QK_ROPE].astype(q_ref.dtype)
    for hd in range(MLA_HEADS):
        k_ref[hd, :, :QK_NOPE] = kv[:, hd * QK_NOPE:(hd + 1) * QK_NOPE].astype(k_ref.dtype)
        k_ref[hd, :, QK_NOPE:] = k_rope
        v_ref[hd] = kv[:, nope_w + hd * V_DIM: nope_w + (hd + 1) * V_DIM].astype(v_ref.dtype)


def _mla_proj(x, pos, ng, w_in, qn, w_uq, kvn, w_ukv, freq):
    t, d = x.shape
    tm = TOKEN_TILE
    const = lambda shape: pl.BlockSpec(shape, lambda i: (0,) * len(shape))
    return pl.pallas_call(
        _mla_proj_kernel,
        grid=(t // tm,),
        in_specs=[pl.BlockSpec((tm, d), lambda i: (i, 0)),
                  pl.BlockSpec((tm, 1), lambda i: (i, 0)),
                  const(ng.shape), const(w_in.shape), const(qn.shape), const(w_uq.shape),
                  const(kvn.shape), const(w_ukv.shape), const(freq.shape)],
        out_specs=[pl.BlockSpec((MLA_HEADS, tm, QK_DIM), lambda i: (0, i, 0)),
                   pl.BlockSpec((MLA_HEADS, tm, QK_DIM), lambda i: (0, i, 0)),
                   pl.BlockSpec((MLA_HEADS, tm, V_DIM), lambda i: (0, i, 0)),
                   pl.BlockSpec((tm, MLA_HEADS * V_DIM), lambda i: (i, 0))],
        out_shape=[jax.ShapeDtypeStruct((MLA_HEADS, t, QK_DIM), BF16),
                   jax.ShapeDtypeStruct((MLA_HEADS, t, QK_DIM), BF16),
                   jax.ShapeDtypeStruct((MLA_HEADS, t, V_DIM), BF16),
                   jax.ShapeDtypeStruct((t, MLA_HEADS * V_DIM), BF16)],
        compiler_params=_params("parallel"),
        name="mla_proj",
    )(x, pos, ng, w_in, qn, w_uq, kvn, w_ukv, freq)


def _attn_kernel(q_ref, k_ref, v_ref, gate_ref, o_ref, *, blk):
    qi = pl.program_id(2)
    q = q_ref[0]

    def step(kb, carry, masked):
        m, l, acc = carry
        start = pl.multiple_of(kb * blk, blk)
        k = k_ref[0, pl.ds(start, blk), :]
        v = v_ref[0, pl.ds(start, blk), :]
        s = _dot_nt(q, k)
        if masked:
            row = lax.broadcasted_iota(jnp.int32, s.shape, 0)
            col = lax.broadcasted_iota(jnp.int32, s.shape, 1)
            s = jnp.where(col <= row, s, MASK_VALUE)
        m_new = jnp.maximum(m, jnp.max(s, axis=-1, keepdims=True))
        p = jnp.exp(s - m_new)
        corr = jnp.exp(m - m_new)
        l_new = l * corr + jnp.sum(p, axis=-1, keepdims=True)
        acc_new = acc * corr + _dot(p, v)
        return m_new, l_new, acc_new

    init = (jnp.full((blk, 1), MASK_VALUE, F32), jnp.zeros((blk, 1), F32),
            jnp.zeros((blk, V_DIM), F32))
    carry = lax.fori_loop(0, qi, lambda kb, c: step(kb, c, False), init)
    _, l, acc = step(qi, carry, True)
    gate = gate_ref[...].astype(F32)
    o_ref[...] = (acc / l * _silu(gate)).astype(o_ref.dtype)


def _attention(q, k, v, gate, bsz, seq):
    blk = ATTN_TILE
    nq = seq // blk
    return pl.pallas_call(
        functools.partial(_attn_kernel, blk=blk),
        grid=(bsz, MLA_HEADS, nq),
        in_specs=[pl.BlockSpec((1, blk, QK_DIM), lambda b, h, i: (h, b * nq + i, 0)),
                  pl.BlockSpec((1, seq, QK_DIM), lambda b, h, i: (h, b, 0)),
                  pl.BlockSpec((1, seq, V_DIM), lambda b, h, i: (h, b, 0)),
                  pl.BlockSpec((blk, V_DIM), lambda b, h, i: (b * nq + i, h))],
        out_specs=pl.BlockSpec((blk, V_DIM), lambda b, h, i: (b * nq + i, h)),
        out_shape=jax.ShapeDtypeStruct((bsz * seq, MLA_HEADS * V_DIM), BF16),
        compiler_params=_params("parallel", "parallel", "arbitrary"),
        name="mla_attention",
    )(q, k, v, gate)


def _out_proj_kernel(y_ref, w_ref, x_ref, g_ref, o_ref, *, final_norm):
    x = x_ref[...] + jnp.dot(y_ref[...], w_ref[...], preferred_element_type=F32)
    if final_norm:
        x = _rms(x, g_ref[...])
    o_ref[...] = x


def _out_proj(y, w, x, final_g, final_norm):
    t, d = x.shape
    tm = TOKEN_TILE
    return pl.pallas_call(
        functools.partial(_out_proj_kernel, final_norm=final_norm),
        grid=(t // tm,),
        in_specs=[pl.BlockSpec((tm, y.shape[1]), lambda i: (i, 0)),
                  pl.BlockSpec(w.shape, lambda i: (0, 0)),
                  pl.BlockSpec((tm, d), lambda i: (i, 0)),
                  pl.BlockSpec(final_g.shape, lambda i: (0, 0))],
        out_specs=pl.BlockSpec((tm, d), lambda i: (i, 0)),
        out_shape=jax.ShapeDtypeStruct((t, d), F32),
        compiler_params=_params("parallel"),
        name="out_proj",
    )(y, w, x, final_g)


def _rwkv_proj_kernel(x_ref, ng_ref, win_ref, mu_ref, w0_ref, ww2_ref, a0_ref, wa2_ref,
                      r_ref, k_ref, v_ref, a_ref, lw_ref, g_ref, prev_ref, *, width):
    @pl.when(pl.program_id(1) == 0)
    def _():
        prev_ref[...] = jnp.zeros_like(prev_ref)

    h = _rms(x_ref[...], ng_ref[...])
    proj = _dot(h, win_ref[...])
    rows = proj.shape[0]
    row = lax.broadcasted_iota(jnp.int32, (rows, 1), 0)
    shifted = jnp.where(row == 0, prev_ref[0:1, :], pltpu.roll(proj, 1, 0))
    prev_ref[0:1, :] = proj[rows - 1:rows, :]
    proj = proj + mu_ref[...] * (shifted - proj)

    lora = proj[:, 4 * width:]
    lane = lax.broadcasted_iota(jnp.int32, lora.shape, 1)
    lora = jnp.where(lane < LORA, jnp.tanh(lora), lora)
    w_arg = w0_ref[...] + _dot(lora, ww2_ref[...])
    neg = -w_arg
    softplus = jnp.maximum(neg, 0.0) + jnp.log(1.0 + jnp.exp(-jnp.abs(neg)))
    log_decay = -jnp.exp(-softplus - 0.5)
    a = 1.0 / (1.0 + jnp.exp(-(a0_ref[...] + _dot(lora, wa2_ref[...]))))

    for gi in range(width // GW):
        sl = slice(gi * GW, (gi + 1) * GW)
        r_ref[gi] = proj[:, sl]
        k_ref[gi] = proj[:, width + gi * GW: width + (gi + 1) * GW]
        v_ref[gi] = proj[:, 2 * width + gi * GW: 2 * width + (gi + 1) * GW]
        g_ref[gi] = proj[:, 3 * width + gi * GW: 3 * width + (gi + 1) * GW].astype(g_ref.dtype)
        a_ref[gi] = a[:, sl]
        lw_ref[gi] = log_decay[:, sl]


def _rwkv_proj(x, ng, w_in, mu, w0, ww2, a0, wa2, bsz, seq):
    t, d = x.shape
    width = w0.shape[1]
    ngroups = width // GW
    tm = TOKEN_TILE
    ns = seq // tm
    const = lambda shape: pl.BlockSpec(shape, lambda b, i: (0,) * len(shape))
    gspec = pl.BlockSpec((ngroups, tm, GW), lambda b, i: (0, b * ns + i, 0))
    gshape = lambda dt: jax.ShapeDtypeStruct((ngroups, t, GW), dt)
    return pl.pallas_call(
        functools.partial(_rwkv_proj_kernel, width=width),
        grid=(bsz, ns),
        in_specs=[pl.BlockSpec((tm, d), lambda b, i: (b * ns + i, 0)),
                  const(ng.shape), const(w_in.shape), const(mu.shape), const(w0.shape),
                  const(ww2.shape), const(a0.shape), const(wa2.shape)],
        out_specs=[gspec] * 6,
        out_shape=[gshape(F32)] * 5 + [gshape(BF16)],
        scratch_shapes=[pltpu.VMEM((8, w_in.shape[1]), F32)],
        compiler_params=_params("parallel", "arbitrary"),
        name="rwkv_proj",
    )(x, ng, w_in, mu, w0, ww2, a0, wa2)


def _block_diag(y, bd_mask):
    return jnp.where(bd_mask, jnp.concatenate([y] * GROUP_HEADS, axis=0), 0.0)


def _split3(x):
    hi = x.astype(BF16)
    r1 = x - hi.astype(F32)
    mid = r1.astype(BF16)
    lo = (r1 - mid.astype(F32)).astype(BF16)
    return hi, mid, lo


def _wkv_kernel(r_ref, k_ref, v_ref, a_ref, lw_ref, g_ref, kk_ref, ka_ref, rk_ref, lng_ref, lnb_ref,
                y_ref, state_ref, *, ngroups, nchunks):
    @pl.when(pl.program_id(1) == 0)
    def _():
        state_ref[...] = jnp.zeros_like(state_ref)

    t_idx = lax.broadcasted_iota(jnp.int32, (CHUNK, GW), 0)
    s_idx = lax.broadcasted_iota(jnp.int32, (CHUNK, GW), 1) % CHUNK
    incl = s_idx <= t_idx
    strict = s_idx < t_idx
    eye = (s_idx == t_idx).astype(F32)
    blk8 = strict & ((s_idx // 8) == (t_idx // 8))
    off_masks = [((s_idx // (2 * b)) == (t_idx // (2 * b))) & ((t_idx // b) % 2 == 1)
                 & ((s_idx // b) % 2 == 0) for b in (8, 16, 32)]
    bd_row = lax.broadcasted_iota(jnp.int32, (GW, GW), 0) // RWKV_HEAD
    bd_col = lax.broadcasted_iota(jnp.int32, (GW, GW), 1) // RWKV_HEAD
    bd_mask = bd_row == bd_col
    bd_ones = bd_mask.astype(BF16)
    tri = (lax.broadcasted_iota(jnp.int32, (CHUNK, CHUNK), 1)
           <= lax.broadcasted_iota(jnp.int32, (CHUNK, CHUNK), 0)).astype(BF16)

    def seg_sum(x):
        hi = x.astype(BF16)
        lo = (x - hi.astype(F32)).astype(BF16)
        return (jnp.dot(hi, bd_ones, preferred_element_type=F32)
                + jnp.dot(lo, bd_ones, preferred_element_type=F32))

    def mm(x, y):
        return _dot(x, _block_diag(y, bd_mask))

    def chunk_body(c, carry):
        t0 = pl.multiple_of(c * CHUNK, CHUNK)
        for gi in range(ngroups):
            rows = pl.ds(t0, CHUNK)
            r = r_ref[gi, rows, :]
            k = k_ref[gi, rows, :]
            v = v_ref[gi, rows, :]
            a = a_ref[gi, rows, :]
            lw = lw_ref[gi, rows, :]

            hi, mid, lo = _split3(lw)
            cum = (jnp.dot(tri, hi, preferred_element_type=F32)
                   + jnp.dot(tri, mid, preferred_element_type=F32)
                   + jnp.dot(tri, lo, preferred_element_type=F32))
            cum_last = cum[CHUNK - 1:CHUNK, :]
            w_inc = jnp.exp(cum)
            w_prev = jnp.exp(cum - lw)
            w_inv = jnp.exp(-cum)
            w_tail = jnp.exp(cum_last - cum)
            w_last = jnp.exp(cum_last)

            kk = k * kk_ref[gi]
            kk = kk / jnp.maximum(jnp.sqrt(seg_sum(kk * kk)), 1e-12)
            kp = k * (1.0 + (a - 1.0) * ka_ref[gi])
            b = kk * a
            r_t = r * w_inc
            a_t = -kk * w_prev
            b_t = b * w_inv
            k_t = kp * w_inv

            lhs = jnp.concatenate([r_t, a_t], axis=0)
            rhs = jnp.concatenate([_block_diag(b_t, bd_mask), _block_diag(k_t, bd_mask)], axis=0)
            amat = _dot_nt(lhs, rhs)
            a_rb = jnp.where(incl, amat[:CHUNK, :GW], 0.0)
            a_rk = jnp.where(incl, amat[:CHUNK, GW:], 0.0)
            m_ab = jnp.where(strict, amat[CHUNK:, :GW], 0.0)
            a_ak = jnp.where(strict, amat[CHUNK:, GW:], 0.0)

            n1 = jnp.where(blk8, m_ab, 0.0)
            n2 = mm(n1, n1)
            n4 = mm(n2, n2)
            tmat = eye + n1
            tmat = tmat + mm(tmat, n2)
            tmat = tmat + mm(tmat, n4)
            for off in off_masks:
                tmat = tmat + mm(tmat, mm(jnp.where(off, m_ab, 0.0), tmat))

            akv = mm(a_ak, v)
            pq = _dot(tmat, jnp.concatenate([_block_diag(a_t, bd_mask), _block_diag(akv, bd_mask)],
                                            axis=1))
            p_mat = pq[:, :GW]
            q_mat = pq[:, GW:]

            state = state_ref[gi]
            us = _dot_nt(jnp.concatenate([p_mat, r_t], axis=0), state)
            u = us[:CHUNK] + q_mat
            y = us[CHUNK:] + _dot(jnp.concatenate([a_rb, a_rk], axis=1),
                                  jnp.concatenate([_block_diag(u, bd_mask), _block_diag(v, bd_mask)],
                                                  axis=0))
            uv_t = jnp.concatenate([u, v], axis=0).T
            upd = _dot(uv_t, jnp.concatenate([b * w_tail, kp * w_tail], axis=0))
            state_ref[gi] = state * w_last + jnp.where(bd_mask, upd, 0.0)

            mean = seg_sum(y) * (1.0 / RWKV_HEAD)
            yc = y - mean
            var = seg_sum(yc * yc) * (1.0 / RWKV_HEAD)
            yn = yc * lax.rsqrt(var + GN_EPS) * lng_ref[gi] + lnb_ref[gi]
            bonus = seg_sum(r * kp * rk_ref[gi])
            gate = g_ref[gi, rows, :].astype(F32)
            out = (yn + bonus * v) * _silu(gate)
            y_ref[rows, gi * GW:(gi + 1) * GW] = out.astype(y_ref.dtype)
        return carry

    lax.fori_loop(0, nchunks, chunk_body, 0)


def _wkv(r, k, v, a, lw, g, kk, ka, rk, lng, lnb, bsz, seq):
    ngroups, t, _ = r.shape
    tt = WKV_TILE
    ns = seq // tt
    gspec = pl.BlockSpec((ngroups, tt, GW), lambda b, i: (0, b * ns + i, 0))
    pspec = pl.BlockSpec((ngroups, 1, GW), lambda b, i: (0, 0, 0))
    return pl.pallas_call(
        functools.partial(_wkv_kernel, ngroups=ngroups, nchunks=tt // CHUNK),
        grid=(bsz, ns),
        in_specs=[gspec] * 6 + [pspec] * 5,
        out_specs=pl.BlockSpec((tt, ngroups * GW), lambda b, i: (b * ns + i, 0)),
        out_shape=jax.ShapeDtypeStruct((t, ngroups * GW), BF16),
        scratch_shapes=[pltpu.VMEM((ngroups, GW, GW), F32)],
        compiler_params=_params("parallel", "arbitrary"),
        name="rwkv_wkv",
    )(r, k, v, a, lw, g, kk, ka, rk, lng, lnb)


def _row(v):
    return v.reshape(1, -1).astype(F32)


def _grouped(v):
    return v.reshape(-1, 1, GW).astype(F32)


def kernel(x, positions, norm_g, mla_w_in, mla_q_norm, mla_w_uq, mla_kv_norm, mla_w_ukv, mla_w_out, rwkv_w_in, rwkv_mu, rwkv_w0, rwkv_w_w2, rwkv_a0, rwkv_w_a2, rwkv_k_k, rwkv_k_a, rwkv_r_k, rwkv_ln_g, rwkv_ln_b, rwkv_w_out, final_g):
    bsz, seq, d = x.shape
    depth = norm_g.shape[0]
    t = bsz * seq
    xf = x.reshape(t, d)
    pos = positions.reshape(t, 1).astype(F32)
    half = QK_ROPE // 2
    inv_freq = 1.0 / (ROPE_THETA ** (jnp.arange(half, dtype=F32) / half))
    freq = jnp.tile(inv_freq, LANES // half).reshape(1, LANES)
    final_row = _row(final_g)

    for i in range(depth):
        j = i // 2
        ng = _row(norm_g[i])
        last = i == depth - 1
        if i % 2 == 0:
            w_in = mla_w_in[j]
            lat = Q_LORA + KV_LORA
            w_in = jnp.concatenate(
                [w_in[:, :lat], w_in[:, lat + QK_ROPE:], w_in[:, lat:lat + QK_ROPE],
                 jnp.zeros((d, LANES - QK_ROPE), w_in.dtype)], axis=1).astype(BF16)
            w_uq = mla_w_uq[j].reshape(Q_LORA, MLA_HEADS, QK_DIM)
            w_uq = jnp.concatenate([w_uq[:, :, :QK_NOPE].reshape(Q_LORA, -1),
                                    w_uq[:, :, QK_NOPE:].reshape(Q_LORA, -1)], axis=1).astype(BF16)
            w_ukv = mla_w_ukv[j].reshape(KV_LORA, MLA_HEADS, QK_NOPE + V_DIM)
            w_ukv = jnp.concatenate([w_ukv[:, :, :QK_NOPE].reshape(KV_LORA, -1),
                                     w_ukv[:, :, QK_NOPE:].reshape(KV_LORA, -1)], axis=1).astype(BF16)
            q, k, v, gate = _mla_proj(xf, pos, ng, w_in, _row(mla_q_norm[j]), w_uq,
                                      _row(mla_kv_norm[j]), w_ukv, freq)
            y = _attention(q, k, v, gate, bsz, seq)
            w_out = mla_w_out[j].astype(BF16)
        else:
            zeros = jnp.zeros((LORA, rwkv_w_w2.shape[2]), F32)
            ww2 = jnp.concatenate([rwkv_w_w2[j], zeros], axis=0).astype(BF16)
            wa2 = jnp.concatenate([zeros, rwkv_w_a2[j]], axis=0).astype(BF16)
            r, k, v, a, lw, g = _rwkv_proj(xf, ng, rwkv_w_in[j].astype(BF16), _row(rwkv_mu[j]),
                                           _row(rwkv_w0[j]), ww2, _row(rwkv_a0[j]), wa2, bsz, seq)
            y = _wkv(r, k, v, a, lw, g, _grouped(rwkv_k_k[j]), _grouped(rwkv_k_a[j]),
                     _grouped(rwkv_r_k[j]), _grouped(rwkv_ln_g[j]), _grouped(rwkv_ln_b[j]),
                     bsz, seq)
            w_out = rwkv_w_out[j].astype(BF16)
        xf = _out_proj(y, w_out, xf, final_row, last)
    return xf.reshape(bsz, seq, d)
```

```python
import functools

import jax
import jax.numpy as jnp
from jax import lax
from jax.experimental import pallas as pl
from jax.experimental.pallas import tpu as pltpu

F32 = jnp.float32
BF16 = jnp.bfloat16

NORM_EPS = 1e-6
GN_EPS = 64e-5
ROPE_THETA = 10000.0

MLA_HEADS = 8
QK_NOPE = 128
QK_ROPE = 64
QK_DIM = QK_NOPE + QK_ROPE
V_DIM = 128
Q_LORA = 768
KV_LORA = 256

RWKV_HEAD = 64
LORA = 64

LANES = 128
CHUNK = 64
GROUP_HEADS = 2
GW = GROUP_HEADS * RWKV_HEAD

TOKEN_TILE = 256
ATTN_TILE = 512
WKV_TILE = 256
VMEM_LIMIT = 56 * 1024 * 1024
MASK_VALUE = -1e30


def _params(*sem):
    return pltpu.CompilerParams(dimension_semantics=sem, vmem_limit_bytes=VMEM_LIMIT)


def _rms(x, g):
    return x * lax.rsqrt(jnp.mean(x * x, axis=-1, keepdims=True) + NORM_EPS) * g


def _dot(a, b):
    return jnp.dot(a.astype(BF16), b.astype(BF16), preferred_element_type=F32)


def _dot_nt(a, b):
    return lax.dot_general(a.astype(BF16), b.astype(BF16), (((1,), (1,)), ((), ())),
                           preferred_element_type=F32)


def _silu(x):
    return x / (1.0 + jnp.exp(-x))


def _rope(t, cos, sin_signed, first):
    partner = jnp.where(first, pltpu.roll(t, LANES - QK_ROPE // 2, 1), pltpu.roll(t, QK_ROPE // 2, 1))
    return t * cos + partner * sin_signed


def _mla_proj_kernel(x_ref, pos_ref, ng_ref, win_ref, qn_ref, wuq_ref, kvn_ref, wukv_ref, freq_ref,
                     q_ref, k_ref, v_ref, gate_ref):
    x = x_ref[...]
    h = _rms(x, ng_ref[...])
    proj = _dot(h, win_ref[...])
    q_lat = proj[:, :Q_LORA]
    kv_lat = proj[:, Q_LORA:Q_LORA + KV_LORA]
    gate_ref[...] = proj[:, 1024:2048].astype(gate_ref.dtype)
    k_rope = proj[:, 2048:2048 + LANES]

    q = _dot(_rms(q_lat, qn_ref[...]), wuq_ref[...]) * (QK_DIM ** -0.5)
    kv = _dot(_rms(kv_lat, kvn_ref[...]), wukv_ref[...])

    ang = pos_ref[...] * freq_ref[...]
    cos = jnp.cos(ang)
    sin = jnp.sin(ang)
    lane = lax.broadcasted_iota(jnp.int32, ang.shape, 1)
    first = (lane % QK_ROPE) < (QK_ROPE // 2)
    sin_signed = jnp.where(first, -sin, sin)

    k_rope = _rope(k_rope, cos, sin_signed, first)[:, :QK_ROPE].astype(k_ref.dtype)
    nope_w = MLA_HEADS * QK_NOPE
    for hp in range(MLA_HEADS // 2):
        q_rope = _rope(q[:, nope_w + hp * LANES: nope_w + (hp + 1) * LANES], cos, sin_signed, first)
        for hh in range(2):
            hd = 2 * hp + hh
            q_ref[hd, :, :QK_NOPE] = q[:, hd * QK_NOPE:(hd + 1) * QK_NOPE].astype(q_ref.dtype)
            q_ref[hd, :, QK_NOPE:] = q_rope[:, hh * QK_ROPE:(hh + 1) * QK_ROPE].astype(q_ref.dtype)
    for hd in range(MLA_HEADS):
        k_ref[hd, :, :QK_NOPE] = kv[:, hd * QK_NOPE:(hd + 1) * QK_NOPE].astype(k_ref.dtype)
        k_ref[hd, :, QK_NOPE:] = k_rope
        v_ref[hd] = kv[:, nope_w + hd * V_DIM: nope_w + (hd + 1) * V_DIM].astype(v_ref.dtype)


def _mla_proj(x, pos, ng, w_in, qn, w_uq, kvn, w_ukv, freq):
    t, d = x.shape
    tm = TOKEN_TILE
    const = lambda shape: pl.BlockSpec(shape, lambda i: (0,) * len(shape))
    return pl.pallas_call(
        _mla_proj_kernel,
        grid=(t // tm,),
        in_specs=[pl.BlockSpec((tm, d), lambda i: (i, 0)),
                  pl.BlockSpec((tm, 1), lambda i: (i, 0)),
                  const(ng.shape), const(w_in.shape), const(qn.shape), const(w_uq.shape),
                  const(kvn.shape), const(w_ukv.shape), const(freq.shape)],
        out_specs=[pl.BlockSpec((MLA_HEADS, tm, QK_DIM), lambda i: (0, i, 0)),
                   pl.BlockSpec((MLA_HEADS, tm, QK_DIM), lambda i: (0, i, 0)),
                   pl.BlockSpec((MLA_HEADS, tm, V_DIM), lambda i: (0, i, 0)),
                   pl.BlockSpec((tm, MLA_HEADS * V_DIM), lambda i: (i, 0))],
        out_shape=[jax.ShapeDtypeStruct((MLA_HEADS, t, QK_DIM), BF16),
                   jax.ShapeDtypeStruct((MLA_HEADS, t, QK_DIM), BF16),
                   jax.ShapeDtypeStruct((MLA_HEADS, t, V_DIM), BF16),
                   jax.ShapeDtypeStruct((t, MLA_HEADS * V_DIM), BF16)],
        compiler_params=_params("parallel"),
        name="mla_proj",
    )(x, pos, ng, w_in, qn, w_uq, kvn, w_ukv, freq)


def _attn_kernel(q_ref, k_ref, v_ref, gate_ref, o_ref, *, blk):
    qi = pl.program_id(2)
    q = q_ref[0]

    def step(kb, carry, masked):
        m, l, acc = carry
        start = pl.multiple_of(kb * blk, blk)
        k = k_ref[0, pl.ds(start, blk), :]
        v = v_ref[0, pl.ds(start, blk), :]
        s = _dot_nt(q, k)
        if masked:
            row = lax.broadcasted_iota(jnp.int32, s.shape, 0)
            col = lax.broadcasted_iota(jnp.int32, s.shape, 1)
            s = jnp.where(col <= row, s, MASK_VALUE)
        m_new = jnp.maximum(m, jnp.max(s, axis=-1, keepdims=True))
        p = jnp.exp(s - m_new)
        corr = jnp.exp(m - m_new)
        l_new = l * corr + jnp.sum(p, axis=-1, keepdims=True)
        acc_new = acc * corr + _dot(p, v)
        return m_new, l_new, acc_new

    init = (jnp.full((blk, 1), MASK_VALUE, F32), jnp.zeros((blk, 1), F32),
            jnp.zeros((blk, V_DIM), F32))
    carry = lax.fori_loop(0, qi, lambda kb, c: step(kb, c, False), init)
    _, l, acc = step(qi, carry, True)
    gate = gate_ref[...].astype(F32)
    o_ref[...] = (acc / l * _silu(gate)).astype(o_ref.dtype)


def _attention(q, k, v, gate, bsz, seq):
    blk = ATTN_TILE
    nq = seq // blk
    return pl.pallas_call(
        functools.partial(_attn_kernel, blk=blk),
        grid=(bsz, MLA_HEADS, nq),
        in_specs=[pl.BlockSpec((1, blk, QK_DIM), lambda b, h, i: (h, b * nq + i, 0)),
                  pl.BlockSpec((1, seq, QK_DIM), lambda b, h, i: (h, b, 0)),
                  pl.BlockSpec((1, seq, V_DIM), lambda b, h, i: (h, b, 0)),
                  pl.BlockSpec((blk, V_DIM), lambda b, h, i: (b * nq + i, h))],
        out_specs=pl.BlockSpec((blk, V_DIM), lambda b, h, i: (b * nq + i, h)),
        out_shape=jax.ShapeDtypeStruct((bsz * seq, MLA_HEADS * V_DIM), BF16),
        compiler_params=_params("parallel", "parallel", "arbitrary"),
        name="mla_attention",
    )(q, k, v, gate)


def _out_proj_kernel(y_ref, w_ref, x_ref, g_ref, o_ref, *, final_norm):
    x = x_ref[...] + jnp.dot(y_ref[...], w_ref[...], preferred_element_type=F32)
    if final_norm:
        x = _rms(x, g_ref[...])
    o_ref[...] = x


def _out_proj(y, w, x, final_g, final_norm):
    t, d = x.shape
    tm = TOKEN_TILE
    return pl.pallas_call(
        functools.partial(_out_proj_kernel, final_norm=final_norm),
        grid=(t // tm,),
        in_specs=[pl.BlockSpec((tm, y.shape[1]), lambda i: (i, 0)),
                  pl.BlockSpec(w.shape, lambda i: (0, 0)),
                  pl.BlockSpec((tm, d), lambda i: (i, 0)),
                  pl.BlockSpec(final_g.shape, lambda i: (0, 0))],
        out_specs=pl.BlockSpec((tm, d), lambda i: (i, 0)),
        out_shape=jax.ShapeDtypeStruct((t, d), F32),
        compiler_params=_params("parallel"),
        name="out_proj",
    )(y, w, x, final_g)


def _rwkv_proj_kernel(x_ref, ng_ref, win_ref, mu_ref, w0_ref, ww2_ref, a0_ref, wa2_ref,
                      r_ref, k_ref, v_ref, a_ref, lw_ref, g_ref, prev_ref, *, width):
    @pl.when(pl.program_id(1) == 0)
    def _():
        prev_ref[...] = jnp.zeros_like(prev_ref)

    h = _rms(x_ref[...], ng_ref[...])
    proj = _dot(h, win_ref[...])
    rows = proj.shape[0]
    row = lax.broadcasted_iota(jnp.int32, (rows, 1), 0)
    shifted = jnp.where(row == 0, prev_ref[0:1, :], pltpu.roll(proj, 1, 0))
    prev_ref[0:1, :] = proj[rows - 1:rows, :]
    proj = proj + mu_ref[...] * (shifted - proj)

    lora = proj[:, 4 * width:]
    lane = lax.broadcasted_iota(jnp.int32, lora.shape, 1)
    lora = jnp.where(lane < LORA, jnp.tanh(lora), lora)
    w_arg = w0_ref[...] + _dot(lora, ww2_ref[...])
    neg = -w_arg
    softplus = jnp.maximum(neg, 0.0) + jnp.log(1.0 + jnp.exp(-jnp.abs(neg)))
    log_decay = -jnp.exp(-softplus - 0.5)
    a = 1.0 / (1.0 + jnp.exp(-(a0_ref[...] + _dot(lora, wa2_ref[...]))))

    for gi in range(width // GW):
        sl = slice(gi * GW, (gi + 1) * GW)
        r_ref[gi] = proj[:, sl]
        k_ref[gi] = proj[:, width + gi * GW: width + (gi + 1) * GW]
        v_ref[gi] = proj[:, 2 * width + gi * GW: 2 * width + (gi + 1) * GW]
        g_ref[gi] = proj[:, 3 * width + gi * GW: 3 * width + (gi + 1) * GW].astype(g_ref.dtype)
        a_ref[gi] = a[:, sl]
        lw_ref[gi] = log_decay[:, sl]


def _rwkv_proj(x, ng, w_in, mu, w0, ww2, a0, wa2, bsz, seq):
    t, d = x.shape
    width = w0.shape[1]
    ngroups = width // GW
    tm = TOKEN_TILE
    ns = seq // tm
    const = lambda shape: pl.BlockSpec(shape, lambda b, i: (0,) * len(shape))
    gspec = pl.BlockSpec((ngroups, tm, GW), lambda b, i: (0, b * ns + i, 0))
    gshape = lambda dt: jax.ShapeDtypeStruct((ngroups, t, GW), dt)
    return pl.pallas_call(
        functools.partial(_rwkv_proj_kernel, width=width),
        grid=(bsz, ns),
        in_specs=[pl.BlockSpec((tm, d), lambda b, i: (b * ns + i, 0)),
                  const(ng.shape), const(w_in.shape), const(mu.shape), const(w0.shape),
                  const(ww2.shape), const(a0.shape), const(wa2.shape)],
        out_specs=[gspec] * 6,
        out_shape=[gshape(F32)] * 5 + [gshape(BF16)],
        scratch_shapes=[pltpu.VMEM((8, w_in.shape[1]), F32)],
        compiler_params=_params("parallel", "arbitrary"),
        name="rwkv_proj",
    )(x, ng, w_in, mu, w0, ww2, a0, wa2)


def _block_diag(y, bd_mask):
    return jnp.where(bd_mask, jnp.concatenate([y] * GROUP_HEADS, axis=0), 0.0)


def _split3(x):
    hi = x.astype(BF16)
    r1 = x - hi.astype(F32)
    mid = r1.astype(BF16)
    lo = (r1 - mid.astype(F32)).astype(BF16)
    return hi, mid, lo


def _wkv_kernel(r_ref, k_ref, v_ref, a_ref, lw_ref, g_ref, kk_ref, ka_ref, rk_ref, lng_ref, lnb_ref,
                y_ref, state_ref, *, ngroups, nchunks):
    @pl.when(pl.program_id(1) == 0)
    def _():
        state_ref[...] = jnp.zeros_like(state_ref)

    t_idx = lax.broadcasted_iota(jnp.int32, (CHUNK, GW), 0)
    s_idx = lax.broadcasted_iota(jnp.int32, (CHUNK, GW), 1) % CHUNK
    incl = s_idx <= t_idx
    strict = s_idx < t_idx
    eye = (s_idx == t_idx).astype(F32)
    blk8 = strict & ((s_idx // 8) == (t_idx // 8))
    off_masks = [((s_idx // (2 * b)) == (t_idx // (2 * b))) & ((t_idx // b) % 2 == 1)
                 & ((s_idx // b) % 2 == 0) for b in (8, 16, 32)]
    bd_row = lax.broadcasted_iota(jnp.int32, (GW, GW), 0) // RWKV_HEAD
    bd_col = lax.broadcasted_iota(jnp.int32, (GW, GW), 1) // RWKV_HEAD
    bd_mask = bd_row == bd_col
    bd_ones = bd_mask.astype(BF16)
    tri = (lax.broadcasted_iota(jnp.int32, (CHUNK, CHUNK), 1)
           <= lax.broadcasted_iota(jnp.int32, (CHUNK, CHUNK), 0)).astype(BF16)

    def seg_sum(x):
        hi = x.astype(BF16)
        lo = (x - hi.astype(F32)).astype(BF16)
        return (jnp.dot(hi, bd_ones, preferred_element_type=F32)
                + jnp.dot(lo, bd_ones, preferred_element_type=F32))

    def mm(x, y):
        return _dot(x, _block_diag(y, bd_mask))

    def chunk_body(c, carry):
        t0 = pl.multiple_of(c * CHUNK, CHUNK)
        rows = pl.ds(t0, CHUNK)
        gs = range(ngroups)
        r = [r_ref[gi, rows, :] for gi in gs]
        k = [k_ref[gi, rows, :] for gi in gs]
        v = [v_ref[gi, rows, :] for gi in gs]
        a = [a_ref[gi, rows, :] for gi in gs]
        lw = [lw_ref[gi, rows, :] for gi in gs]

        parts = [_split3(x) for x in lw]
        cum = [jnp.dot(tri, hi, preferred_element_type=F32)
               + jnp.dot(tri, mid, preferred_element_type=F32)
               + jnp.dot(tri, lo, preferred_element_type=F32) for hi, mid, lo in parts]
        cum_last = [x[CHUNK - 1:CHUNK, :] for x in cum]
        w_inc = [jnp.exp(x) for x in cum]
        w_prev = [jnp.exp(x - y) for x, y in zip(cum, lw)]
        w_inv = [jnp.exp(-x) for x in cum]
        w_tail = [jnp.exp(x - y) for x, y in zip(cum_last, cum)]
        w_last = [jnp.exp(x) for x in cum_last]

        kk = [k[gi] * kk_ref[gi] for gi in gs]
        kk_norm = [seg_sum(x * x) for x in kk]
        kk = [x / jnp.maximum(jnp.sqrt(n), 1e-12) for x, n in zip(kk, kk_norm)]
        kp = [k[gi] * (1.0 + (a[gi] - 1.0) * ka_ref[gi]) for gi in gs]
        b = [x * y for x, y in zip(kk, a)]
        r_t = [x * y for x, y in zip(r, w_inc)]
        a_t = [-x * y for x, y in zip(kk, w_prev)]
        b_t = [x * y for x, y in zip(b, w_inv)]
        k_t = [x * y for x, y in zip(kp, w_inv)]

        amat = [_dot_nt(jnp.concatenate([r_t[gi], a_t[gi]], axis=0),
                        jnp.concatenate([_block_diag(b_t[gi], bd_mask),
                                         _block_diag(k_t[gi], bd_mask)], axis=0)) for gi in gs]
        a_rb = [jnp.where(incl, x[:CHUNK, :GW], 0.0) for x in amat]
        a_rk = [jnp.where(incl, x[:CHUNK, GW:], 0.0) for x in amat]
        m_ab = [jnp.where(strict, x[CHUNK:, :GW], 0.0) for x in amat]
        a_ak = [jnp.where(strict, x[CHUNK:, GW:], 0.0) for x in amat]

        n1 = [jnp.where(blk8, x, 0.0) for x in m_ab]
        n2 = [mm(x, x) for x in n1]
        n4 = [mm(x, x) for x in n2]
        tmat = [eye + x for x in n1]
        tmat = [x + mm(x, y) for x, y in zip(tmat, n2)]
        tmat = [x + mm(x, y) for x, y in zip(tmat, n4)]
        for off in off_masks:
            z = [mm(jnp.where(off, m, 0.0), x) for m, x in zip(m_ab, tmat)]
            tmat = [x + mm(x, y) for x, y in zip(tmat, z)]

        akv = [mm(x, y) for x, y in zip(a_ak, v)]
        pq = [_dot(tmat[gi], jnp.concatenate([_block_diag(a_t[gi], bd_mask),
                                              _block_diag(akv[gi], bd_mask)], axis=1)) for gi in gs]

        state = [state_ref[gi] for gi in gs]
        us = [_dot_nt(jnp.concatenate([pq[gi][:, :GW], r_t[gi]], axis=0), state[gi]) for gi in gs]
        u = [us[gi][:CHUNK] + pq[gi][:, GW:] for gi in gs]
        y = [us[gi][CHUNK:]
             + _dot(jnp.concatenate([a_rb[gi], a_rk[gi]], axis=1),
                    jnp.concatenate([_block_diag(u[gi], bd_mask), _block_diag(v[gi], bd_mask)],
                                    axis=0)) for gi in gs]
        upd = [_dot(jnp.concatenate([u[gi], v[gi]], axis=0).T,
                    jnp.concatenate([b[gi] * w_tail[gi], kp[gi] * w_tail[gi]], axis=0)) for gi in gs]
        for gi in gs:
            state_ref[gi] = state[gi] * w_last[gi] + jnp.where(bd_mask, upd[gi], 0.0)

        mean = [seg_sum(x) * (1.0 / RWKV_HEAD) for x in y]
        yc = [x - m for x, m in zip(y, mean)]
        var = [seg_sum(x * x) * (1.0 / RWKV_HEAD) for x in yc]
        bonus = [seg_sum(r[gi] * kp[gi] * rk_ref[gi]) for gi in gs]
        for gi in gs:
            yn = yc[gi] * lax.rsqrt(var[gi] + GN_EPS) * lng_ref[gi] + lnb_ref[gi]
            gate = g_ref[gi, rows, :].astype(F32)
            out = (yn + bonus[gi] * v[gi]) * _silu(gate)
            y_ref[rows, gi * GW:(gi + 1) * GW] = out.astype(y_ref.dtype)
        return carry

    lax.fori_loop(0, nchunks, chunk_body, 0)


def _wkv(r, k, v, a, lw, g, kk, ka, rk, lng, lnb, bsz, seq):
    ngroups, t, _ = r.shape
    tt = WKV_TILE
    ns = seq // tt
    gspec = pl.BlockSpec((ngroups, tt, GW), lambda b, i: (0, b * ns + i, 0))
    pspec = pl.BlockSpec((ngroups, 1, GW), lambda b, i: (0, 0, 0))
    return pl.pallas_call(
        functools.partial(_wkv_kernel, ngroups=ngroups, nchunks=tt // CHUNK),
        grid=(bsz, ns),
        in_specs=[gspec] * 6 + [pspec] * 5,
        out_specs=pl.BlockSpec((tt, ngroups * GW), lambda b, i: (b * ns + i, 0)),
        out_shape=jax.ShapeDtypeStruct((t, ngroups * GW), BF16),
        scratch_shapes=[pltpu.VMEM((ngroups, GW, GW), F32)],
        compiler_params=_params("parallel", "arbitrary"),
        name="rwkv_wkv",
    )(r, k, v, a, lw, g, kk, ka, rk, lng, lnb)


def _row(v):
    return v.reshape(1, -1).astype(F32)


def _grouped(v):
    return v.reshape(-1, 1, GW).astype(F32)


def kernel(x, positions, norm_g, mla_w_in, mla_q_norm, mla_w_uq, mla_kv_norm, mla_w_ukv, mla_w_out, rwkv_w_in, rwkv_mu, rwkv_w0, rwkv_w_w2, rwkv_a0, rwkv_w_a2, rwkv_k_k, rwkv_k_a, rwkv_r_k, rwkv_ln_g, rwkv_ln_b, rwkv_w_out, final_g):
    bsz, seq, d = x.shape
    depth = norm_g.shape[0]
    t = bsz * seq
    xf = x.reshape(t, d)
    pos = positions.reshape(t, 1).astype(F32)
    half = QK_ROPE // 2
    inv_freq = 1.0 / (ROPE_THETA ** (jnp.arange(half, dtype=F32) / half))
    freq = jnp.tile(inv_freq, LANES // half).reshape(1, LANES)
    final_row = _row(final_g)

    for i in range(depth):
        j = i // 2
        ng = _row(norm_g[i])
        last = i == depth - 1
        if i % 2 == 0:
            w_in = mla_w_in[j]
            lat = Q_LORA + KV_LORA
            w_in = jnp.concatenate(
                [w_in[:, :lat], w_in[:, lat + QK_ROPE:], w_in[:, lat:lat + QK_ROPE],
                 jnp.zeros((d, LANES - QK_ROPE), w_in.dtype)], axis=1).astype(BF16)
            w_uq = mla_w_uq[j].reshape(Q_LORA, MLA_HEADS, QK_DIM)
            w_uq = jnp.concatenate([w_uq[:, :, :QK_NOPE].reshape(Q_LORA, -1),
                                    w_uq[:, :, QK_NOPE:].reshape(Q_LORA, -1)], axis=1).astype(BF16)
            w_ukv = mla_w_ukv[j].reshape(KV_LORA, MLA_HEADS, QK_NOPE + V_DIM)
            w_ukv = jnp.concatenate([w_ukv[:, :, :QK_NOPE].reshape(KV_LORA, -1),
                                     w_ukv[:, :, QK_NOPE:].reshape(KV_LORA, -1)], axis=1).astype(BF16)
            q, k, v, gate = _mla_proj(xf, pos, ng, w_in, _row(mla_q_norm[j]), w_uq,
                                      _row(mla_kv_norm[j]), w_ukv, freq)
            y = _attention(q, k, v, gate, bsz, seq)
            w_out = mla_w_out[j].astype(BF16)
        else:
            zeros = jnp.zeros((LORA, rwkv_w_w2.shape[2]), F32)
            ww2 = jnp.concatenate([rwkv_w_w2[j], zeros], axis=0).astype(BF16)
            wa2 = jnp.concatenate([zeros, rwkv_w_a2[j]], axis=0).astype(BF16)
            r, k, v, a, lw, g = _rwkv_proj(xf, ng, rwkv_w_in[j].astype(BF16), _row(rwkv_mu[j]),
                                           _row(rwkv_w0[j]), ww2, _row(rwkv_a0[j]), wa2, bsz, seq)
            y = _wkv(r, k, v, a, lw, g, _grouped(rwkv_k_k[j]), _grouped(rwkv_k_a[j]),
                     _grouped(rwkv_r_k[j]), _grouped(rwkv_ln_g[j]), _grouped(rwkv_ln_b[j]),
                     bsz, seq)
            w_out = rwkv_w_out[j].astype(BF16)
        xf = _out_proj(y, w_out, xf, final_row, last)
    return xf.reshape(bsz, seq, d)
```

```python
import functools

import jax
import jax.numpy as jnp
from jax import lax
from jax.experimental import pallas as pl
from jax.experimental.pallas import tpu as pltpu

F32 = jnp.float32
BF16 = jnp.bfloat16

NORM_EPS = 1e-6
GN_EPS = 64e-5
ROPE_THETA = 10000.0

MLA_HEADS = 8
QK_NOPE = 128
QK_ROPE = 64
QK_DIM = QK_NOPE + QK_ROPE
V_DIM = 128
Q_LORA = 768
KV_LORA = 256

RWKV_HEAD = 64
LORA = 64

LANES = 128
CHUNK = 64
GROUP_HEADS = 2
GW = GROUP_HEADS * RWKV_HEAD

TOKEN_TILE = 256
ATTN_TILE = 512
WKV_TILE = 256
VMEM_LIMIT = 56 * 1024 * 1024
MASK_VALUE = -1e30
LOG2E = 1.4426950408889634


def _params(*sem):
    return pltpu.CompilerParams(dimension_semantics=sem, vmem_limit_bytes=VMEM_LIMIT)


def _rms(x, g):
    return x * lax.rsqrt(jnp.mean(x * x, axis=-1, keepdims=True) + NORM_EPS) * g


def _dot(a, b):
    return jnp.dot(a.astype(BF16), b.astype(BF16), preferred_element_type=F32)


def _dot_nt(a, b):
    return lax.dot_general(a.astype(BF16), b.astype(BF16), (((1,), (1,)), ((), ())),
                           preferred_element_type=F32)


def _silu(x):
    return x / (1.0 + jnp.exp(-x))


def _rope(t, cos, sin_signed, first):
    partner = jnp.where(first, pltpu.roll(t, LANES - QK_ROPE // 2, 1), pltpu.roll(t, QK_ROPE // 2, 1))
    return t * cos + partner * sin_signed


def _mla_proj_kernel(x_ref, pos_ref, ng_ref, win_ref, qn_ref, wuq_ref, kvn_ref, wukv_ref, freq_ref,
                     q_ref, k_ref, v_ref, gate_ref):
    x = x_ref[...]
    h = _rms(x, ng_ref[...])
    proj = _dot(h, win_ref[...])
    q_lat = proj[:, :Q_LORA]
    kv_lat = proj[:, Q_LORA:Q_LORA + KV_LORA]
    gate_ref[...] = proj[:, 1024:2048].astype(gate_ref.dtype)
    k_rope = proj[:, 2048:2048 + LANES]

    q = _dot(_rms(q_lat, qn_ref[...]), wuq_ref[...]) * (QK_DIM ** -0.5 * LOG2E)
    kv = _dot(_rms(kv_lat, kvn_ref[...]), wukv_ref[...])

    ang = pos_ref[...] * freq_ref[...]
    cos = jnp.cos(ang)
    sin = jnp.sin(ang)
    lane = lax.broadcasted_iota(jnp.int32, ang.shape, 1)
    first = (lane % QK_ROPE) < (QK_ROPE // 2)
    sin_signed = jnp.where(first, -sin, sin)

    k_rope = _rope(k_rope, cos, sin_signed, first)[:, :QK_ROPE].astype(k_ref.dtype)
    nope_w = MLA_HEADS * QK_NOPE
    for hp in range(MLA_HEADS // 2):
        q_rope = _rope(q[:, nope_w + hp * LANES: nope_w + (hp + 1) * LANES], cos, sin_signed, first)
        for hh in range(2):
            hd = 2 * hp + hh
            q_ref[hd, :, :QK_NOPE] = q[:, hd * QK_NOPE:(hd + 1) * QK_NOPE].astype(q_ref.dtype)
            q_ref[hd, :, QK_NOPE:] = q_rope[:, hh * QK_ROPE:(hh + 1) * QK_ROPE].astype(q_ref.dtype)
    for hd in range(MLA_HEADS):
        k_ref[hd, :, :QK_NOPE] = kv[:, hd * QK_NOPE:(hd + 1) * QK_NOPE].astype(k_ref.dtype)
        k_ref[hd, :, QK_NOPE:] = k_rope
        v_ref[hd, 0] = kv[:, nope_w + hd * V_DIM: nope_w + (hd + 1) * V_DIM].T.astype(v_ref.dtype)


def _mla_proj(x, pos, ng, w_in, qn, w_uq, kvn, w_ukv, freq):
    t, d = x.shape
    tm = TOKEN_TILE
    const = lambda shape: pl.BlockSpec(shape, lambda i: (0,) * len(shape))
    return pl.pallas_call(
        _mla_proj_kernel,
        grid=(t // tm,),
        in_specs=[pl.BlockSpec((tm, d), lambda i: (i, 0)),
                  pl.BlockSpec((tm, 1), lambda i: (i, 0)),
                  const(ng.shape), const(w_in.shape), const(qn.shape), const(w_uq.shape),
                  const(kvn.shape), const(w_ukv.shape), const(freq.shape)],
        out_specs=[pl.BlockSpec((MLA_HEADS, tm, QK_DIM), lambda i: (0, i, 0)),
                   pl.BlockSpec((MLA_HEADS, tm, QK_DIM), lambda i: (0, i, 0)),
                   pl.BlockSpec((MLA_HEADS, 1, V_DIM, tm), lambda i: (0, i, 0, 0)),
                   pl.BlockSpec((tm, MLA_HEADS * V_DIM), lambda i: (i, 0))],
        out_shape=[jax.ShapeDtypeStruct((MLA_HEADS, t, QK_DIM), BF16),
                   jax.ShapeDtypeStruct((MLA_HEADS, t, QK_DIM), BF16),
                   jax.ShapeDtypeStruct((MLA_HEADS, t // tm, V_DIM, tm), BF16),
                   jax.ShapeDtypeStruct((t, MLA_HEADS * V_DIM), BF16)],
        compiler_params=_params("parallel"),
        name="mla_proj",
    )(x, pos, ng, w_in, qn, w_uq, kvn, w_ukv, freq)


def _attn_kernel(q_ref, k_ref, vt_ref, gate_ref, o_ref, st0_ref, st1_ref, acc_ref, *, blk, sub):
    qi = pl.program_id(2)
    q = q_ref[0]
    nsub = blk // sub

    def scores(kb, st_ref):
        start = pl.multiple_of(kb * blk, blk)
        st = _dot_nt(k_ref[0, pl.ds(start, blk), :], q)
        st_ref[...] = st
        return jnp.max(st, axis=0, keepdims=True)

    def update(kb, st_ref, st_max, carry, masked):
        m, l = carry
        vt = jnp.concatenate([vt_ref[0, kb * nsub + j] for j in range(nsub)], axis=1)
        st = st_ref[...]
        if masked:
            key = lax.broadcasted_iota(jnp.int32, st.shape, 0)
            qry = lax.broadcasted_iota(jnp.int32, st.shape, 1)
            st = jnp.where(key <= qry, st, MASK_VALUE)
            st_max = jnp.max(st, axis=0, keepdims=True)
        m_new = jnp.maximum(m, st_max)
        pt = jnp.exp2(st - m_new)
        corr = jnp.exp2(m - m_new)
        l_new = l * corr + jnp.sum(pt, axis=0, keepdims=True)
        acc_ref[...] = acc_ref[...] * corr + _dot(vt, pt)
        return m_new, l_new

    acc_ref[...] = jnp.zeros_like(acc_ref)
    init = (jnp.full((1, blk), MASK_VALUE, F32), jnp.zeros((1, blk), F32))

    def step(kb, cur_ref, nxt_ref, c):
        nxt_max = scores(kb + 1, nxt_ref)
        return (nxt_max,) + update(kb, cur_ref, c[0], c[1:], False)

    def body(j, c):
        c = step(2 * j, st0_ref, st1_ref, c)
        return step(2 * j + 1, st1_ref, st0_ref, c)

    c = lax.fori_loop(0, qi // 2, body, (scores(0, st0_ref),) + init)

    def tail_even(c):
        return update(qi, st0_ref, c[0], c[1:], True)

    def tail_odd(c):
        c = step(qi - 1, st0_ref, st1_ref, c)
        return update(qi, st1_ref, c[0], c[1:], True)

    _, l = lax.cond(qi % 2 == 0, tail_even, tail_odd, c)
    gate = gate_ref[...].astype(F32)
    o_ref[...] = ((acc_ref[...] / l).T * _silu(gate)).astype(o_ref.dtype)


def _attention(q, k, vt, gate, bsz, seq):
    blk = ATTN_TILE
    sub = vt.shape[3]
    nq = seq // blk
    nv = seq // sub
    return pl.pallas_call(
        functools.partial(_attn_kernel, blk=blk, sub=sub),
        grid=(bsz, MLA_HEADS, nq),
        in_specs=[pl.BlockSpec((1, blk, QK_DIM), lambda b, h, i: (h, b * nq + i, 0)),
                  pl.BlockSpec((1, seq, QK_DIM), lambda b, h, i: (h, b, 0)),
                  pl.BlockSpec((1, nv, V_DIM, sub), lambda b, h, i: (h, b, 0, 0)),
                  pl.BlockSpec((blk, V_DIM), lambda b, h, i: (b * nq + i, h))],
        out_specs=pl.BlockSpec((blk, V_DIM), lambda b, h, i: (b * nq + i, h)),
        out_shape=jax.ShapeDtypeStruct((bsz * seq, MLA_HEADS * V_DIM), BF16),
        scratch_shapes=[pltpu.VMEM((blk, blk), F32), pltpu.VMEM((blk, blk), F32),
                        pltpu.VMEM((V_DIM, blk), F32)],
        compiler_params=_params("parallel", "parallel", "arbitrary"),
        name="mla_attention",
    )(q, k, vt, gate)


def _out_proj_kernel(y_ref, w_ref, x_ref, g_ref, o_ref, *, final_norm):
    x = x_ref[...] + jnp.dot(y_ref[...], w_ref[...], preferred_element_type=F32)
    if final_norm:
        x = _rms(x, g_ref[...])
    o_ref[...] = x


def _out_proj(y, w, x, final_g, final_norm):
    t, d = x.shape
    tm = TOKEN_TILE
    return pl.pallas_call(
        functools.partial(_out_proj_kernel, final_norm=final_norm),
        grid=(t // tm,),
        in_specs=[pl.BlockSpec((tm, y.shape[1]), lambda i: (i, 0)),
                  pl.BlockSpec(w.shape, lambda i: (0, 0)),
                  pl.BlockSpec((tm, d), lambda i: (i, 0)),
                  pl.BlockSpec(final_g.shape, lambda i: (0, 0))],
        out_specs=pl.BlockSpec((tm, d), lambda i: (i, 0)),
        out_shape=jax.ShapeDtypeStruct((t, d), F32),
        compiler_params=_params("parallel"),
        name="out_proj",
    )(y, w, x, final_g)


def _rwkv_proj_kernel(x_ref, ng_ref, win_ref, mu_ref, w0_ref, ww2_ref, a0_ref, wa2_ref,
                      r_ref, k_ref, v_ref, a_ref, lw_ref, g_ref, prev_ref, *, width):
    @pl.when(pl.program_id(1) == 0)
    def _():
        prev_ref[...] = jnp.zeros_like(prev_ref)

    h = _rms(x_ref[...], ng_ref[...])
    proj = _dot(h, win_ref[...])
    rows = proj.shape[0]
    row = lax.broadcasted_iota(jnp.int32, (rows, 1), 0)
    shifted = jnp.where(row == 0, prev_ref[0:1, :], pltpu.roll(proj, 1, 0))
    prev_ref[0:1, :] = proj[rows - 1:rows, :]
    proj = proj + mu_ref[...] * (shifted - proj)

    lora = proj[:, 4 * width:]
    lane = lax.broadcasted_iota(jnp.int32, lora.shape, 1)
    lora = jnp.where(lane < LORA, jnp.tanh(lora), lora)
    w_arg = w0_ref[...] + _dot(lora, ww2_ref[...])
    neg = -w_arg
    softplus = jnp.maximum(neg, 0.0) + jnp.log(1.0 + jnp.exp(-jnp.abs(neg)))
    log_decay = -jnp.exp(-softplus - 0.5)
    a = 1.0 / (1.0 + jnp.exp(-(a0_ref[...] + _dot(lora, wa2_ref[...]))))

    for gi in range(width // GW):
        sl = slice(gi * GW, (gi + 1) * GW)
        r_ref[gi] = proj[:, sl]
        k_ref[gi] = proj[:, width + gi * GW: width + (gi + 1) * GW]
        v_ref[gi] = proj[:, 2 * width + gi * GW: 2 * width + (gi + 1) * GW]
        g_ref[gi] = proj[:, 3 * width + gi * GW: 3 * width + (gi + 1) * GW].astype(g_ref.dtype)
        a_ref[gi] = a[:, sl]
        lw_ref[gi] = log_decay[:, sl]


def _rwkv_proj(x, ng, w_in, mu, w0, ww2, a0, wa2, bsz, seq):
    t, d = x.shape
    width = w0.shape[1]
    ngroups = width // GW
    tm = TOKEN_TILE
    ns = seq // tm
    const = lambda shape: pl.BlockSpec(shape, lambda b, i: (0,) * len(shape))
    gspec = pl.BlockSpec((ngroups, tm, GW), lambda b, i: (0, b * ns + i, 0))
    gshape = lambda dt: jax.ShapeDtypeStruct((ngroups, t, GW), dt)
    return pl.pallas_call(
        functools.partial(_rwkv_proj_kernel, width=width),
        grid=(bsz, ns),
        in_specs=[pl.BlockSpec((tm, d), lambda b, i: (b * ns + i, 0)),
                  const(ng.shape), const(w_in.shape), const(mu.shape), const(w0.shape),
                  const(ww2.shape), const(a0.shape), const(wa2.shape)],
        out_specs=[gspec] * 6,
        out_shape=[gshape(F32)] * 5 + [gshape(BF16)],
        scratch_shapes=[pltpu.VMEM((8, w_in.shape[1]), F32)],
        compiler_params=_params("parallel", "arbitrary"),
        name="rwkv_proj",
    )(x, ng, w_in, mu, w0, ww2, a0, wa2)


def _block_diag(y, bd_mask):
    return jnp.where(bd_mask, jnp.concatenate([y] * GROUP_HEADS, axis=0), 0.0)


def _split3(x):
    hi = x.astype(BF16)
    r1 = x - hi.astype(F32)
    mid = r1.astype(BF16)
    lo = (r1 - mid.astype(F32)).astype(BF16)
    return hi, mid, lo


def _wkv_kernel(r_ref, k_ref, v_ref, a_ref, lw_ref, g_ref, kk_ref, ka_ref, rk_ref, lng_ref, lnb_ref,
                y_ref, state_ref, *, ngroups, nchunks):
    @pl.when(pl.program_id(1) == 0)
    def _():
        state_ref[...] = jnp.zeros_like(state_ref)

    t_idx = lax.broadcasted_iota(jnp.int32, (CHUNK, GW), 0)
    s_idx = lax.broadcasted_iota(jnp.int32, (CHUNK, GW), 1) % CHUNK
    incl = s_idx <= t_idx
    strict = s_idx < t_idx
    eye = (s_idx == t_idx).astype(F32)
    blk8 = strict & ((s_idx // 8) == (t_idx // 8))
    off_masks = [((s_idx // (2 * b)) == (t_idx // (2 * b))) & ((t_idx // b) % 2 == 1)
                 & ((s_idx // b) % 2 == 0) for b in (8, 16, 32)]
    bd_row = lax.broadcasted_iota(jnp.int32, (GW, GW), 0) // RWKV_HEAD
    bd_col = lax.broadcasted_iota(jnp.int32, (GW, GW), 1) // RWKV_HEAD
    bd_mask = bd_row == bd_col
    bd_ones = bd_mask.astype(BF16)
    tri = (lax.broadcasted_iota(jnp.int32, (CHUNK, CHUNK), 1)
           <= lax.broadcasted_iota(jnp.int32, (CHUNK, CHUNK), 0)).astype(BF16)

    def seg_sum(x):
        hi = x.astype(BF16)
        lo = (x - hi.astype(F32)).astype(BF16)
        return (jnp.dot(hi, bd_ones, preferred_element_type=F32)
                + jnp.dot(lo, bd_ones, preferred_element_type=F32))

    def mm(x, y):
        return _dot(x, _block_diag(y, bd_mask))

    def chunk_body(c, carry):
        t0 = pl.multiple_of(c * CHUNK, CHUNK)
        rows = pl.ds(t0, CHUNK)
        gs = range(ngroups)
        r = [r_ref[gi, rows, :] for gi in gs]
        k = [k_ref[gi, rows, :] for gi in gs]
        v = [v_ref[gi, rows, :] for gi in gs]
        a = [a_ref[gi, rows, :] for gi in gs]
        lw = [lw_ref[gi, rows, :] for gi in gs]

        parts = [_split3(x) for x in lw]
        cum = [jnp.dot(tri, hi, preferred_element_type=F32)
               + jnp.dot(tri, mid, preferred_element_type=F32)
               + jnp.dot(tri, lo, preferred_element_type=F32) for hi, mid, lo in parts]
        cum_last = [x[CHUNK - 1:CHUNK, :] for x in cum]
        w_inc = [jnp.exp(x) for x in cum]
        w_prev = [jnp.exp(x - y) for x, y in zip(cum, lw)]
        w_inv = [jnp.exp(-x) for x in cum]
        w_tail = [jnp.exp(x - y) for x, y in zip(cum_last, cum)]
        w_last = [jnp.exp(x) for x in cum_last]

        kk = [k[gi] * kk_ref[gi] for gi in gs]
        kk_norm = [seg_sum(x * x) for x in kk]
        kk = [x / jnp.maximum(jnp.sqrt(n), 1e-12) for x, n in zip(kk, kk_norm)]
        kp = [k[gi] * (1.0 + (a[gi] - 1.0) * ka_ref[gi]) for gi in gs]
        b = [x * y for x, y in zip(kk, a)]
        r_t = [x * y for x, y in zip(r, w_inc)]
        a_t = [-x * y for x, y in zip(kk, w_prev)]
        b_t = [x * y for x, y in zip(b, w_inv)]
        k_t = [x * y for x, y in zip(kp, w_inv)]

        amat = [_dot_nt(jnp.concatenate([r_t[gi], a_t[gi]], axis=0),
                        jnp.concatenate([_block_diag(b_t[gi], bd_mask),
                                         _block_diag(k_t[gi], bd_mask)], axis=0)) for gi in gs]
        a_rb = [jnp.where(incl, x[:CHUNK, :GW], 0.0) for x in amat]
        a_rk = [jnp.where(incl, x[:CHUNK, GW:], 0.0) for x in amat]
        m_ab = [jnp.where(strict, x[CHUNK:, :GW], 0.0) for x in amat]
        a_ak = [jnp.where(strict, x[CHUNK:, GW:], 0.0) for x in amat]

        n1 = [jnp.where(blk8, x, 0.0) for x in m_ab]
        n2 = [mm(x, x) for x in n1]
        n4 = [mm(x, x) for x in n2]
        tmat = [eye + x for x in n1]
        tmat = [x + mm(x, y) for x, y in zip(tmat, n2)]
        tmat = [x + mm(x, y) for x, y in zip(tmat, n4)]
        for off in off_masks:
            z = [mm(jnp.where(off, m, 0.0), x) for m, x in zip(m_ab, tmat)]
            tmat = [x + mm(x, y) for x, y in zip(tmat, z)]

        akv = [mm(x, y) for x, y in zip(a_ak, v)]
        pq = [_dot(tmat[gi], jnp.concatenate([_block_diag(a_t[gi], bd_mask),
                                              _block_diag(akv[gi], bd_mask)], axis=1)) for gi in gs]

        state = [state_ref[gi] for gi in gs]
        us = [_dot_nt(jnp.concatenate([pq[gi][:, :GW], r_t[gi]], axis=0), state[gi]) for gi in gs]
        u = [us[gi][:CHUNK] + pq[gi][:, GW:] for gi in gs]
        y = [us[gi][CHUNK:]
             + _dot(jnp.concatenate([a_rb[gi], a_rk[gi]], axis=1),
                    jnp.concatenate([_block_diag(u[gi], bd_mask), _block_diag(v[gi], bd_mask)],
                                    axis=0)) for gi in gs]
        upd = [_dot(jnp.concatenate([u[gi], v[gi]], axis=0).T,
                    jnp.concatenate([b[gi] * w_tail[gi], kp[gi] * w_tail[gi]], axis=0)) for gi in gs]
        for gi in gs:
            state_ref[gi] = state[gi] * w_last[gi] + jnp.where(bd_mask, upd[gi], 0.0)

        mean = [seg_sum(x) * (1.0 / RWKV_HEAD) for x in y]
        yc = [x - m for x, m in zip(y, mean)]
        var = [seg_sum(x * x) * (1.0 / RWKV_HEAD) for x in yc]
        bonus = [seg_sum(r[gi] * kp[gi] * rk_ref[gi]) for gi in gs]
        for gi in gs:
            yn = yc[gi] * lax.rsqrt(var[gi] + GN_EPS) * lng_ref[gi] + lnb_ref[gi]
            gate = g_ref[gi, rows, :].astype(F32)
            out = (yn + bonus[gi] * v[gi]) * _silu(gate)
            y_ref[rows, gi * GW:(gi + 1) * GW] = out.astype(y_ref.dtype)
        return carry

    lax.fori_loop(0, nchunks, chunk_body, 0)


def _wkv(r, k, v, a, lw, g, kk, ka, rk, lng, lnb, bsz, seq):
    ngroups, t, _ = r.shape
    tt = WKV_TILE
    ns = seq // tt
    gspec = pl.BlockSpec((ngroups, tt, GW), lambda b, i: (0, b * ns + i, 0))
    pspec = pl.BlockSpec((ngroups, 1, GW), lambda b, i: (0, 0, 0))
    return pl.pallas_call(
        functools.partial(_wkv_kernel, ngroups=ngroups, nchunks=tt // CHUNK),
        grid=(bsz, ns),
        in_specs=[gspec] * 6 + [pspec] * 5,
        out_specs=pl.BlockSpec((tt, ngroups * GW), lambda b, i: (b * ns + i, 0)),
        out_shape=jax.ShapeDtypeStruct((t, ngroups * GW), BF16),
        scratch_shapes=[pltpu.VMEM((ngroups, GW, GW), F32)],
        compiler_params=_params("parallel", "arbitrary"),
        name="rwkv_wkv",
    )(r, k, v, a, lw, g, kk, ka, rk, lng, lnb)


def _row(v):
    return v.reshape(1, -1).astype(F32)


def _grouped(v):
    return v.reshape(-1, 1, GW).astype(F32)


def kernel(x, positions, norm_g, mla_w_in, mla_q_norm, mla_w_uq, mla_kv_norm, mla_w_ukv, mla_w_out, rwkv_w_in, rwkv_mu, rwkv_w0, rwkv_w_w2, rwkv_a0, rwkv_w_a2, rwkv_k_k, rwkv_k_a, rwkv_r_k, rwkv_ln_g, rwkv_ln_b, rwkv_w_out, final_g):
    bsz, seq, d = x.shape
    depth = norm_g.shape[0]
    t = bsz * seq
    xf = x.reshape(t, d)
    pos = positions.reshape(t, 1).astype(F32)
    half = QK_ROPE // 2
    inv_freq = 1.0 / (ROPE_THETA ** (jnp.arange(half, dtype=F32) / half))
    freq = jnp.tile(inv_freq, LANES // half).reshape(1, LANES)
    final_row = _row(final_g)

    for i in range(depth):
        j = i // 2
        ng = _row(norm_g[i])
        last = i == depth - 1
        if i % 2 == 0:
            w_in = mla_w_in[j]
            lat = Q_LORA + KV_LORA
            w_in = jnp.concatenate(
                [w_in[:, :lat], w_in[:, lat + QK_ROPE:], w_in[:, lat:lat + QK_ROPE],
                 jnp.zeros((d, LANES - QK_ROPE), w_in.dtype)], axis=1).astype(BF16)
            w_uq = mla_w_uq[j].reshape(Q_LORA, MLA_HEADS, QK_DIM)
            w_uq = jnp.concatenate([w_uq[:, :, :QK_NOPE].reshape(Q_LORA, -1),
                                    w_uq[:, :, QK_NOPE:].reshape(Q_LORA, -1)], axis=1).astype(BF16)
            w_ukv = mla_w_ukv[j].reshape(KV_LORA, MLA_HEADS, QK_NOPE + V_DIM)
            w_ukv = jnp.concatenate([w_ukv[:, :, :QK_NOPE].reshape(KV_LORA, -1),
                                     w_ukv[:, :, QK_NOPE:].reshape(KV_LORA, -1)], axis=1).astype(BF16)
            q, k, v, gate = _mla_proj(xf, pos, ng, w_in, _row(mla_q_norm[j]), w_uq,
                                      _row(mla_kv_norm[j]), w_ukv, freq)
            y = _attention(q, k, v, gate, bsz, seq)
            w_out = mla_w_out[j].astype(BF16)
        else:
            zeros = jnp.zeros((LORA, rwkv_w_w2.shape[2]), F32)
            ww2 = jnp.concatenate([rwkv_w_w2[j], zeros], axis=0).astype(BF16)
            wa2 = jnp.concatenate([zeros, rwkv_w_a2[j]], axis=0).astype(BF16)
            r, k, v, a, lw, g = _rwkv_proj(xf, ng, rwkv_w_in[j].astype(BF16), _row(rwkv_mu[j]),
                                           _row(rwkv_w0[j]), ww2, _row(rwkv_a0[j]), wa2, bsz, seq)
            y = _wkv(r, k, v, a, lw, g, _grouped(rwkv_k_k[j]), _grouped(rwkv_k_a[j]),
                     _grouped(rwkv_r_k[j]), _grouped(rwkv_ln_g[j]), _grouped(rwkv_ln_b[j]),
                     bsz, seq)
            w_out = rwkv_w_out[j].astype(BF16)
        xf = _out_proj(y, w_out, xf, final_row, last)
    return xf.reshape(bsz, seq, d)
```

```python
import functools

import jax
import jax.numpy as jnp
from jax import lax
from jax.experimental import pallas as pl
from jax.experimental.pallas import tpu as pltpu

F32 = jnp.float32
BF16 = jnp.bfloat16

NORM_EPS = 1e-6
GN_EPS = 64e-5
ROPE_THETA = 10000.0

MLA_HEADS = 8
QK_NOPE = 128
QK_ROPE = 64
QK_DIM = QK_NOPE + QK_ROPE
V_DIM = 128
Q_LORA = 768
KV_LORA = 256

RWKV_HEAD = 64
LORA = 64

LANES = 128
CHUNK = 64
GROUP_HEADS = 2
GW = GROUP_HEADS * RWKV_HEAD

TOKEN_TILE = 256
ATTN_TILE = 512
WKV_TILE = 256
VMEM_LIMIT = 56 * 1024 * 1024
MASK_VALUE = -1e30
LOG2E = 1.4426950408889634


def _params(*sem):
    return pltpu.CompilerParams(dimension_semantics=sem, vmem_limit_bytes=VMEM_LIMIT)


def _rms(x, g):
    return x * lax.rsqrt(jnp.mean(x * x, axis=-1, keepdims=True) + NORM_EPS) * g


def _dot(a, b):
    return jnp.dot(a.astype(BF16), b.astype(BF16), preferred_element_type=F32)


def _dot_nt(a, b):
    return lax.dot_general(a.astype(BF16), b.astype(BF16), (((1,), (1,)), ((), ())),
                           preferred_element_type=F32)


def _silu(x):
    return x / (1.0 + jnp.exp(-x))


def _rope(t, cos, sin_signed, first):
    partner = jnp.where(first, pltpu.roll(t, LANES - QK_ROPE // 2, 1), pltpu.roll(t, QK_ROPE // 2, 1))
    return t * cos + partner * sin_signed


def _mla_proj_kernel(x_ref, pos_ref, ng_ref, win_ref, qn_ref, wuq_ref, kvn_ref, wukv_ref, freq_ref,
                     q_ref, k_ref, v_ref, gate_ref):
    x = x_ref[...]
    h = _rms(x, ng_ref[...])
    proj = _dot(h, win_ref[...])
    q_lat = proj[:, :Q_LORA]
    kv_lat = proj[:, Q_LORA:Q_LORA + KV_LORA]
    gate_ref[...] = proj[:, 1024:2048].astype(gate_ref.dtype)
    k_rope = proj[:, 2048:2048 + LANES]

    q = _dot(_rms(q_lat, qn_ref[...]), wuq_ref[...]) * (QK_DIM ** -0.5 * LOG2E)
    kv = _dot(_rms(kv_lat, kvn_ref[...]), wukv_ref[...])

    ang = pos_ref[...] * freq_ref[...]
    cos = jnp.cos(ang)
    sin = jnp.sin(ang)
    lane = lax.broadcasted_iota(jnp.int32, ang.shape, 1)
    first = (lane % QK_ROPE) < (QK_ROPE // 2)
    sin_signed = jnp.where(first, -sin, sin)

    k_rope = _rope(k_rope, cos, sin_signed, first)[:, :QK_ROPE].astype(k_ref.dtype)
    nope_w = MLA_HEADS * QK_NOPE
    for hp in range(MLA_HEADS // 2):
        q_rope = _rope(q[:, nope_w + hp * LANES: nope_w + (hp + 1) * LANES], cos, sin_signed, first)
        for hh in range(2):
            hd = 2 * hp + hh
            q_ref[hd, :, :QK_NOPE] = q[:, hd * QK_NOPE:(hd + 1) * QK_NOPE].astype(q_ref.dtype)
            q_ref[hd, :, QK_NOPE:] = q_rope[:, hh * QK_ROPE:(hh + 1) * QK_ROPE].astype(q_ref.dtype)
    for hd in range(MLA_HEADS):
        k_ref[hd, :, :QK_NOPE] = kv[:, hd * QK_NOPE:(hd + 1) * QK_NOPE].astype(k_ref.dtype)
        k_ref[hd, :, QK_NOPE:] = k_rope
        v_ref[hd, 0] = kv[:, nope_w + hd * V_DIM: nope_w + (hd + 1) * V_DIM].T.astype(v_ref.dtype)


def _mla_proj(x, pos, ng, w_in, qn, w_uq, kvn, w_ukv, freq):
    t, d = x.shape
    tm = TOKEN_TILE
    const = lambda shape: pl.BlockSpec(shape, lambda i: (0,) * len(shape))
    return pl.pallas_call(
        _mla_proj_kernel,
        grid=(t // tm,),
        in_specs=[pl.BlockSpec((tm, d), lambda i: (i, 0)),
                  pl.BlockSpec((tm, 1), lambda i: (i, 0)),
                  const(ng.shape), const(w_in.shape), const(qn.shape), const(w_uq.shape),
                  const(kvn.shape), const(w_ukv.shape), const(freq.shape)],
        out_specs=[pl.BlockSpec((MLA_HEADS, tm, QK_DIM), lambda i: (0, i, 0)),
                   pl.BlockSpec((MLA_HEADS, tm, QK_DIM), lambda i: (0, i, 0)),
                   pl.BlockSpec((MLA_HEADS, 1, V_DIM, tm), lambda i: (0, i, 0, 0)),
                   pl.BlockSpec((tm, MLA_HEADS * V_DIM), lambda i: (i, 0))],
        out_shape=[jax.ShapeDtypeStruct((MLA_HEADS, t, QK_DIM), BF16),
                   jax.ShapeDtypeStruct((MLA_HEADS, t, QK_DIM), BF16),
                   jax.ShapeDtypeStruct((MLA_HEADS, t // tm, V_DIM, tm), BF16),
                   jax.ShapeDtypeStruct((t, MLA_HEADS * V_DIM), BF16)],
        compiler_params=_params("parallel"),
        name="mla_proj",
    )(x, pos, ng, w_in, qn, w_uq, kvn, w_ukv, freq)


def _attn_kernel(q_ref, k_ref, vt_ref, gate_ref, o_ref, st0_ref, st1_ref, acc_ref, *, blk, sub):
    qi = pl.program_id(2)
    q = q_ref[0]
    nsub = blk // sub

    def scores(kb, st_ref):
        start = pl.multiple_of(kb * blk, blk)
        st = _dot_nt(k_ref[0, pl.ds(start, blk), :], q)
        st_ref[...] = st
        return jnp.max(st, axis=0, keepdims=True)

    def update(kb, st_ref, st_max, carry, masked):
        m, l = carry
        vt = jnp.concatenate([vt_ref[0, kb * nsub + j] for j in range(nsub)], axis=1)
        st = st_ref[...]
        if masked:
            key = lax.broadcasted_iota(jnp.int32, st.shape, 0)
            qry = lax.broadcasted_iota(jnp.int32, st.shape, 1)
            st = jnp.where(key <= qry, st, MASK_VALUE)
            st_max = jnp.max(st, axis=0, keepdims=True)
        m_new = jnp.maximum(m, st_max)
        pt = jnp.exp2(st - m_new)
        corr = jnp.exp2(m - m_new)
        l_new = l * corr + jnp.sum(pt, axis=0, keepdims=True)
        acc_ref[...] = acc_ref[...] * corr + _dot(vt, pt)
        return m_new, l_new

    acc_ref[...] = jnp.zeros_like(acc_ref)
    init = (jnp.full((1, blk), MASK_VALUE, F32), jnp.zeros((1, blk), F32))

    def step(kb, cur_ref, nxt_ref, c):
        nxt_max = scores(kb + 1, nxt_ref)
        return (nxt_max,) + update(kb, cur_ref, c[0], c[1:], False)

    def body(j, c):
        c = step(2 * j, st0_ref, st1_ref, c)
        return step(2 * j + 1, st1_ref, st0_ref, c)

    c = lax.fori_loop(0, qi // 2, body, (scores(0, st0_ref),) + init)

    def tail_even(c):
        return update(qi, st0_ref, c[0], c[1:], True)

    def tail_odd(c):
        c = step(qi - 1, st0_ref, st1_ref, c)
        return update(qi, st1_ref, c[0], c[1:], True)

    _, l = lax.cond(qi % 2 == 0, tail_even, tail_odd, c)
    gate = gate_ref[...].astype(F32)
    o_ref[...] = ((acc_ref[...] / l).T * _silu(gate)).astype(o_ref.dtype)


def _attention(q, k, vt, gate, bsz, seq):
    blk = ATTN_TILE
    sub = vt.shape[3]
    nq = seq // blk
    nv = seq // sub
    return pl.pallas_call(
        functools.partial(_attn_kernel, blk=blk, sub=sub),
        grid=(bsz, MLA_HEADS, nq),
        in_specs=[pl.BlockSpec((1, blk, QK_DIM), lambda b, h, i: (h, b * nq + i, 0)),
                  pl.BlockSpec((1, seq, QK_DIM), lambda b, h, i: (h, b, 0)),
                  pl.BlockSpec((1, nv, V_DIM, sub), lambda b, h, i: (h, b, 0, 0)),
                  pl.BlockSpec((blk, V_DIM), lambda b, h, i: (b * nq + i, h))],
        out_specs=pl.BlockSpec((blk, V_DIM), lambda b, h, i: (b * nq + i, h)),
        out_shape=jax.ShapeDtypeStruct((bsz * seq, MLA_HEADS * V_DIM), BF16),
        scratch_shapes=[pltpu.VMEM((blk, blk), F32), pltpu.VMEM((blk, blk), F32),
                        pltpu.VMEM((V_DIM, blk), F32)],
        compiler_params=_params("parallel", "parallel", "arbitrary"),
        name="mla_attention",
    )(q, k, vt, gate)


def _seg_sum(x, seg_ref, segt_ref):
    s = jnp.dot(x.astype(BF16), seg_ref[...], preferred_element_type=F32)
    hi = s.astype(BF16)
    lo = (s - hi.astype(F32)).astype(BF16)
    return (jnp.dot(hi, segt_ref[...], preferred_element_type=F32)
            + jnp.dot(lo, segt_ref[...], preferred_element_type=F32))


def _finish(y, w_ref, x_ref, g_ref, o_ref, final_norm):
    x = x_ref[...] + jnp.dot(y.astype(BF16), w_ref[...], preferred_element_type=F32)
    if final_norm:
        x = _rms(x, g_ref[...])
    o_ref[...] = x


def _out_proj_kernel(y_ref, w_ref, x_ref, g_ref, o_ref, *, final_norm):
    _finish(y_ref[...], w_ref, x_ref, g_ref, o_ref, final_norm)


def _rwkv_out_kernel(y_ref, bv_ref, sg_ref, lng_ref, lnb_ref, seg_ref, segt_ref, w_ref, x_ref, g_ref,
                     o_ref, *, final_norm):
    y = y_ref[...].astype(F32)
    mean = _seg_sum(y, seg_ref, segt_ref) * (1.0 / RWKV_HEAD)
    yc = y - mean
    var = _seg_sum(yc * yc, seg_ref, segt_ref) * (1.0 / RWKV_HEAD)
    yn = yc * lax.rsqrt(var + GN_EPS) * lng_ref[...] + lnb_ref[...]
    out = (yn + bv_ref[...].astype(F32)) * sg_ref[...].astype(F32)
    _finish(out, w_ref, x_ref, g_ref, o_ref, final_norm)


def _out_proj(y, w, x, final_g, final_norm, gn=None):
    t, d = x.shape
    tm = TOKEN_TILE
    tile = lambda a: pl.BlockSpec((tm, a.shape[1]), lambda i: (i, 0))
    const = lambda a: pl.BlockSpec(a.shape, lambda i: (0, 0))
    if gn is None:
        body, ops = _out_proj_kernel, [y]
        specs = [tile(y)]
    else:
        bv, sg, lng, lnb, seg, segt = gn
        body, ops = _rwkv_out_kernel, [y, bv, sg, lng, lnb, seg, segt]
        specs = [tile(y), tile(bv), tile(sg), const(lng), const(lnb), const(seg), const(segt)]
    return pl.pallas_call(
        functools.partial(body, final_norm=final_norm),
        grid=(t // tm,),
        in_specs=specs + [const(w), tile(x), const(final_g)],
        out_specs=pl.BlockSpec((tm, d), lambda i: (i, 0)),
        out_shape=jax.ShapeDtypeStruct((t, d), F32),
        compiler_params=_params("parallel"),
        name="out_proj" if gn is None else "rwkv_out",
    )(*ops, w, x, final_g)


def _split3(x):
    hi = x.astype(BF16)
    r1 = x - hi.astype(F32)
    mid = r1.astype(BF16)
    lo = (r1 - mid.astype(F32)).astype(BF16)
    return hi, mid, lo


def _rwkv_proj_kernel(x_ref, ng_ref, win_ref, mu_ref, w0_ref, ww2_ref, a0_ref, wa2_ref,
                      kk_ref, ka_ref, rk_ref, tri_ref, seg_ref, segt_ref,
                      rt_ref, at_ref, bt_ref, kt_ref, bh_ref, kh_ref, v_ref, wl_ref, bv_ref, sg_ref,
                      prev_ref, *, width):
    @pl.when(pl.program_id(1) == 0)
    def _():
        prev_ref[...] = jnp.zeros_like(prev_ref)

    h = _rms(x_ref[...], ng_ref[...])
    proj = _dot(h, win_ref[...])
    rows = proj.shape[0]
    row = lax.broadcasted_iota(jnp.int32, (rows, 1), 0)
    shifted = jnp.where(row == 0, prev_ref[0:1, :], pltpu.roll(proj, 1, 0))
    prev_ref[0:1, :] = proj[rows - 1:rows, :]
    proj = proj + mu_ref[...] * (shifted - proj)

    lora = proj[:, 4 * width:]
    lane = lax.broadcasted_iota(jnp.int32, lora.shape, 1)
    lora = jnp.where(lane < LORA, jnp.tanh(lora), lora)
    neg = -(w0_ref[...] + _dot(lora, ww2_ref[...]))
    softplus = jnp.maximum(neg, 0.0) + jnp.log(1.0 + jnp.exp(-jnp.abs(neg)))
    log_decay = -jnp.exp(-softplus - 0.5) * LOG2E
    a = 1.0 / (1.0 + jnp.exp(-(a0_ref[...] + _dot(lora, wa2_ref[...]))))

    r = proj[:, :width]
    k = proj[:, width:2 * width]
    v = proj[:, 2 * width:3 * width]
    sg_ref[...] = _silu(proj[:, 3 * width:4 * width]).astype(sg_ref.dtype)

    kk = k * kk_ref[...]
    kk = kk * lax.rsqrt(jnp.maximum(_seg_sum(kk * kk, seg_ref, segt_ref), 1e-24))
    kp = k * (1.0 + (a - 1.0) * ka_ref[...])
    b = kk * a
    bv_ref[...] = (_seg_sum(r * kp * rk_ref[...], seg_ref, segt_ref) * v).astype(bv_ref.dtype)

    tri = tri_ref[...]
    cum = sum(jnp.dot(tri, part, preferred_element_type=F32) for part in _split3(log_decay))
    nchunks = rows // CHUNK
    last = [cum[(c + 1) * CHUNK - 1:(c + 1) * CHUNK, :] for c in range(nchunks)]
    cum_last = jnp.concatenate([jnp.broadcast_to(x, (CHUNK, width)) for x in last], axis=0)
    wl_ref[...] = jnp.concatenate([jnp.broadcast_to(jnp.exp2(x), (8, width)) for x in last], axis=0)
    w_inv = jnp.exp2(-cum)
    w_tail = jnp.exp2(cum_last - cum)
    outs = ((rt_ref, r * jnp.exp2(cum)), (at_ref, -kk * jnp.exp2(cum - log_decay)),
            (bt_ref, b * w_inv), (kt_ref, kp * w_inv), (bh_ref, b * w_tail), (kh_ref, kp * w_tail),
            (v_ref, v))
    for ref, val in outs:
        for gi in range(width // GW):
            ref[gi] = val[:, gi * GW:(gi + 1) * GW].astype(ref.dtype)


def _rwkv_proj(x, ng, w_in, mu, w0, ww2, a0, wa2, kk, ka, rk, tri, seg, segt, bsz, seq):
    t, d = x.shape
    width = w0.shape[1]
    ngroups = width // GW
    tm = TOKEN_TILE
    ns = seq // tm
    const = lambda a: pl.BlockSpec(a.shape, lambda b, i: (0,) * a.ndim)
    consts = [ng, w_in, mu, w0, ww2, a0, wa2, kk, ka, rk, tri, seg, segt]
    gspec = pl.BlockSpec((ngroups, tm, GW), lambda b, i: (0, b * ns + i, 0))
    nspec = pl.BlockSpec((tm, width), lambda b, i: (b * ns + i, 0))
    wl_rows = tm // CHUNK * 8
    return pl.pallas_call(
        functools.partial(_rwkv_proj_kernel, width=width),
        grid=(bsz, ns),
        in_specs=[pl.BlockSpec((tm, d), lambda b, i: (b * ns + i, 0))] + [const(a) for a in consts],
        out_specs=[gspec] * 7 + [pl.BlockSpec((wl_rows, width), lambda b, i: (b * ns + i, 0)),
                                 nspec, nspec],
        out_shape=[jax.ShapeDtypeStruct((ngroups, t, GW), BF16)] * 7
        + [jax.ShapeDtypeStruct((t // CHUNK * 8, width), F32),
           jax.ShapeDtypeStruct((t, width), BF16), jax.ShapeDtypeStruct((t, width), BF16)],
        scratch_shapes=[pltpu.VMEM((8, w_in.shape[1]), F32)],
        compiler_params=_params("parallel", "arbitrary"),
        name="rwkv_proj",
    )(x, *consts)


def _block_diag(y, bd_mask):
    return jnp.where(bd_mask, jnp.concatenate([y.astype(BF16)] * GROUP_HEADS, axis=0), 0)


def _wkv_kernel(rt_ref, at_ref, bt_ref, kt_ref, bh_ref, kh_ref, v_ref, wl_ref, y_ref, state_ref, *,
                ngroups, nchunks):
    @pl.when(pl.program_id(1) == 0)
    def _():
        state_ref[...] = jnp.zeros_like(state_ref)

    t_idx = lax.broadcasted_iota(jnp.int32, (CHUNK, GW), 0)
    s_idx = lax.broadcasted_iota(jnp.int32, (CHUNK, GW), 1) % CHUNK
    incl = s_idx <= t_idx
    strict = s_idx < t_idx
    eye = (s_idx == t_idx).astype(F32)
    blk8 = strict & ((s_idx // 8) == (t_idx // 8))
    off_masks = [((s_idx // (2 * b)) == (t_idx // (2 * b))) & ((t_idx // b) % 2 == 1)
                 & ((s_idx // b) % 2 == 0) for b in (8, 16, 32)]
    bd_mask = (lax.broadcasted_iota(jnp.int32, (GW, GW), 0) // RWKV_HEAD
               == lax.broadcasted_iota(jnp.int32, (GW, GW), 1) // RWKV_HEAD)

    def bd(y):
        return _block_diag(y, bd_mask)

    def mm(x, y):
        return _dot(x, bd(y))

    gs = range(ngroups)
    chains = [(c, gi) for c in range(nchunks) for gi in gs]
    rows = [slice(c * CHUNK, (c + 1) * CHUNK) for c, _ in chains]
    r_t = [rt_ref[gi, rw, :] for (_, gi), rw in zip(chains, rows)]
    a_t = [at_ref[gi, rw, :] for (_, gi), rw in zip(chains, rows)]
    v = [v_ref[gi, rw, :] for (_, gi), rw in zip(chains, rows)]

    amat = [_dot_nt(jnp.concatenate([r_t[n], a_t[n]], axis=0),
                    jnp.concatenate([bd(bt_ref[gi, rows[n], :]), bd(kt_ref[gi, rows[n], :])], axis=0))
            for n, (_, gi) in enumerate(chains)]
    a_rb = [jnp.where(incl, x[:CHUNK, :GW], 0.0) for x in amat]
    a_rk = [jnp.where(incl, x[:CHUNK, GW:], 0.0) for x in amat]
    m_ab = [jnp.where(strict, x[CHUNK:, :GW], 0.0) for x in amat]
    a_ak = [jnp.where(strict, x[CHUNK:, GW:], 0.0) for x in amat]

    n1 = [jnp.where(blk8, x, 0.0) for x in m_ab]
    n2 = [mm(x, x) for x in n1]
    n4 = [mm(x, x) for x in n2]
    tmat = [eye + x for x in n1]
    tmat = [x + mm(x, y) for x, y in zip(tmat, n2)]
    tmat = [x + mm(x, y) for x, y in zip(tmat, n4)]
    for off in off_masks:
        z = [mm(jnp.where(off, m, 0.0), x) for m, x in zip(m_ab, tmat)]
        tmat = [x + mm(x, y) for x, y in zip(tmat, z)]

    akv = [mm(x, y) for x, y in zip(a_ak, v)]
    pq = [_dot(tmat[n], jnp.concatenate([bd(a_t[n]), bd(akv[n])], axis=1))
          for n in range(len(chains))]

    state = [state_ref[gi] for gi in gs]
    for c in range(nchunks):
        ns = [c * ngroups + gi for gi in gs]
        us = [_dot_nt(jnp.concatenate([pq[n][:, :GW].astype(BF16), r_t[n]], axis=0), state[gi])
              for gi, n in zip(gs, ns)]
        u = [us[gi][:CHUNK] + pq[n][:, GW:] for gi, n in zip(gs, ns)]
        y = [us[gi][CHUNK:]
             + _dot(jnp.concatenate([a_rb[n], a_rk[n]], axis=1),
                    jnp.concatenate([bd(u[gi]), bd(v[n])], axis=0)) for gi, n in zip(gs, ns)]
        upd = [_dot(jnp.concatenate([u[gi], v[n].astype(F32)], axis=0).T,
                    jnp.concatenate([bh_ref[gi, rows[n], :], kh_ref[gi, rows[n], :]], axis=0))
               for gi, n in zip(gs, ns)]
        wl = wl_ref[c * 8:c * 8 + 1, :]
        state = [state[gi] * wl[:, gi * GW:(gi + 1) * GW] + jnp.where(bd_mask, upd[gi], 0.0)
                 for gi in gs]
        for gi in gs:
            y_ref[rows[ns[gi]], gi * GW:(gi + 1) * GW] = y[gi].astype(y_ref.dtype)
    for gi in gs:
        state_ref[gi] = state[gi]


def _wkv(rt, at, bt, kt, bh, kh, v, wl, bsz, seq):
    ngroups, t, _ = rt.shape
    tt = WKV_TILE
    ns = seq // tt
    gspec = pl.BlockSpec((ngroups, tt, GW), lambda b, i: (0, b * ns + i, 0))
    return pl.pallas_call(
        functools.partial(_wkv_kernel, ngroups=ngroups, nchunks=tt // CHUNK),
        grid=(bsz, ns),
        in_specs=[gspec] * 7 + [pl.BlockSpec((tt // CHUNK * 8, ngroups * GW),
                                             lambda b, i: (b * ns + i, 0))],
        out_specs=pl.BlockSpec((tt, ngroups * GW), lambda b, i: (b * ns + i, 0)),
        out_shape=jax.ShapeDtypeStruct((t, ngroups * GW), BF16),
        scratch_shapes=[pltpu.VMEM((ngroups, GW, GW), F32)],
        compiler_params=_params("parallel", "arbitrary"),
        name="rwkv_wkv",
    )(rt, at, bt, kt, bh, kh, v, wl)


def _row(v):
    return v.reshape(1, -1).astype(F32)


def kernel(x, positions, norm_g, mla_w_in, mla_q_norm, mla_w_uq, mla_kv_norm, mla_w_ukv, mla_w_out, rwkv_w_in, rwkv_mu, rwkv_w0, rwkv_w_w2, rwkv_a0, rwkv_w_a2, rwkv_k_k, rwkv_k_a, rwkv_r_k, rwkv_ln_g, rwkv_ln_b, rwkv_w_out, final_g):
    bsz, seq, d = x.shape
    depth = norm_g.shape[0]
    t = bsz * seq
    xf = x.reshape(t, d)
    pos = positions.reshape(t, 1).astype(F32)
    half = QK_ROPE // 2
    inv_freq = 1.0 / (ROPE_THETA ** (jnp.arange(half, dtype=F32) / half))
    freq = jnp.tile(inv_freq, LANES // half).reshape(1, LANES)
    final_row = _row(final_g)

    width = rwkv_w0.shape[1]
    tok = jnp.arange(TOKEN_TILE)
    tri = ((tok[:, None] // CHUNK == tok[None, :] // CHUNK) & (tok[None, :] <= tok[:, None])).astype(BF16)
    seg = (jnp.arange(width)[:, None] // RWKV_HEAD == jnp.arange(LANES)[None, :]).astype(BF16)
    segt = seg.T

    for i in range(depth):
        j = i // 2
        ng = _row(norm_g[i])
        last = i == depth - 1
        if i % 2 == 0:
            w_in = mla_w_in[j]
            lat = Q_LORA + KV_LORA
            w_in = jnp.concatenate(
                [w_in[:, :lat], w_in[:, lat + QK_ROPE:], w_in[:, lat:lat + QK_ROPE],
                 jnp.zeros((d, LANES - QK_ROPE), w_in.dtype)], axis=1).astype(BF16)
            w_uq = mla_w_uq[j].reshape(Q_LORA, MLA_HEADS, QK_DIM)
            w_uq = jnp.concatenate([w_uq[:, :, :QK_NOPE].reshape(Q_LORA, -1),
                                    w_uq[:, :, QK_NOPE:].reshape(Q_LORA, -1)], axis=1).astype(BF16)
            w_ukv = mla_w_ukv[j].reshape(KV_LORA, MLA_HEADS, QK_NOPE + V_DIM)
            w_ukv = jnp.concatenate([w_ukv[:, :, :QK_NOPE].reshape(KV_LORA, -1),
                                     w_ukv[:, :, QK_NOPE:].reshape(KV_LORA, -1)], axis=1).astype(BF16)
            q, k, vt, gate = _mla_proj(xf, pos, ng, w_in, _row(mla_q_norm[j]), w_uq,
                                       _row(mla_kv_norm[j]), w_ukv, freq)
            y = _attention(q, k, vt, gate, bsz, seq)
            xf = _out_proj(y, mla_w_out[j].astype(BF16), xf, final_row, last)
        else:
            zeros = jnp.zeros((LORA, width), F32)
            ww2 = jnp.concatenate([rwkv_w_w2[j], zeros], axis=0).astype(BF16)
            wa2 = jnp.concatenate([zeros, rwkv_w_a2[j]], axis=0).astype(BF16)
            *ops, wl, bv, sg = _rwkv_proj(
                xf, ng, rwkv_w_in[j].astype(BF16), _row(rwkv_mu[j]), _row(rwkv_w0[j]), ww2,
                _row(rwkv_a0[j]), wa2, _row(rwkv_k_k[j]), _row(rwkv_k_a[j]), _row(rwkv_r_k[j]),
                tri, seg, segt, bsz, seq)
            y = _wkv(*ops, wl, bsz, seq)
            gn = (bv, sg, _row(rwkv_ln_g[j]), _row(rwkv_ln_b[j]), seg, segt)
            xf = _out_proj(y, rwkv_w_out[j].astype(BF16), xf, final_row, last, gn)
    return xf.reshape(bsz, seq, d)
```

```python
import functools

import jax
import jax.numpy as jnp
from jax import lax
from jax.experimental import pallas as pl
from jax.experimental.pallas import tpu as pltpu

F32 = jnp.float32
BF16 = jnp.bfloat16

NORM_EPS = 1e-6
GN_EPS = 64e-5
ROPE_THETA = 10000.0

MLA_HEADS = 8
QK_NOPE = 128
QK_ROPE = 64
QK_DIM = QK_NOPE + QK_ROPE
V_DIM = 128
Q_LORA = 768
KV_LORA = 256

RWKV_HEAD = 64
LORA = 64

LANES = 128
CHUNK = 64
GROUP_HEADS = 2
GW = GROUP_HEADS * RWKV_HEAD

TOKEN_TILE = 256
MLA_TILE = 512
OUT_TILE = 512
ATTN_TILE = 512
WKV_TILE = 256
VMEM_LIMIT = 56 * 1024 * 1024
MASK_VALUE = -1e30
LOG2E = 1.4426950408889634
EXP_M05 = 0.6065306597126334


def _params(*sem):
    return pltpu.CompilerParams(dimension_semantics=sem, vmem_limit_bytes=VMEM_LIMIT)


def _rms(x, g):
    return x * lax.rsqrt(jnp.mean(x * x, axis=-1, keepdims=True) + NORM_EPS) * g


def _dot(a, b):
    return jnp.dot(a.astype(BF16), b.astype(BF16), preferred_element_type=F32)


def _dot_nt(a, b):
    return lax.dot_general(a.astype(BF16), b.astype(BF16), (((1,), (1,)), ((), ())),
                           preferred_element_type=F32)


def _silu(x):
    return x / (1.0 + jnp.exp(-x))


def _rope(t, cos, sin_signed, first):
    partner = jnp.where(first, pltpu.roll(t, LANES - QK_ROPE // 2, 1), pltpu.roll(t, QK_ROPE // 2, 1))
    return t * cos + partner * sin_signed


def _mla_proj_kernel(x_ref, cos_ref, sin_ref, ng_ref, win_ref, qn_ref, wuq_ref, kvn_ref, wukv_ref,
                     q_ref, k_ref, v_ref, gate_ref):
    x = x_ref[...]
    h = _rms(x, ng_ref[...])
    proj = _dot(h, win_ref[...])
    q_lat = proj[:, :Q_LORA]
    kv_lat = proj[:, Q_LORA:Q_LORA + KV_LORA]
    gate_ref[...] = proj[:, 1024:2048].astype(gate_ref.dtype)
    k_rope = proj[:, 2048:2048 + LANES]

    q = _dot(_rms(q_lat, qn_ref[...]), wuq_ref[...]) * (QK_DIM ** -0.5 * LOG2E)
    kv = _dot(_rms(kv_lat, kvn_ref[...]), wukv_ref[...])

    cos = cos_ref[...]
    sin = sin_ref[...]
    lane = lax.broadcasted_iota(jnp.int32, cos.shape, 1)
    first = (lane % QK_ROPE) < (QK_ROPE // 2)
    sin_signed = jnp.where(first, -sin, sin)

    k_rope = _rope(k_rope, cos, sin_signed, first)[:, :QK_ROPE].astype(k_ref.dtype)
    nope_w = MLA_HEADS * QK_NOPE
    for hp in range(MLA_HEADS // 2):
        q_rope = _rope(q[:, nope_w + hp * LANES: nope_w + (hp + 1) * LANES], cos, sin_signed, first)
        for hh in range(2):
            hd = 2 * hp + hh
            q_ref[hd, :, :QK_NOPE] = q[:, hd * QK_NOPE:(hd + 1) * QK_NOPE].astype(q_ref.dtype)
            q_ref[hd, :, QK_NOPE:] = q_rope[:, hh * QK_ROPE:(hh + 1) * QK_ROPE].astype(q_ref.dtype)
    for hd in range(MLA_HEADS):
        k_ref[hd, :, :QK_NOPE] = kv[:, hd * QK_NOPE:(hd + 1) * QK_NOPE].astype(k_ref.dtype)
        k_ref[hd, :, QK_NOPE:] = k_rope
        v_ref[hd, 0] = kv[:, nope_w + hd * V_DIM: nope_w + (hd + 1) * V_DIM].T.astype(v_ref.dtype)


def _mla_proj(x, cos, sin, ng, w_in, qn, w_uq, kvn, w_ukv):
    t, d = x.shape
    tm = MLA_TILE
    const = lambda shape: pl.BlockSpec(shape, lambda i: (0,) * len(shape))
    return pl.pallas_call(
        _mla_proj_kernel,
        grid=(t // tm,),
        in_specs=[pl.BlockSpec((tm, d), lambda i: (i, 0)),
                  pl.BlockSpec((tm, LANES), lambda i: (i, 0)),
                  pl.BlockSpec((tm, LANES), lambda i: (i, 0)),
                  const(ng.shape), const(w_in.shape), const(qn.shape), const(w_uq.shape),
                  const(kvn.shape), const(w_ukv.shape)],
        out_specs=[pl.BlockSpec((MLA_HEADS, tm, QK_DIM), lambda i: (0, i, 0)),
                   pl.BlockSpec((MLA_HEADS, tm, QK_DIM), lambda i: (0, i, 0)),
                   pl.BlockSpec((MLA_HEADS, 1, V_DIM, tm), lambda i: (0, i, 0, 0)),
                   pl.BlockSpec((tm, MLA_HEADS * V_DIM), lambda i: (i, 0))],
        out_shape=[jax.ShapeDtypeStruct((MLA_HEADS, t, QK_DIM), BF16),
                   jax.ShapeDtypeStruct((MLA_HEADS, t, QK_DIM), BF16),
                   jax.ShapeDtypeStruct((MLA_HEADS, t // tm, V_DIM, tm), BF16),
                   jax.ShapeDtypeStruct((t, MLA_HEADS * V_DIM), BF16)],
        compiler_params=_params("parallel"),
        name="mla_proj",
    )(x, cos, sin, ng, w_in, qn, w_uq, kvn, w_ukv)


def _attn_kernel(q_ref, k_ref, vt_ref, gate_ref, o_ref, st0_ref, st1_ref, acc_ref, *, blk, sub):
    qi = pl.program_id(2)
    q = q_ref[0]
    nsub = blk // sub

    def scores(kb, st_ref):
        start = pl.multiple_of(kb * blk, blk)
        st = _dot_nt(k_ref[0, pl.ds(start, blk), :], q)
        st_ref[...] = st
        return jnp.max(st, axis=0, keepdims=True)

    def update(kb, st_ref, st_max, carry, masked):
        m, l = carry
        vt = jnp.concatenate([vt_ref[0, kb * nsub + j] for j in range(nsub)], axis=1)
        st = st_ref[...]
        if masked:
            key = lax.broadcasted_iota(jnp.int32, st.shape, 0)
            qry = lax.broadcasted_iota(jnp.int32, st.shape, 1)
            st = jnp.where(key <= qry, st, MASK_VALUE)
            st_max = jnp.max(st, axis=0, keepdims=True)
        m_new = jnp.maximum(m, st_max)
        pt = jnp.exp2(st - m_new)
        corr = jnp.exp2(m - m_new)
        l_new = l * corr + jnp.sum(pt, axis=0, keepdims=True)
        acc_ref[...] = acc_ref[...] * corr + _dot(vt, pt)
        return m_new, l_new

    acc_ref[...] = jnp.zeros_like(acc_ref)
    init = (jnp.full((1, blk), MASK_VALUE, F32), jnp.zeros((1, blk), F32))

    def step(kb, cur_ref, nxt_ref, c):
        nxt_max = scores(kb + 1, nxt_ref)
        return (nxt_max,) + update(kb, cur_ref, c[0], c[1:], False)

    def body(j, c):
        c = step(2 * j, st0_ref, st1_ref, c)
        return step(2 * j + 1, st1_ref, st0_ref, c)

    c = lax.fori_loop(0, qi // 2, body, (scores(0, st0_ref),) + init)

    def tail_even(c):
        return update(qi, st0_ref, c[0], c[1:], True)

    def tail_odd(c):
        c = step(qi - 1, st0_ref, st1_ref, c)
        return update(qi, st1_ref, c[0], c[1:], True)

    _, l = lax.cond(qi % 2 == 0, tail_even, tail_odd, c)
    gate = gate_ref[...].astype(F32)
    o_ref[...] = ((acc_ref[...] / l).T * _silu(gate)).astype(o_ref.dtype)


def _attention(q, k, vt, gate, bsz, seq):
    blk = ATTN_TILE
    sub = vt.shape[3]
    nq = seq // blk
    nv = seq // sub
    return pl.pallas_call(
        functools.partial(_attn_kernel, blk=blk, sub=sub),
        grid=(bsz, MLA_HEADS, nq),
        in_specs=[pl.BlockSpec((1, blk, QK_DIM), lambda b, h, i: (h, b * nq + i, 0)),
                  pl.BlockSpec((1, seq, QK_DIM), lambda b, h, i: (h, b, 0)),
                  pl.BlockSpec((1, nv, V_DIM, sub), lambda b, h, i: (h, b, 0, 0)),
                  pl.BlockSpec((blk, V_DIM), lambda b, h, i: (b * nq + i, h))],
        out_specs=pl.BlockSpec((blk, V_DIM), lambda b, h, i: (b * nq + i, h)),
        out_shape=jax.ShapeDtypeStruct((bsz * seq, MLA_HEADS * V_DIM), BF16),
        scratch_shapes=[pltpu.VMEM((blk, blk), F32), pltpu.VMEM((blk, blk), F32),
                        pltpu.VMEM((V_DIM, blk), F32)],
        compiler_params=_params("parallel", "parallel", "arbitrary"),
        name="mla_attention",
    )(q, k, vt, gate)


def _seg_sum(x, seg_ref, segt_ref):
    s = jnp.dot(x.astype(BF16), seg_ref[...], preferred_element_type=F32)
    hi = s.astype(BF16)
    lo = (s - hi.astype(F32)).astype(BF16)
    return (jnp.dot(hi, segt_ref[...], preferred_element_type=F32)
            + jnp.dot(lo, segt_ref[...], preferred_element_type=F32))


def _finish(y, w_ref, x_ref, g_ref, o_ref, final_norm):
    x = x_ref[...] + jnp.dot(y.astype(BF16), w_ref[...], preferred_element_type=F32)
    if final_norm:
        x = _rms(x, g_ref[...])
    o_ref[...] = x


def _out_proj_kernel(y_ref, w_ref, x_ref, g_ref, o_ref, *, final_norm):
    _finish(y_ref[...], w_ref, x_ref, g_ref, o_ref, final_norm)


def _rwkv_out_kernel(y_ref, bv_ref, sg_ref, lng_ref, lnb_ref, seg_ref, segt_ref, w_ref, x_ref, g_ref,
                     o_ref, *, final_norm):
    y = y_ref[...].astype(F32)
    mean = _seg_sum(y, seg_ref, segt_ref) * (1.0 / RWKV_HEAD)
    yc = y - mean
    var = _seg_sum(yc * yc, seg_ref, segt_ref) * (1.0 / RWKV_HEAD)
    yn = yc * lax.rsqrt(var + GN_EPS) * lng_ref[...] + lnb_ref[...]
    out = (yn + bv_ref[...].astype(F32)) * sg_ref[...].astype(F32)
    _finish(out, w_ref, x_ref, g_ref, o_ref, final_norm)


def _out_proj(y, w, x, final_g, final_norm, gn=None):
    t, d = x.shape
    tm = OUT_TILE if gn is None else TOKEN_TILE
    tile = lambda a: pl.BlockSpec((tm, a.shape[1]), lambda i: (i, 0))
    const = lambda a: pl.BlockSpec(a.shape, lambda i: (0, 0))
    if gn is None:
        body, ops = _out_proj_kernel, [y]
        specs = [tile(y)]
    else:
        bv, sg, lng, lnb, seg, segt = gn
        body, ops = _rwkv_out_kernel, [y, bv, sg, lng, lnb, seg, segt]
        specs = [tile(y), tile(bv), tile(sg), const(lng), const(lnb), const(seg), const(segt)]
    return pl.pallas_call(
        functools.partial(body, final_norm=final_norm),
        grid=(t // tm,),
        in_specs=specs + [const(w), tile(x), const(final_g)],
        out_specs=pl.BlockSpec((tm, d), lambda i: (i, 0)),
        out_shape=jax.ShapeDtypeStruct((t, d), F32),
        compiler_params=_params("parallel"),
        name="out_proj" if gn is None else "rwkv_out",
    )(*ops, w, x, final_g)


def _split3(x):
    hi = x.astype(BF16)
    r1 = x - hi.astype(F32)
    mid = r1.astype(BF16)
    lo = (r1 - mid.astype(F32)).astype(BF16)
    return hi, mid, lo


def _rwkv_proj_kernel(x_ref, ng_ref, win_ref, mu_ref, w0_ref, ww2_ref, a0_ref, wa2_ref,
                      kk_ref, ka_ref, rk_ref, tri_ref, seg_ref, segt_ref,
                      rt_ref, at_ref, bt_ref, kt_ref, bh_ref, kh_ref, v_ref, wl_ref, bv_ref, sg_ref,
                      prev_ref, *, width):
    @pl.when(pl.program_id(1) == 0)
    def _():
        prev_ref[...] = jnp.zeros_like(prev_ref)

    h = _rms(x_ref[...], ng_ref[...])
    proj = _dot(h, win_ref[...])
    rows = proj.shape[0]
    rolled = pltpu.roll(proj, 1, 0)
    row = lax.broadcasted_iota(jnp.int32, (8, 1), 0)
    shifted = jnp.concatenate([jnp.where(row == 0, prev_ref[0:1, :], rolled[:8]), rolled[8:]], axis=0)
    prev_ref[0:1, :] = proj[rows - 1:rows, :]
    proj = proj + mu_ref[...] * (shifted - proj)

    lora = proj[:, 4 * width:]
    lane = lax.broadcasted_iota(jnp.int32, lora.shape, 1)
    lora = jnp.where(lane < LORA, jnp.tanh(lora), lora)
    z = w0_ref[...] + _dot(lora, ww2_ref[...])
    log_decay = (-LOG2E * EXP_M05) / (1.0 + jnp.exp(-z))
    a = 1.0 / (1.0 + jnp.exp(-(a0_ref[...] + _dot(lora, wa2_ref[...]))))

    r = proj[:, :width]
    k = proj[:, width:2 * width]
    v = proj[:, 2 * width:3 * width]
    sg_ref[...] = _silu(proj[:, 3 * width:4 * width]).astype(sg_ref.dtype)

    kk = k * kk_ref[...]
    kk = kk * lax.rsqrt(jnp.maximum(_seg_sum(kk * kk, seg_ref, segt_ref), 1e-24))
    kp = k * (1.0 + (a - 1.0) * ka_ref[...])
    b = kk * a
    bv_ref[...] = (_seg_sum(r * kp * rk_ref[...], seg_ref, segt_ref) * v).astype(bv_ref.dtype)

    tri = tri_ref[...]
    cum = sum(jnp.dot(tri, part, preferred_element_type=F32) for part in _split3(log_decay))
    nchunks = rows // CHUNK
    last = [cum[(c + 1) * CHUNK - 1:(c + 1) * CHUNK, :] for c in range(nchunks)]
    cum_last = jnp.concatenate([jnp.broadcast_to(x, (CHUNK, width)) for x in last], axis=0)
    wl_ref[...] = jnp.concatenate([jnp.broadcast_to(jnp.exp2(x), (8, width)) for x in last], axis=0)
    w_inv = jnp.exp2(-cum)
    w_tail = jnp.exp2(cum_last - cum)
    outs = ((rt_ref, r * jnp.exp2(cum)), (at_ref, -kk * jnp.exp2(cum - log_decay)),
            (bt_ref, b * w_inv), (kt_ref, kp * w_inv), (bh_ref, b * w_tail), (kh_ref, kp * w_tail),
            (v_ref, v))
    for ref, val in outs:
        for gi in range(width // GW):
            ref[gi] = val[:, gi * GW:(gi + 1) * GW].astype(ref.dtype)


def _rwkv_proj(x, ng, w_in, mu, w0, ww2, a0, wa2, kk, ka, rk, tri, seg, segt, bsz, seq):
    t, d = x.shape
    width = w0.shape[1]
    ngroups = width // GW
    tm = TOKEN_TILE
    ns = seq // tm
    const = lambda a: pl.BlockSpec(a.shape, lambda b, i: (0,) * a.ndim)
    consts = [ng, w_in, mu, w0, ww2, a0, wa2, kk, ka, rk, tri, seg, segt]
    gspec = pl.BlockSpec((ngroups, tm, GW), lambda b, i: (0, b * ns + i, 0))
    nspec = pl.BlockSpec((tm, width), lambda b, i: (b * ns + i, 0))
    wl_rows = tm // CHUNK * 8
    return pl.pallas_call(
        functools.partial(_rwkv_proj_kernel, width=width),
        grid=(bsz, ns),
        in_specs=[pl.BlockSpec((tm, d), lambda b, i: (b * ns + i, 0))] + [const(a) for a in consts],
        out_specs=[gspec] * 7 + [pl.BlockSpec((wl_rows, width), lambda b, i: (b * ns + i, 0)),
                                 nspec, nspec],
        out_shape=[jax.ShapeDtypeStruct((ngroups, t, GW), BF16)] * 7
        + [jax.ShapeDtypeStruct((t // CHUNK * 8, width), F32),
           jax.ShapeDtypeStruct((t, width), BF16), jax.ShapeDtypeStruct((t, width), BF16)],
        scratch_shapes=[pltpu.VMEM((8, w_in.shape[1]), F32)],
        compiler_params=_params("parallel", "arbitrary"),
        name="rwkv_proj",
    )(x, *consts)


def _block_diag(y, bd_mask):
    return jnp.where(bd_mask, jnp.concatenate([y.astype(BF16)] * GROUP_HEADS, axis=0), 0)


def _wkv_kernel(rt_ref, at_ref, bt_ref, kt_ref, bh_ref, kh_ref, v_ref, wl_ref, y_ref, state_ref, *,
                ngroups, nchunks):
    @pl.when(pl.program_id(1) == 0)
    def _():
        state_ref[...] = jnp.zeros_like(state_ref)

    t_idx = lax.broadcasted_iota(jnp.int32, (CHUNK, GW), 0)
    s_idx = lax.broadcasted_iota(jnp.int32, (CHUNK, GW), 1) % CHUNK
    incl = s_idx <= t_idx
    strict = s_idx < t_idx
    eye = (s_idx == t_idx).astype(F32)
    blk8 = strict & ((s_idx // 8) == (t_idx // 8))
    off_masks = [((s_idx // (2 * b)) == (t_idx // (2 * b))) & ((t_idx // b) % 2 == 1)
                 & ((s_idx // b) % 2 == 0) for b in (8, 16, 32)]
    bd_mask = (lax.broadcasted_iota(jnp.int32, (GW, GW), 0) // RWKV_HEAD
               == lax.broadcasted_iota(jnp.int32, (GW, GW), 1) // RWKV_HEAD)

    def bd(y):
        return _block_diag(y, bd_mask)

    def mm(x, y):
        return _dot(x, bd(y))

    gs = range(ngroups)
    chains = [(c, gi) for c in range(nchunks) for gi in gs]
    rows = [slice(c * CHUNK, (c + 1) * CHUNK) for c, _ in chains]
    r_t = [rt_ref[gi, rw, :] for (_, gi), rw in zip(chains, rows)]
    a_t = [at_ref[gi, rw, :] for (_, gi), rw in zip(chains, rows)]
    v = [v_ref[gi, rw, :] for (_, gi), rw in zip(chains, rows)]

    amat = [_dot_nt(jnp.concatenate([r_t[n], a_t[n]], axis=0),
                    jnp.concatenate([bd(bt_ref[gi, rows[n], :]), bd(kt_ref[gi, rows[n], :])], axis=0))
            for n, (_, gi) in enumerate(chains)]
    a_rb = [jnp.where(incl, x[:CHUNK, :GW], 0.0) for x in amat]
    a_rk = [jnp.where(incl, x[:CHUNK, GW:], 0.0) for x in amat]
    m_ab = [jnp.where(strict, x[CHUNK:, :GW], 0.0) for x in amat]
    a_ak = [jnp.where(strict, x[CHUNK:, GW:], 0.0) for x in amat]

    n1 = [jnp.where(blk8, x, 0.0) for x in m_ab]
    n2 = [mm(x, x) for x in n1]
    n4 = [mm(x, x) for x in n2]
    tmat = [eye + x for x in n1]
    tmat = [x + mm(x, y) for x, y in zip(tmat, n2)]
    tmat = [x + mm(x, y) for x, y in zip(tmat, n4)]
    for off in off_masks:
        z = [mm(jnp.where(off, m, 0.0), x) for m, x in zip(m_ab, tmat)]
        tmat = [x + mm(x, y) for x, y in zip(tmat, z)]

    akv = [mm(x, y) for x, y in zip(a_ak, v)]
    pq = [_dot(tmat[n], jnp.concatenate([bd(a_t[n]), bd(akv[n])], axis=1))
          for n in range(len(chains))]

    state = [state_ref[gi] for gi in gs]
    for c in range(nchunks):
        ns = [c * ngroups + gi for gi in gs]
        us = [_dot_nt(jnp.concatenate([pq[n][:, :GW].astype(BF16), r_t[n]], axis=0), state[gi])
              for gi, n in zip(gs, ns)]
        u = [us[gi][:CHUNK] + pq[n][:, GW:] for gi, n in zip(gs, ns)]
        y = [us[gi][CHUNK:]
             + _dot(jnp.concatenate([a_rb[n], a_rk[n]], axis=1),
                    jnp.concatenate([bd(u[gi]), bd(v[n])], axis=0)) for gi, n in zip(gs, ns)]
        upd = [_dot(jnp.concatenate([u[gi], v[n].astype(F32)], axis=0).T,
                    jnp.concatenate([bh_ref[gi, rows[n], :], kh_ref[gi, rows[n], :]], axis=0))
               for gi, n in zip(gs, ns)]
        wl = wl_ref[c * 8:c * 8 + 1, :]
        state = [state[gi] * wl[:, gi * GW:(gi + 1) * GW] + jnp.where(bd_mask, upd[gi], 0.0)
                 for gi in gs]
        for gi in gs:
            y_ref[rows[ns[gi]], gi * GW:(gi + 1) * GW] = y[gi].astype(y_ref.dtype)
    for gi in gs:
        state_ref[gi] = state[gi]


def _wkv(rt, at, bt, kt, bh, kh, v, wl, bsz, seq):
    ngroups, t, _ = rt.shape
    tt = WKV_TILE
    ns = seq // tt
    gspec = pl.BlockSpec((ngroups, tt, GW), lambda b, i: (0, b * ns + i, 0))
    return pl.pallas_call(
        functools.partial(_wkv_kernel, ngroups=ngroups, nchunks=tt // CHUNK),
        grid=(bsz, ns),
        in_specs=[gspec] * 7 + [pl.BlockSpec((tt // CHUNK * 8, ngroups * GW),
                                             lambda b, i: (b * ns + i, 0))],
        out_specs=pl.BlockSpec((tt, ngroups * GW), lambda b, i: (b * ns + i, 0)),
        out_shape=jax.ShapeDtypeStruct((t, ngroups * GW), BF16),
        scratch_shapes=[pltpu.VMEM((ngroups, GW, GW), F32)],
        compiler_params=_params("parallel", "arbitrary"),
        name="rwkv_wkv",
    )(rt, at, bt, kt, bh, kh, v, wl)


def _row(v):
    return v.reshape(1, -1).astype(F32)


def kernel(x, positions, norm_g, mla_w_in, mla_q_norm, mla_w_uq, mla_kv_norm, mla_w_ukv, mla_w_out, rwkv_w_in, rwkv_mu, rwkv_w0, rwkv_w_w2, rwkv_a0, rwkv_w_a2, rwkv_k_k, rwkv_k_a, rwkv_r_k, rwkv_ln_g, rwkv_ln_b, rwkv_w_out, final_g):
    bsz, seq, d = x.shape
    depth = norm_g.shape[0]
    t = bsz * seq
    xf = x.reshape(t, d)
    half = QK_ROPE // 2
    inv_freq = 1.0 / (ROPE_THETA ** (jnp.arange(half, dtype=F32) / half))
    ang = positions.reshape(t, 1).astype(F32) * jnp.tile(inv_freq, LANES // half)[None, :]
    cos, sin = jnp.cos(ang), jnp.sin(ang)
    final_row = _row(final_g)

    width = rwkv_w0.shape[1]
    tok = jnp.arange(TOKEN_TILE)
    tri = ((tok[:, None] // CHUNK == tok[None, :] // CHUNK) & (tok[None, :] <= tok[:, None])).astype(BF16)
    seg = (jnp.arange(width)[:, None] // RWKV_HEAD == jnp.arange(LANES)[None, :]).astype(BF16)
    segt = seg.T

    for i in range(depth):
        j = i // 2
        ng = _row(norm_g[i])
        last = i == depth - 1
        if i % 2 == 0:
            w_in = mla_w_in[j]
            lat = Q_LORA + KV_LORA
            w_in = jnp.concatenate(
                [w_in[:, :lat], w_in[:, lat + QK_ROPE:], w_in[:, lat:lat + QK_ROPE],
                 jnp.zeros((d, LANES - QK_ROPE), w_in.dtype)], axis=1).astype(BF16)
            w_uq = mla_w_uq[j].reshape(Q_LORA, MLA_HEADS, QK_DIM)
            w_uq = jnp.concatenate([w_uq[:, :, :QK_NOPE].reshape(Q_LORA, -1),
                                    w_uq[:, :, QK_NOPE:].reshape(Q_LORA, -1)], axis=1).astype(BF16)
            w_ukv = mla_w_ukv[j].reshape(KV_LORA, MLA_HEADS, QK_NOPE + V_DIM)
            w_ukv = jnp.concatenate([w_ukv[:, :, :QK_NOPE].reshape(KV_LORA, -1),
                                     w_ukv[:, :, QK_NOPE:].reshape(KV_LORA, -1)], axis=1).astype(BF16)
            q, k, vt, gate = _mla_proj(xf, cos, sin, ng, w_in, _row(mla_q_norm[j]), w_uq,
                                       _row(mla_kv_norm[j]), w_ukv)
            y = _attention(q, k, vt, gate, bsz, seq)
            xf = _out_proj(y, mla_w_out[j].astype(BF16), xf, final_row, last)
        else:
            zeros = jnp.zeros((LORA, width), F32)
            ww2 = jnp.concatenate([rwkv_w_w2[j], zeros], axis=0).astype(BF16)
            wa2 = jnp.concatenate([zeros, rwkv_w_a2[j]], axis=0).astype(BF16)
            *ops, wl, bv, sg = _rwkv_proj(
                xf, ng, rwkv_w_in[j].astype(BF16), _row(rwkv_mu[j]), _row(rwkv_w0[j]), ww2,
                _row(rwkv_a0[j]), wa2, _row(rwkv_k_k[j]), _row(rwkv_k_a[j]), _row(rwkv_r_k[j]),
                tri, seg, segt, bsz, seq)
            y = _wkv(*ops, wl, bsz, seq)
            gn = (bv, sg, _row(rwkv_ln_g[j]), _row(rwkv_ln_b[j]), seg, segt)
            xf = _out_proj(y, rwkv_w_out[j].astype(BF16), xf, final_row, last, gn)
    return xf.reshape(bsz, seq, d)
```

```python
import functools

import jax
import jax.numpy as jnp
from jax import lax
from jax.experimental import pallas as pl
from jax.experimental.pallas import tpu as pltpu

F32 = jnp.float32
BF16 = jnp.bfloat16

NORM_EPS = 1e-6
GN_EPS = 64e-5
ROPE_THETA = 10000.0

MLA_HEADS = 8
QK_NOPE = 128
QK_ROPE = 64
QK_DIM = QK_NOPE + QK_ROPE
V_DIM = 128
Q_LORA = 768
KV_LORA = 256

RWKV_HEAD = 64
LORA = 64

LANES = 128
CHUNK = 64
GROUP_HEADS = 2
GW = GROUP_HEADS * RWKV_HEAD

TOKEN_TILE = 256
MLA_TILE = 512
OUT_TILE = 512
ATTN_TILE = 512
WKV_TILE = 256
VMEM_LIMIT = 56 * 1024 * 1024
MASK_VALUE = -1e30
ONES_ROWS = 16
LOG2E = 1.4426950408889634
EXP_M05 = 0.6065306597126334


def _params(*sem):
    return pltpu.CompilerParams(dimension_semantics=sem, vmem_limit_bytes=VMEM_LIMIT)


def _rms(x, g):
    return x * lax.rsqrt(jnp.mean(x * x, axis=-1, keepdims=True) + NORM_EPS) * g


def _dot(a, b):
    return jnp.dot(a.astype(BF16), b.astype(BF16), preferred_element_type=F32)


def _dot_nt(a, b):
    return lax.dot_general(a.astype(BF16), b.astype(BF16), (((1,), (1,)), ((), ())),
                           preferred_element_type=F32)


def _silu(x):
    return x / (1.0 + jnp.exp(-x))


def _rope(t, cos, sin_signed, first):
    partner = jnp.where(first, pltpu.roll(t, LANES - QK_ROPE // 2, 1), pltpu.roll(t, QK_ROPE // 2, 1))
    return t * cos + partner * sin_signed


def _mla_proj_kernel(x_ref, cos_ref, sin_ref, ng_ref, win_ref, qn_ref, wuq_ref, kvn_ref, wukv_ref,
                     q_ref, k_ref, v_ref, gate_ref):
    x = x_ref[...]
    h = _rms(x, ng_ref[...])
    proj = _dot(h, win_ref[...])
    q_lat = proj[:, :Q_LORA]
    kv_lat = proj[:, Q_LORA:Q_LORA + KV_LORA]
    gate_ref[...] = proj[:, 1024:2048].astype(gate_ref.dtype)
    k_rope = proj[:, 2048:2048 + LANES]

    q = _dot(_rms(q_lat, qn_ref[...]), wuq_ref[...]) * (QK_DIM ** -0.5 * LOG2E)
    kv = _dot(_rms(kv_lat, kvn_ref[...]), wukv_ref[...])

    cos = cos_ref[...]
    sin = sin_ref[...]
    lane = lax.broadcasted_iota(jnp.int32, cos.shape, 1)
    first = (lane % QK_ROPE) < (QK_ROPE // 2)
    sin_signed = jnp.where(first, -sin, sin)

    k_rope = _rope(k_rope, cos, sin_signed, first)[:, :QK_ROPE].astype(k_ref.dtype)
    nope_w = MLA_HEADS * QK_NOPE
    for hp in range(MLA_HEADS // 2):
        q_rope = _rope(q[:, nope_w + hp * LANES: nope_w + (hp + 1) * LANES], cos, sin_signed, first)
        for hh in range(2):
            hd = 2 * hp + hh
            q_ref[hd, :, :QK_NOPE] = q[:, hd * QK_NOPE:(hd + 1) * QK_NOPE].astype(q_ref.dtype)
            q_ref[hd, :, QK_NOPE:] = q_rope[:, hh * QK_ROPE:(hh + 1) * QK_ROPE].astype(q_ref.dtype)
    for hd in range(MLA_HEADS):
        k_ref[hd, :, :QK_NOPE] = kv[:, hd * QK_NOPE:(hd + 1) * QK_NOPE].astype(k_ref.dtype)
        k_ref[hd, :, QK_NOPE:] = k_rope
        v_ref[hd, 0] = kv[:, nope_w + hd * V_DIM: nope_w + (hd + 1) * V_DIM].T.astype(v_ref.dtype)


def _mla_proj(x, cos, sin, ng, w_in, qn, w_uq, kvn, w_ukv):
    t, d = x.shape
    tm = MLA_TILE
    const = lambda shape: pl.BlockSpec(shape, lambda i: (0,) * len(shape))
    return pl.pallas_call(
        _mla_proj_kernel,
        grid=(t // tm,),
        in_specs=[pl.BlockSpec((tm, d), lambda i: (i, 0)),
                  pl.BlockSpec((tm, LANES), lambda i: (i, 0)),
                  pl.BlockSpec((tm, LANES), lambda i: (i, 0)),
                  const(ng.shape), const(w_in.shape), const(qn.shape), const(w_uq.shape),
                  const(kvn.shape), const(w_ukv.shape)],
        out_specs=[pl.BlockSpec((MLA_HEADS, tm, QK_DIM), lambda i: (0, i, 0)),
                   pl.BlockSpec((MLA_HEADS, tm, QK_DIM), lambda i: (0, i, 0)),
                   pl.BlockSpec((MLA_HEADS, 1, V_DIM, tm), lambda i: (0, i, 0, 0)),
                   pl.BlockSpec((tm, MLA_HEADS * V_DIM), lambda i: (i, 0))],
        out_shape=[jax.ShapeDtypeStruct((MLA_HEADS, t, QK_DIM), BF16),
                   jax.ShapeDtypeStruct((MLA_HEADS, t, QK_DIM), BF16),
                   jax.ShapeDtypeStruct((MLA_HEADS, t // tm, V_DIM, tm), BF16),
                   jax.ShapeDtypeStruct((t, MLA_HEADS * V_DIM), BF16)],
        compiler_params=_params("parallel"),
        name="mla_proj",
    )(x, cos, sin, ng, w_in, qn, w_uq, kvn, w_ukv)


def _attn_kernel(q_ref, k_ref, vt_ref, gate_ref, o_ref, st0_ref, st1_ref, acc_ref, *, blk, sub):
    qi = pl.program_id(2)
    q = q_ref[0]
    nsub = blk // sub

    def scores(kb, st_ref):
        start = pl.multiple_of(kb * blk, blk)
        st = _dot_nt(k_ref[0, pl.ds(start, blk), :], q)
        st_ref[...] = st
        return jnp.max(st, axis=0, keepdims=True)

    def update(kb, st_ref, st_max, m, masked):
        vt = jnp.concatenate([vt_ref[0, kb * nsub + j] for j in range(nsub)], axis=1)
        vt = jnp.concatenate([vt, jnp.ones((ONES_ROWS, blk), vt.dtype)], axis=0)
        st = st_ref[...]
        if masked:
            key = lax.broadcasted_iota(jnp.int32, st.shape, 0)
            qry = lax.broadcasted_iota(jnp.int32, st.shape, 1)
            st = jnp.where(key <= qry, st, MASK_VALUE)
            st_max = jnp.max(st, axis=0, keepdims=True)
        m_new = jnp.maximum(m, st_max)
        pt = jnp.exp2((st - m_new).astype(BF16))
        corr = jnp.exp2(m - m_new)
        acc_ref[...] = acc_ref[...] * corr + _dot(vt, pt)
        return m_new

    acc_ref[...] = jnp.zeros_like(acc_ref)
    m_init = jnp.full((1, blk), MASK_VALUE, F32)

    def body(j, c):
        mx, m = c
        mx1 = scores(2 * j + 1, st1_ref)
        m = update(2 * j, st0_ref, mx, m, False)
        mx2 = scores(2 * j + 2, st0_ref)
        return mx2, update(2 * j + 1, st1_ref, mx1, m, False)

    npairs = qi // 2
    c = lax.fori_loop(0, npairs // 2, lambda j, c: body(2 * j + 1, body(2 * j, c)),
                      (scores(0, st0_ref), m_init))
    c = lax.cond(npairs % 2 == 1, lambda c: body(npairs - 1, c), lambda c: c, c)

    def tail_even(c):
        update(qi, st0_ref, c[0], c[1], True)

    def tail_odd(c):
        mx1 = scores(qi, st1_ref)
        m = update(qi - 1, st0_ref, c[0], c[1], False)
        update(qi, st1_ref, mx1, m, True)

    lax.cond(qi % 2 == 0, tail_even, tail_odd, c)
    gate = gate_ref[...].astype(F32)
    out = acc_ref[:V_DIM, :] / acc_ref[V_DIM:V_DIM + 1, :]
    o_ref[...] = (out.T * _silu(gate)).astype(o_ref.dtype)


def _attention(q, k, vt, gate, bsz, seq):
    blk = ATTN_TILE
    sub = vt.shape[3]
    nq = seq // blk
    nv = seq // sub
    return pl.pallas_call(
        functools.partial(_attn_kernel, blk=blk, sub=sub),
        grid=(bsz, MLA_HEADS, nq),
        in_specs=[pl.BlockSpec((1, blk, QK_DIM), lambda b, h, i: (h, b * nq + i, 0)),
                  pl.BlockSpec((1, seq, QK_DIM), lambda b, h, i: (h, b, 0)),
                  pl.BlockSpec((1, nv, V_DIM, sub), lambda b, h, i: (h, b, 0, 0)),
                  pl.BlockSpec((blk, V_DIM), lambda b, h, i: (b * nq + i, h))],
        out_specs=pl.BlockSpec((blk, V_DIM), lambda b, h, i: (b * nq + i, h)),
        out_shape=jax.ShapeDtypeStruct((bsz * seq, MLA_HEADS * V_DIM), BF16),
        scratch_shapes=[pltpu.VMEM((blk, blk), F32), pltpu.VMEM((blk, blk), F32),
                        pltpu.VMEM((V_DIM + ONES_ROWS, blk), F32)],
        compiler_params=_params("parallel", "parallel", "arbitrary"),
        name="mla_attention",
    )(q, k, vt, gate)


def _seg_sum(x, seg_ref, segt_ref):
    s = jnp.dot(x.astype(BF16), seg_ref[...], preferred_element_type=F32)
    hi = s.astype(BF16)
    lo = (s - hi.astype(F32)).astype(BF16)
    return (jnp.dot(hi, segt_ref[...], preferred_element_type=F32)
            + jnp.dot(lo, segt_ref[...], preferred_element_type=F32))


def _finish(y, w_ref, x_ref, g_ref, o_ref, final_norm):
    x = x_ref[...] + jnp.dot(y.astype(BF16), w_ref[...], preferred_element_type=F32)
    if final_norm:
        x = _rms(x, g_ref[...])
    o_ref[...] = x


def _out_proj_kernel(y_ref, w_ref, x_ref, g_ref, o_ref, *, final_norm):
    _finish(y_ref[...], w_ref, x_ref, g_ref, o_ref, final_norm)


def _rwkv_out_kernel(y_ref, bv_ref, sg_ref, lng_ref, lnb_ref, seg_ref, segt_ref, w_ref, x_ref, g_ref,
                     o_ref, *, final_norm):
    y = y_ref[...].astype(F32)
    mean = _seg_sum(y, seg_ref, segt_ref) * (1.0 / RWKV_HEAD)
    yc = y - mean
    var = _seg_sum(yc * yc, seg_ref, segt_ref) * (1.0 / RWKV_HEAD)
    yn = yc * lax.rsqrt(var + GN_EPS) * lng_ref[...] + lnb_ref[...]
    out = (yn + bv_ref[...].astype(F32)) * sg_ref[...].astype(F32)
    _finish(out, w_ref, x_ref, g_ref, o_ref, final_norm)


def _out_proj(y, w, x, final_g, final_norm, gn=None):
    t, d = x.shape
    tm = OUT_TILE if gn is None else TOKEN_TILE
    tile = lambda a: pl.BlockSpec((tm, a.shape[1]), lambda i: (i, 0))
    const = lambda a: pl.BlockSpec(a.shape, lambda i: (0, 0))
    if gn is None:
        body, ops = _out_proj_kernel, [y]
        specs = [tile(y)]
    else:
        bv, sg, lng, lnb, seg, segt = gn
        body, ops = _rwkv_out_kernel, [y, bv, sg, lng, lnb, seg, segt]
        specs = [tile(y), tile(bv), tile(sg), const(lng), const(lnb), const(seg), const(segt)]
    return pl.pallas_call(
        functools.partial(body, final_norm=final_norm),
        grid=(t // tm,),
        in_specs=specs + [const(w), tile(x), const(final_g)],
        out_specs=pl.BlockSpec((tm, d), lambda i: (i, 0)),
        out_shape=jax.ShapeDtypeStruct((t, d), F32),
        compiler_params=_params("parallel"),
        name="out_proj" if gn is None else "rwkv_out",
    )(*ops, w, x, final_g)


def _split3(x):
    hi = x.astype(BF16)
    r1 = x - hi.astype(F32)
    mid = r1.astype(BF16)
    lo = (r1 - mid.astype(F32)).astype(BF16)
    return hi, mid, lo


def _rwkv_proj_kernel(x_ref, ng_ref, win_ref, mu_ref, w0_ref, ww2_ref, a0_ref, wa2_ref,
                      kk_ref, ka_ref, rk_ref, tri_ref, seg_ref, segt_ref,
                      rt_ref, at_ref, bt_ref, kt_ref, bh_ref, kh_ref, v_ref, wl_ref, bv_ref, sg_ref,
                      prev_ref, *, width):
    @pl.when(pl.program_id(1) == 0)
    def _():
        prev_ref[...] = jnp.zeros_like(prev_ref)

    h = _rms(x_ref[...], ng_ref[...])
    proj = _dot(h, win_ref[...])
    rows = proj.shape[0]
    rolled = pltpu.roll(proj, 1, 0)
    row = lax.broadcasted_iota(jnp.int32, (8, 1), 0)
    shifted = jnp.concatenate([jnp.where(row == 0, prev_ref[0:1, :], rolled[:8]), rolled[8:]], axis=0)
    prev_ref[0:1, :] = proj[rows - 1:rows, :]
    proj = proj + mu_ref[...] * (shifted - proj)

    lora = proj[:, 4 * width:]
    lane = lax.broadcasted_iota(jnp.int32, lora.shape, 1)
    lora = jnp.where(lane < LORA, jnp.tanh(lora), lora)
    z = w0_ref[...] + _dot(lora, ww2_ref[...])
    log_decay = (-LOG2E * EXP_M05) / (1.0 + jnp.exp(-z))
    a = 1.0 / (1.0 + jnp.exp(-(a0_ref[...] + _dot(lora, wa2_ref[...]))))

    r = proj[:, :width]
    k = proj[:, width:2 * width]
    v = proj[:, 2 * width:3 * width]
    sg_ref[...] = _silu(proj[:, 3 * width:4 * width]).astype(sg_ref.dtype)

    kk = k * kk_ref[...]
    kk = kk * lax.rsqrt(jnp.maximum(_seg_sum(kk * kk, seg_ref, segt_ref), 1e-24))
    kp = k * (1.0 + (a - 1.0) * ka_ref[...])
    b = kk * a
    bv_ref[...] = (_seg_sum(r * kp * rk_ref[...], seg_ref, segt_ref) * v).astype(bv_ref.dtype)

    tri = tri_ref[...]
    cum = sum(jnp.dot(tri, part, preferred_element_type=F32) for part in _split3(log_decay))
    nchunks = rows // CHUNK
    last = [cum[(c + 1) * CHUNK - 1:(c + 1) * CHUNK, :] for c in range(nchunks)]
    cum_last = jnp.concatenate([jnp.broadcast_to(x, (CHUNK, width)) for x in last], axis=0)
    wl_ref[...] = jnp.concatenate([jnp.broadcast_to(jnp.exp2(x), (8, width)) for x in last], axis=0)
    w_inv = jnp.exp2(-cum)
    w_tail = jnp.exp2(cum_last - cum)
    outs = ((rt_ref, r * jnp.exp2(cum)), (at_ref, -kk * jnp.exp2(cum - log_decay)),
            (bt_ref, b * w_inv), (kt_ref, kp * w_inv), (bh_ref, b * w_tail), (kh_ref, kp * w_tail),
            (v_ref, v))
    for ref, val in outs:
        for gi in range(width // GW):
            ref[gi] = val[:, gi * GW:(gi + 1) * GW].astype(ref.dtype)


def _rwkv_proj(x, ng, w_in, mu, w0, ww2, a0, wa2, kk, ka, rk, tri, seg, segt, bsz, seq):
    t, d = x.shape
    width = w0.shape[1]
    ngroups = width // GW
    tm = TOKEN_TILE
    ns = seq // tm
    const = lambda a: pl.BlockSpec(a.shape, lambda b, i: (0,) * a.ndim)
    consts = [ng, w_in, mu, w0, ww2, a0, wa2, kk, ka, rk, tri, seg, segt]
    gspec = pl.BlockSpec((ngroups, tm, GW), lambda b, i: (0, b * ns + i, 0))
    nspec = pl.BlockSpec((tm, width), lambda b, i: (b * ns + i, 0))
    wl_rows = tm // CHUNK * 8
    return pl.pallas_call(
        functools.partial(_rwkv_proj_kernel, width=width),
        grid=(bsz, ns),
        in_specs=[pl.BlockSpec((tm, d), lambda b, i: (b * ns + i, 0))] + [const(a) for a in consts],
        out_specs=[gspec] * 7 + [pl.BlockSpec((wl_rows, width), lambda b, i: (b * ns + i, 0)),
                                 nspec, nspec],
        out_shape=[jax.ShapeDtypeStruct((ngroups, t, GW), BF16)] * 7
        + [jax.ShapeDtypeStruct((t // CHUNK * 8, width), F32),
           jax.ShapeDtypeStruct((t, width), BF16), jax.ShapeDtypeStruct((t, width), BF16)],
        scratch_shapes=[pltpu.VMEM((8, w_in.shape[1]), F32)],
        compiler_params=_params("parallel", "arbitrary"),
        name="rwkv_proj",
    )(x, *consts)


def _block_diag(y, bd_mask):
    return jnp.where(bd_mask, jnp.concatenate([y.astype(BF16)] * GROUP_HEADS, axis=0), 0)


def _wkv_kernel(rt_ref, at_ref, bt_ref, kt_ref, bh_ref, kh_ref, v_ref, wl_ref, y_ref, state_ref, *,
                ngroups, nchunks):
    @pl.when(pl.program_id(1) == 0)
    def _():
        state_ref[...] = jnp.zeros_like(state_ref)

    t_idx = lax.broadcasted_iota(jnp.int32, (CHUNK, GW), 0)
    s_idx = lax.broadcasted_iota(jnp.int32, (CHUNK, GW), 1) % CHUNK
    incl = s_idx <= t_idx
    strict = s_idx < t_idx
    eye = (s_idx == t_idx).astype(F32)
    blk8 = strict & ((s_idx // 8) == (t_idx // 8))
    off_masks = [((s_idx // (2 * b)) == (t_idx // (2 * b))) & ((t_idx // b) % 2 == 1)
                 & ((s_idx // b) % 2 == 0) for b in (8, 16, 32)]
    bd_mask = (lax.broadcasted_iota(jnp.int32, (GW, GW), 0) // RWKV_HEAD
               == lax.broadcasted_iota(jnp.int32, (GW, GW), 1) // RWKV_HEAD)

    def bd(y):
        return _block_diag(y, bd_mask)

    def mm(x, y):
        return _dot(x, bd(y))

    gs = range(ngroups)
    chains = [(c, gi) for c in range(nchunks) for gi in gs]
    rows = [slice(c * CHUNK, (c + 1) * CHUNK) for c, _ in chains]
    r_t = [rt_ref[gi, rw, :] for (_, gi), rw in zip(chains, rows)]
    a_t = [at_ref[gi, rw, :] for (_, gi), rw in zip(chains, rows)]
    v = [v_ref[gi, rw, :] for (_, gi), rw in zip(chains, rows)]

    amat = [_dot_nt(jnp.concatenate([r_t[n], a_t[n]], axis=0),
                    jnp.concatenate([bd(bt_ref[gi, rows[n], :]), bd(kt_ref[gi, rows[n], :])], axis=0))
            for n, (_, gi) in enumerate(chains)]
    a_rb = [jnp.where(incl, x[:CHUNK, :GW], 0.0) for x in amat]
    a_rk = [jnp.where(incl, x[:CHUNK, GW:], 0.0) for x in amat]
    m_ab = [jnp.where(strict, x[CHUNK:, :GW], 0.0) for x in amat]
    a_ak = [jnp.where(strict, x[CHUNK:, GW:], 0.0) for x in amat]

    n1 = [jnp.where(blk8, x, 0.0) for x in m_ab]
    n2 = [mm(x, x) for x in n1]
    n4 = [mm(x, x) for x in n2]
    tmat = [eye + x for x in n1]
    tmat = [x + mm(x, y) for x, y in zip(tmat, n2)]
    tmat = [x + mm(x, y) for x, y in zip(tmat, n4)]
    for off in off_masks:
        z = [mm(jnp.where(off, m, 0.0), x) for m, x in zip(m_ab, tmat)]
        tmat = [x + mm(x, y) for x, y in zip(tmat, z)]

    akv = [mm(x, y) for x, y in zip(a_ak, v)]
    pq = [_dot(tmat[n], jnp.concatenate([bd(a_t[n]), bd(akv[n])], axis=1))
          for n in range(len(chains))]

    state = [state_ref[gi] for gi in gs]
    for c in range(nchunks):
        ns = [c * ngroups + gi for gi in gs]
        us = [_dot_nt(jnp.concatenate([pq[n][:, :GW].astype(BF16), r_t[n]], axis=0), state[gi])
              for gi, n in zip(gs, ns)]
        u = [us[gi][:CHUNK] + pq[n][:, GW:] for gi, n in zip(gs, ns)]
        y = [us[gi][CHUNK:]
             + _dot(jnp.concatenate([a_rb[n], a_rk[n]], axis=1),
                    jnp.concatenate([bd(u[gi]), bd(v[n])], axis=0)) for gi, n in zip(gs, ns)]
        upd = [_dot(jnp.concatenate([u[gi], v[n].astype(F32)], axis=0).T,
                    jnp.concatenate([bh_ref[gi, rows[n], :], kh_ref[gi, rows[n], :]], axis=0))
               for gi, n in zip(gs, ns)]
        wl = wl_ref[c * 8:c * 8 + 1, :]
        state = [state[gi] * wl[:, gi * GW:(gi + 1) * GW] + jnp.where(bd_mask, upd[gi], 0.0)
                 for gi in gs]
        for gi in gs:
            y_ref[rows[ns[gi]], gi * GW:(gi + 1) * GW] = y[gi].astype(y_ref.dtype)
    for gi in gs:
        state_ref[gi] = state[gi]


def _wkv(rt, at, bt, kt, bh, kh, v, wl, bsz, seq):
    ngroups, t, _ = rt.shape
    tt = WKV_TILE
    ns = seq // tt
    gspec = pl.BlockSpec((ngroups, tt, GW), lambda b, i: (0, b * ns + i, 0))
    return pl.pallas_call(
        functools.partial(_wkv_kernel, ngroups=ngroups, nchunks=tt // CHUNK),
        grid=(bsz, ns),
        in_specs=[gspec] * 7 + [pl.BlockSpec((tt // CHUNK * 8, ngroups * GW),
                                             lambda b, i: (b * ns + i, 0))],
        out_specs=pl.BlockSpec((tt, ngroups * GW), lambda b, i: (b * ns + i, 0)),
        out_shape=jax.ShapeDtypeStruct((t, ngroups * GW), BF16),
        scratch_shapes=[pltpu.VMEM((ngroups, GW, GW), F32)],
        compiler_params=_params("parallel", "arbitrary"),
        name="rwkv_wkv",
    )(rt, at, bt, kt, bh, kh, v, wl)


def _row(v):
    return v.reshape(1, -1).astype(F32)


def kernel(x, positions, norm_g, mla_w_in, mla_q_norm, mla_w_uq, mla_kv_norm, mla_w_ukv, mla_w_out, rwkv_w_in, rwkv_mu, rwkv_w0, rwkv_w_w2, rwkv_a0, rwkv_w_a2, rwkv_k_k, rwkv_k_a, rwkv_r_k, rwkv_ln_g, rwkv_ln_b, rwkv_w_out, final_g):
    bsz, seq, d = x.shape
    depth = norm_g.shape[0]
    t = bsz * seq
    xf = x.reshape(t, d)
    half = QK_ROPE // 2
    inv_freq = 1.0 / (ROPE_THETA ** (jnp.arange(half, dtype=F32) / half))
    ang = positions.reshape(t, 1).astype(F32) * inv_freq[None, :]
    cos = jnp.tile(jnp.cos(ang), (1, LANES // half))
    sin = jnp.tile(jnp.sin(ang), (1, LANES // half))
    final_row = _row(final_g)

    width = rwkv_w0.shape[1]
    tok = jnp.arange(TOKEN_TILE)
    tri = ((tok[:, None] // CHUNK == tok[None, :] // CHUNK) & (tok[None, :] <= tok[:, None])).astype(BF16)
    seg = (jnp.arange(width)[:, None] // RWKV_HEAD == jnp.arange(LANES)[None, :]).astype(BF16)
    segt = seg.T

    for i in range(depth):
        j = i // 2
        ng = _row(norm_g[i])
        last = i == depth - 1
        if i % 2 == 0:
            w_in = mla_w_in[j]
            lat = Q_LORA + KV_LORA
            w_in = jnp.concatenate(
                [w_in[:, :lat], w_in[:, lat + QK_ROPE:], w_in[:, lat:lat + QK_ROPE],
                 jnp.zeros((d, LANES - QK_ROPE), w_in.dtype)], axis=1).astype(BF16)
            w_uq = mla_w_uq[j].reshape(Q_LORA, MLA_HEADS, QK_DIM)
            w_uq = jnp.concatenate([w_uq[:, :, :QK_NOPE].reshape(Q_LORA, -1),
                                    w_uq[:, :, QK_NOPE:].reshape(Q_LORA, -1)], axis=1).astype(BF16)
            w_ukv = mla_w_ukv[j].reshape(KV_LORA, MLA_HEADS, QK_NOPE + V_DIM)
            w_ukv = jnp.concatenate([w_ukv[:, :, :QK_NOPE].reshape(KV_LORA, -1),
                                     w_ukv[:, :, QK_NOPE:].reshape(KV_LORA, -1)], axis=1).astype(BF16)
            q, k, vt, gate = _mla_proj(xf, cos, sin, ng, w_in, _row(mla_q_norm[j]), w_uq,
                                       _row(mla_kv_norm[j]), w_ukv)
            y = _attention(q, k, vt, gate, bsz, seq)
            xf = _out_proj(y, mla_w_out[j].astype(BF16), xf, final_row, last)
        else:
            zeros = jnp.zeros((LORA, width), F32)
            ww2 = jnp.concatenate([rwkv_w_w2[j], zeros], axis=0).astype(BF16)
            wa2 = jnp.concatenate([zeros, rwkv_w_a2[j]], axis=0).astype(BF16)
            *ops, wl, bv, sg = _rwkv_proj(
                xf, ng, rwkv_w_in[j].astype(BF16), _row(rwkv_mu[j]), _row(rwkv_w0[j]), ww2,
                _row(rwkv_a0[j]), wa2, _row(rwkv_k_k[j]), _row(rwkv_k_a[j]), _row(rwkv_r_k[j]),
                tri, seg, segt, bsz, seq)
            y = _wkv(*ops, wl, bsz, seq)
            gn = (bv, sg, _row(rwkv_ln_g[j]), _row(rwkv_ln_b[j]), seg, segt)
            xf = _out_proj(y, rwkv_w_out[j].astype(BF16), xf, final_row, last, gn)
    return xf.reshape(bsz, seq, d)
```

```python
import functools

import jax
import jax.numpy as jnp
from jax import lax
from jax.experimental import pallas as pl
from jax.experimental.pallas import tpu as pltpu

F32 = jnp.float32
BF16 = jnp.bfloat16

NORM_EPS = 1e-6
GN_EPS = 64e-5
ROPE_THETA = 10000.0

MLA_HEADS = 8
QK_NOPE = 128
QK_ROPE = 64
QK_DIM = QK_NOPE + QK_ROPE
V_DIM = 128
Q_LORA = 768
KV_LORA = 256

RWKV_HEAD = 64
LORA = 64

LANES = 128
CHUNK = 64
GROUP_HEADS = 2
GW = GROUP_HEADS * RWKV_HEAD

TOKEN_TILE = 256
MLA_TILE = 512
OUT_TILE = 512
ATTN_TILE = 512
WKV_TILE = 256
VMEM_LIMIT = 56 * 1024 * 1024
MASK_VALUE = -1e30
ONES_ROWS = 16
LOG2E = 1.4426950408889634
EXP_M05 = 0.6065306597126334


def _params(*sem):
    return pltpu.CompilerParams(dimension_semantics=sem, vmem_limit_bytes=VMEM_LIMIT)


def _rms(x, g):
    return x * lax.rsqrt(jnp.mean(x * x, axis=-1, keepdims=True) + NORM_EPS) * g


def _dot(a, b):
    return jnp.dot(a.astype(BF16), b.astype(BF16), preferred_element_type=F32)


def _dot_nt(a, b):
    return lax.dot_general(a.astype(BF16), b.astype(BF16), (((1,), (1,)), ((), ())),
                           preferred_element_type=F32)


def _silu(x):
    return x / (1.0 + jnp.exp(-x))


def _rope(t, cos, sin_signed, first):
    partner = jnp.where(first, pltpu.roll(t, LANES - QK_ROPE // 2, 1), pltpu.roll(t, QK_ROPE // 2, 1))
    return t * cos + partner * sin_signed


def _mla_proj_kernel(x_ref, cos_ref, sin_ref, ng_ref, win_ref, qn_ref, wuq_ref, kvn_ref, wukv_ref,
                     q_ref, k_ref, v_ref, gate_ref):
    x = x_ref[...]
    h = _rms(x, ng_ref[...])
    proj = _dot(h, win_ref[...])
    q_lat = proj[:, :Q_LORA]
    kv_lat = proj[:, Q_LORA:Q_LORA + KV_LORA]
    gate_ref[...] = proj[:, 1024:2048].astype(gate_ref.dtype)
    k_rope = proj[:, 2048:2048 + LANES]

    q = _dot(_rms(q_lat, qn_ref[...]), wuq_ref[...]) * (QK_DIM ** -0.5 * LOG2E)
    kv = _dot(_rms(kv_lat, kvn_ref[...]), wukv_ref[...])

    cos = cos_ref[...]
    sin = sin_ref[...]
    lane = lax.broadcasted_iota(jnp.int32, cos.shape, 1)
    first = (lane % QK_ROPE) < (QK_ROPE // 2)
    sin_signed = jnp.where(first, -sin, sin)

    k_rope = _rope(k_rope, cos, sin_signed, first)[:, :QK_ROPE].astype(k_ref.dtype)
    nope_w = MLA_HEADS * QK_NOPE
    for hp in range(MLA_HEADS // 2):
        q_rope = _rope(q[:, nope_w + hp * LANES: nope_w + (hp + 1) * LANES], cos, sin_signed, first)
        for hh in range(2):
            hd = 2 * hp + hh
            q_ref[hd, :, :QK_NOPE] = q[:, hd * QK_NOPE:(hd + 1) * QK_NOPE].astype(q_ref.dtype)
            q_ref[hd, :, QK_NOPE:] = q_rope[:, hh * QK_ROPE:(hh + 1) * QK_ROPE].astype(q_ref.dtype)
    for hd in range(MLA_HEADS):
        k_ref[hd, :, :QK_NOPE] = kv[:, hd * QK_NOPE:(hd + 1) * QK_NOPE].astype(k_ref.dtype)
        k_ref[hd, :, QK_NOPE:] = k_rope
        v_ref[hd, 0] = kv[:, nope_w + hd * V_DIM: nope_w + (hd + 1) * V_DIM].T.astype(v_ref.dtype)


def _mla_proj(x, cos, sin, ng, w_in, qn, w_uq, kvn, w_ukv):
    t, d = x.shape
    tm = MLA_TILE
    const = lambda shape: pl.BlockSpec(shape, lambda i: (0,) * len(shape))
    return pl.pallas_call(
        _mla_proj_kernel,
        grid=(t // tm,),
        in_specs=[pl.BlockSpec((tm, d), lambda i: (i, 0)),
                  pl.BlockSpec((tm, LANES), lambda i: (i, 0)),
                  pl.BlockSpec((tm, LANES), lambda i: (i, 0)),
                  const(ng.shape), const(w_in.shape), const(qn.shape), const(w_uq.shape),
                  const(kvn.shape), const(w_ukv.shape)],
        out_specs=[pl.BlockSpec((MLA_HEADS, tm, QK_DIM), lambda i: (0, i, 0)),
                   pl.BlockSpec((MLA_HEADS, tm, QK_DIM), lambda i: (0, i, 0)),
                   pl.BlockSpec((MLA_HEADS, 1, V_DIM, tm), lambda i: (0, i, 0, 0)),
                   pl.BlockSpec((tm, MLA_HEADS * V_DIM), lambda i: (i, 0))],
        out_shape=[jax.ShapeDtypeStruct((MLA_HEADS, t, QK_DIM), BF16),
                   jax.ShapeDtypeStruct((MLA_HEADS, t, QK_DIM), BF16),
                   jax.ShapeDtypeStruct((MLA_HEADS, t // tm, V_DIM, tm), BF16),
                   jax.ShapeDtypeStruct((t, MLA_HEADS * V_DIM), BF16)],
        compiler_params=_params("parallel"),
        name="mla_proj",
    )(x, cos, sin, ng, w_in, qn, w_uq, kvn, w_ukv)


def _attn_kernel(q_ref, k_ref, vt_ref, gate_ref, o_ref, st0_ref, st1_ref, acc_ref, *, blk, sub, nq):
    nsub = blk // sub

    def query_block(qi, carry):
        rows = pl.ds(pl.multiple_of(qi * blk, blk), blk)
        q = q_ref[0, rows, :]

        def scores(kb, st_ref):
            start = pl.multiple_of(kb * blk, blk)
            st = _dot_nt(k_ref[0, pl.ds(start, blk), :], q)
            st_ref[...] = st
            return jnp.max(st, axis=0, keepdims=True)

        def update(kb, st_ref, st_max, m, masked):
            vt = jnp.concatenate([vt_ref[0, kb * nsub + j] for j in range(nsub)], axis=1)
            vt = jnp.concatenate([vt, jnp.ones((ONES_ROWS, blk), vt.dtype)], axis=0)
            st = st_ref[...]
            if masked:
                key = lax.broadcasted_iota(jnp.int32, st.shape, 0)
                qry = lax.broadcasted_iota(jnp.int32, st.shape, 1)
                st = jnp.where(key <= qry, st, MASK_VALUE)
                st_max = jnp.max(st, axis=0, keepdims=True)
            m_new = jnp.maximum(m, st_max)
            pt = jnp.exp2((st - m_new).astype(BF16))
            corr = jnp.exp2(m - m_new)
            acc_ref[...] = acc_ref[...] * corr + _dot(vt, pt)
            return m_new

        acc_ref[...] = jnp.zeros_like(acc_ref)
        m_init = jnp.full((1, blk), MASK_VALUE, F32)

        def body(j, c):
            mx, m = c
            mx1 = scores(2 * j + 1, st1_ref)
            m = update(2 * j, st0_ref, mx, m, False)
            mx2 = scores(2 * j + 2, st0_ref)
            return mx2, update(2 * j + 1, st1_ref, mx1, m, False)

        npairs = qi // 2
        c = lax.fori_loop(0, npairs // 2, lambda j, c: body(2 * j + 1, body(2 * j, c)),
                          (scores(0, st0_ref), m_init))
        c = lax.cond(npairs % 2 == 1, lambda c: body(npairs - 1, c), lambda c: c, c)

        def tail_even(c):
            update(qi, st0_ref, c[0], c[1], True)

        def tail_odd(c):
            mx1 = scores(qi, st1_ref)
            m = update(qi - 1, st0_ref, c[0], c[1], False)
            update(qi, st1_ref, mx1, m, True)

        lax.cond(qi % 2 == 0, tail_even, tail_odd, c)
        gate = gate_ref[rows, :].astype(F32)
        out = acc_ref[:V_DIM, :] / acc_ref[V_DIM:V_DIM + 1, :]
        o_ref[rows, :] = (out.T * _silu(gate)).astype(o_ref.dtype)
        return carry

    lax.fori_loop(0, nq, query_block, 0)


def _attention(q, k, vt, gate, bsz, seq):
    blk = ATTN_TILE
    sub = vt.shape[3]
    nv = seq // sub
    return pl.pallas_call(
        functools.partial(_attn_kernel, blk=blk, sub=sub, nq=seq // blk),
        grid=(bsz, MLA_HEADS),
        in_specs=[pl.BlockSpec((1, seq, QK_DIM), lambda b, h: (h, b, 0)),
                  pl.BlockSpec((1, seq, QK_DIM), lambda b, h: (h, b, 0)),
                  pl.BlockSpec((1, nv, V_DIM, sub), lambda b, h: (h, b, 0, 0)),
                  pl.BlockSpec((seq, V_DIM), lambda b, h: (b, h))],
        out_specs=pl.BlockSpec((seq, V_DIM), lambda b, h: (b, h)),
        out_shape=jax.ShapeDtypeStruct((bsz * seq, MLA_HEADS * V_DIM), BF16),
        scratch_shapes=[pltpu.VMEM((blk, blk), F32), pltpu.VMEM((blk, blk), F32),
                        pltpu.VMEM((V_DIM + ONES_ROWS, blk), F32)],
        compiler_params=_params("parallel", "parallel"),
        name="mla_attention",
    )(q, k, vt, gate)


def _seg_sum(x, seg_ref, segt_ref):
    s = jnp.dot(x.astype(BF16), seg_ref[...], preferred_element_type=F32)
    hi = s.astype(BF16)
    lo = (s - hi.astype(F32)).astype(BF16)
    return (jnp.dot(hi, segt_ref[...], preferred_element_type=F32)
            + jnp.dot(lo, segt_ref[...], preferred_element_type=F32))


def _finish(y, w_ref, x_ref, g_ref, o_ref, final_norm):
    x = x_ref[...] + jnp.dot(y.astype(BF16), w_ref[...], preferred_element_type=F32)
    if final_norm:
        x = _rms(x, g_ref[...])
    o_ref[...] = x


def _out_proj_kernel(y_ref, w_ref, x_ref, g_ref, o_ref, *, final_norm):
    _finish(y_ref[...], w_ref, x_ref, g_ref, o_ref, final_norm)


def _rwkv_out_kernel(y_ref, bv_ref, sg_ref, lng_ref, lnb_ref, seg_ref, segt_ref, w_ref, x_ref, g_ref,
                     o_ref, *, final_norm):
    y = y_ref[...].astype(F32)
    mean = _seg_sum(y, seg_ref, segt_ref) * (1.0 / RWKV_HEAD)
    yc = y - mean
    var = _seg_sum(yc * yc, seg_ref, segt_ref) * (1.0 / RWKV_HEAD)
    yn = yc * lax.rsqrt(var + GN_EPS) * lng_ref[...] + lnb_ref[...]
    out = (yn + bv_ref[...].astype(F32)) * sg_ref[...].astype(F32)
    _finish(out, w_ref, x_ref, g_ref, o_ref, final_norm)


def _out_proj(y, w, x, final_g, final_norm, gn=None):
    t, d = x.shape
    tm = OUT_TILE if gn is None else TOKEN_TILE
    tile = lambda a: pl.BlockSpec((tm, a.shape[1]), lambda i: (i, 0))
    const = lambda a: pl.BlockSpec(a.shape, lambda i: (0, 0))
    if gn is None:
        body, ops = _out_proj_kernel, [y]
        specs = [tile(y)]
    else:
        bv, sg, lng, lnb, seg, segt = gn
        body, ops = _rwkv_out_kernel, [y, bv, sg, lng, lnb, seg, segt]
        specs = [tile(y), tile(bv), tile(sg), const(lng), const(lnb), const(seg), const(segt)]
    return pl.pallas_call(
        functools.partial(body, final_norm=final_norm),
        grid=(t // tm,),
        in_specs=specs + [const(w), tile(x), const(final_g)],
        out_specs=pl.BlockSpec((tm, d), lambda i: (i, 0)),
        out_shape=jax.ShapeDtypeStruct((t, d), F32),
        compiler_params=_params("parallel"),
        name="out_proj" if gn is None else "rwkv_out",
    )(*ops, w, x, final_g)


def _split3(x):
    hi = x.astype(BF16)
    r1 = x - hi.astype(F32)
    mid = r1.astype(BF16)
    lo = (r1 - mid.astype(F32)).astype(BF16)
    return hi, mid, lo


def _rwkv_proj_kernel(x_ref, ng_ref, win_ref, mu_ref, w0_ref, ww2_ref, a0_ref, wa2_ref,
                      kk_ref, ka_ref, rk_ref, tri_ref, seg_ref, segt_ref,
                      rt_ref, at_ref, bt_ref, kt_ref, bh_ref, kh_ref, v_ref, wl_ref, bv_ref, sg_ref,
                      prev_ref, *, width):
    @pl.when(pl.program_id(1) == 0)
    def _():
        prev_ref[...] = jnp.zeros_like(prev_ref)

    h = _rms(x_ref[...], ng_ref[...])
    proj = _dot(h, win_ref[...])
    rows = proj.shape[0]
    rolled = pltpu.roll(proj, 1, 0)
    row = lax.broadcasted_iota(jnp.int32, (8, 1), 0)
    shifted = jnp.concatenate([jnp.where(row == 0, prev_ref[0:1, :], rolled[:8]), rolled[8:]], axis=0)
    prev_ref[0:1, :] = proj[rows - 1:rows, :]
    proj = proj + mu_ref[...] * (shifted - proj)

    lora = proj[:, 4 * width:]
    lane = lax.broadcasted_iota(jnp.int32, lora.shape, 1)
    lora = jnp.where(lane < LORA, jnp.tanh(lora), lora)
    z = w0_ref[...] + _dot(lora, ww2_ref[...])
    log_decay = (-LOG2E * EXP_M05) / (1.0 + jnp.exp(-z))
    a = 1.0 / (1.0 + jnp.exp(-(a0_ref[...] + _dot(lora, wa2_ref[...]))))

    r = proj[:, :width]
    k = proj[:, width:2 * width]
    v = proj[:, 2 * width:3 * width]
    sg_ref[...] = _silu(proj[:, 3 * width:4 * width]).astype(sg_ref.dtype)

    kk = k * kk_ref[...]
    kk = kk * lax.rsqrt(jnp.maximum(_seg_sum(kk * kk, seg_ref, segt_ref), 1e-24))
    kp = k * (1.0 + (a - 1.0) * ka_ref[...])
    b = kk * a
    bv_ref[...] = (_seg_sum(r * kp * rk_ref[...], seg_ref, segt_ref) * v).astype(bv_ref.dtype)

    tri = tri_ref[...]
    cum = sum(jnp.dot(tri, part, preferred_element_type=F32) for part in _split3(log_decay))
    nchunks = rows // CHUNK
    last = [cum[(c + 1) * CHUNK - 1:(c + 1) * CHUNK, :] for c in range(nchunks)]
    cum_last = jnp.concatenate([jnp.broadcast_to(x, (CHUNK, width)) for x in last], axis=0)
    wl_ref[...] = jnp.concatenate([jnp.broadcast_to(jnp.exp2(x), (8, width)) for x in last], axis=0)
    w_inv = jnp.exp2(-cum)
    w_tail = jnp.exp2(cum_last - cum)
    outs = ((rt_ref, r * jnp.exp2(cum)), (at_ref, -kk * jnp.exp2(cum - log_decay)),
            (bt_ref, b * w_inv), (kt_ref, kp * w_inv), (bh_ref, b * w_tail), (kh_ref, kp * w_tail),
            (v_ref, v))
    for ref, val in outs:
        for gi in range(width // GW):
            ref[gi] = val[:, gi * GW:(gi + 1) * GW].astype(ref.dtype)


def _rwkv_proj(x, ng, w_in, mu, w0, ww2, a0, wa2, kk, ka, rk, tri, seg, segt, bsz, seq):
    t, d = x.shape
    width = w0.shape[1]
    ngroups = width // GW
    tm = TOKEN_TILE
    ns = seq // tm
    const = lambda a: pl.BlockSpec(a.shape, lambda b, i: (0,) * a.ndim)
    consts = [ng, w_in, mu, w0, ww2, a0, wa2, kk, ka, rk, tri, seg, segt]
    gspec = pl.BlockSpec((ngroups, tm, GW), lambda b, i: (0, b * ns + i, 0))
    nspec = pl.BlockSpec((tm, width), lambda b, i: (b * ns + i, 0))
    wl_rows = tm // CHUNK * 8
    return pl.pallas_call(
        functools.partial(_rwkv_proj_kernel, width=width),
        grid=(bsz, ns),
        in_specs=[pl.BlockSpec((tm, d), lambda b, i: (b * ns + i, 0))] + [const(a) for a in consts],
        out_specs=[gspec] * 7 + [pl.BlockSpec((wl_rows, width), lambda b, i: (b * ns + i, 0)),
                                 nspec, nspec],
        out_shape=[jax.ShapeDtypeStruct((ngroups, t, GW), BF16)] * 7
        + [jax.ShapeDtypeStruct((t // CHUNK * 8, width), F32),
           jax.ShapeDtypeStruct((t, width), BF16), jax.ShapeDtypeStruct((t, width), BF16)],
        scratch_shapes=[pltpu.VMEM((8, w_in.shape[1]), F32)],
        compiler_params=_params("parallel", "arbitrary"),
        name="rwkv_proj",
    )(x, *consts)


def _block_diag(y, bd_mask):
    return jnp.where(bd_mask, jnp.concatenate([y.astype(BF16)] * GROUP_HEADS, axis=0), 0)


def _wkv_kernel(rt_ref, at_ref, bt_ref, kt_ref, bh_ref, kh_ref, v_ref, wl_ref, y_ref, state_ref, *,
                ngroups, nchunks):
    @pl.when(pl.program_id(1) == 0)
    def _():
        state_ref[...] = jnp.zeros_like(state_ref)

    t_idx = lax.broadcasted_iota(jnp.int32, (CHUNK, GW), 0)
    s_idx = lax.broadcasted_iota(jnp.int32, (CHUNK, GW), 1) % CHUNK
    incl = s_idx <= t_idx
    strict = s_idx < t_idx
    eye = (s_idx == t_idx).astype(F32)
    blk8 = strict & ((s_idx // 8) == (t_idx // 8))
    off_masks = [((s_idx // (2 * b)) == (t_idx // (2 * b))) & ((t_idx // b) % 2 == 1)
                 & ((s_idx // b) % 2 == 0) for b in (8, 16, 32)]
    bd_mask = (lax.broadcasted_iota(jnp.int32, (GW, GW), 0) // RWKV_HEAD
               == lax.broadcasted_iota(jnp.int32, (GW, GW), 1) // RWKV_HEAD)

    def bd(y):
        return _block_diag(y, bd_mask)

    def mm(x, y):
        return _dot(x, bd(y))

    gs = range(ngroups)
    chains = [(c, gi) for c in range(nchunks) for gi in gs]
    rows = [slice(c * CHUNK, (c + 1) * CHUNK) for c, _ in chains]
    r_t = [rt_ref[gi, rw, :] for (_, gi), rw in zip(chains, rows)]
    a_t = [at_ref[gi, rw, :] for (_, gi), rw in zip(chains, rows)]
    v = [v_ref[gi, rw, :] for (_, gi), rw in zip(chains, rows)]

    amat = [_dot_nt(jnp.concatenate([r_t[n], a_t[n]], axis=0),
                    jnp.concatenate([bd(bt_ref[gi, rows[n], :]), bd(kt_ref[gi, rows[n], :])], axis=0))
            for n, (_, gi) in enumerate(chains)]
    a_rb = [jnp.where(incl, x[:CHUNK, :GW], 0.0) for x in amat]
    a_rk = [jnp.where(incl, x[:CHUNK, GW:], 0.0) for x in amat]
    m_ab = [jnp.where(strict, x[CHUNK:, :GW], 0.0) for x in amat]
    a_ak = [jnp.where(strict, x[CHUNK:, GW:], 0.0) for x in amat]

    n1 = [jnp.where(blk8, x, 0.0) for x in m_ab]
    n2 = [mm(x, x) for x in n1]
    n4 = [mm(x, x) for x in n2]
    tmat = [eye + x for x in n1]
    tmat = [x + mm(x, y) for x, y in zip(tmat, n2)]
    tmat = [x + mm(x, y) for x, y in zip(tmat, n4)]
    for off in off_masks:
        z = [mm(jnp.where(off, m, 0.0), x) for m, x in zip(m_ab, tmat)]
        tmat = [x + mm(x, y) for x, y in zip(tmat, z)]

    akv = [mm(x, y) for x, y in zip(a_ak, v)]
    pq = [_dot(tmat[n], jnp.concatenate([bd(a_t[n]), bd(akv[n])], axis=1))
          for n in range(len(chains))]

    state = [state_ref[gi] for gi in gs]
    for c in range(nchunks):
        ns = [c * ngroups + gi for gi in gs]
        us = [_dot_nt(jnp.concatenate([pq[n][:, :GW].astype(BF16), r_t[n]], axis=0), state[gi])
              for gi, n in zip(gs, ns)]
        u = [us[gi][:CHUNK] + pq[n][:, GW:] for gi, n in zip(gs, ns)]
        y = [us[gi][CHUNK:]
             + _dot(jnp.concatenate([a_rb[n], a_rk[n]], axis=1),
                    jnp.concatenate([bd(u[gi]), bd(v[n])], axis=0)) for gi, n in zip(gs, ns)]
        upd = [_dot(jnp.concatenate([u[gi], v[n].astype(F32)], axis=0).T,
                    jnp.concatenate([bh_ref[gi, rows[n], :], kh_ref[gi, rows[n], :]], axis=0))
               for gi, n in zip(gs, ns)]
        wl = wl_ref[c * 8:c * 8 + 1, :]
        state = [state[gi] * wl[:, gi * GW:(gi + 1) * GW] + jnp.where(bd_mask, upd[gi], 0.0)
                 for gi in gs]
        for gi in gs:
            y_ref[rows[ns[gi]], gi * GW:(gi + 1) * GW] = y[gi].astype(y_ref.dtype)
    for gi in gs:
        state_ref[gi] = state[gi]


def _wkv(rt, at, bt, kt, bh, kh, v, wl, bsz, seq):
    ngroups, t, _ = rt.shape
    tt = WKV_TILE
    ns = seq // tt
    gspec = pl.BlockSpec((ngroups, tt, GW), lambda b, i: (0, b * ns + i, 0))
    return pl.pallas_call(
        functools.partial(_wkv_kernel, ngroups=ngroups, nchunks=tt // CHUNK),
        grid=(bsz, ns),
        in_specs=[gspec] * 7 + [pl.BlockSpec((tt // CHUNK * 8, ngroups * GW),
                                             lambda b, i: (b * ns + i, 0))],
        out_specs=pl.BlockSpec((tt, ngroups * GW), lambda b, i: (b * ns + i, 0)),
        out_shape=jax.ShapeDtypeStruct((t, ngroups * GW), BF16),
        scratch_shapes=[pltpu.VMEM((ngroups, GW, GW), F32)],
        compiler_params=_params("parallel", "arbitrary"),
        name="rwkv_wkv",
    )(rt, at, bt, kt, bh, kh, v, wl)


def _row(v):
    return v.reshape(1, -1).astype(F32)


def kernel(x, positions, norm_g, mla_w_in, mla_q_norm, mla_w_uq, mla_kv_norm, mla_w_ukv, mla_w_out, rwkv_w_in, rwkv_mu, rwkv_w0, rwkv_w_w2, rwkv_a0, rwkv_w_a2, rwkv_k_k, rwkv_k_a, rwkv_r_k, rwkv_ln_g, rwkv_ln_b, rwkv_w_out, final_g):
    bsz, seq, d = x.shape
    depth = norm_g.shape[0]
    t = bsz * seq
    xf = x.reshape(t, d)
    half = QK_ROPE // 2
    inv_freq = 1.0 / (ROPE_THETA ** (jnp.arange(half, dtype=F32) / half))
    ang = positions.reshape(t, 1).astype(F32) * inv_freq[None, :]
    cos = jnp.tile(jnp.cos(ang), (1, LANES // half))
    sin = jnp.tile(jnp.sin(ang), (1, LANES // half))
    final_row = _row(final_g)

    width = rwkv_w0.shape[1]
    tok = jnp.arange(TOKEN_TILE)
    tri = ((tok[:, None] // CHUNK == tok[None, :] // CHUNK) & (tok[None, :] <= tok[:, None])).astype(BF16)
    seg = (jnp.arange(width)[:, None] // RWKV_HEAD == jnp.arange(LANES)[None, :]).astype(BF16)
    segt = seg.T

    for i in range(depth):
        j = i // 2
        ng = _row(norm_g[i])
        last = i == depth - 1
        if i % 2 == 0:
            w_in = mla_w_in[j]
            lat = Q_LORA + KV_LORA
            w_in = jnp.concatenate(
                [w_in[:, :lat], w_in[:, lat + QK_ROPE:], w_in[:, lat:lat + QK_ROPE],
                 jnp.zeros((d, LANES - QK_ROPE), w_in.dtype)], axis=1).astype(BF16)
            w_uq = mla_w_uq[j].reshape(Q_LORA, MLA_HEADS, QK_DIM)
            w_uq = jnp.concatenate([w_uq[:, :, :QK_NOPE].reshape(Q_LORA, -1),
                                    w_uq[:, :, QK_NOPE:].reshape(Q_LORA, -1)], axis=1).astype(BF16)
            w_ukv = mla_w_ukv[j].reshape(KV_LORA, MLA_HEADS, QK_NOPE + V_DIM)
            w_ukv = jnp.concatenate([w_ukv[:, :, :QK_NOPE].reshape(KV_LORA, -1),
                                     w_ukv[:, :, QK_NOPE:].reshape(KV_LORA, -1)], axis=1).astype(BF16)
            q, k, vt, gate = _mla_proj(xf, cos, sin, ng, w_in, _row(mla_q_norm[j]), w_uq,
                                       _row(mla_kv_norm[j]), w_ukv)
            y = _attention(q, k, vt, gate, bsz, seq)
            xf = _out_proj(y, mla_w_out[j].astype(BF16), xf, final_row, last)
        else:
            zeros = jnp.zeros((LORA, width), F32)
            ww2 = jnp.concatenate([rwkv_w_w2[j], zeros], axis=0).astype(BF16)
            wa2 = jnp.concatenate([zeros, rwkv_w_a2[j]], axis=0).astype(BF16)
            *ops, wl, bv, sg = _rwkv_proj(
                xf, ng, rwkv_w_in[j].astype(BF16), _row(rwkv_mu[j]), _row(rwkv_w0[j]), ww2,
                _row(rwkv_a0[j]), wa2, _row(rwkv_k_k[j]), _row(rwkv_k_a[j]), _row(rwkv_r_k[j]),
                tri, seg, segt, bsz, seq)
            y = _wkv(*ops, wl, bsz, seq)
            gn = (bv, sg, _row(rwkv_ln_g[j]), _row(rwkv_ln_b[j]), seg, segt)
            xf = _out_proj(y, rwkv_w_out[j].astype(BF16), xf, final_row, last, gn)
    return xf.reshape(bsz, seq, d)
```

```python
import functools

import jax
import jax.numpy as jnp
from jax import lax
from jax.experimental import pallas as pl
from jax.experimental.pallas import tpu as pltpu

F32 = jnp.float32
BF16 = jnp.bfloat16

NORM_EPS = 1e-6
GN_EPS = 64e-5
ROPE_THETA = 10000.0

MLA_HEADS = 8
QK_NOPE = 128
QK_ROPE = 64
QK_DIM = QK_NOPE + QK_ROPE
V_DIM = 128
Q_LORA = 768
KV_LORA = 256

RWKV_HEAD = 64
LORA = 64

LANES = 128
CHUNK = 64
GROUP_HEADS = 2
GW = GROUP_HEADS * RWKV_HEAD

TOKEN_TILE = 256
MLA_TILE = 512
OUT_TILE = 512
ATTN_TILE = 512
COL_BLOCK = 256
WKV_TILE = 256
VMEM_LIMIT = 56 * 1024 * 1024
MASK_VALUE = -1e30
ONES_ROWS = 16
LOG2E = 1.4426950408889634
EXP_M05 = 0.6065306597126334


def _params(*sem):
    return pltpu.CompilerParams(dimension_semantics=sem, vmem_limit_bytes=VMEM_LIMIT)


def _rms(x, g):
    return x * lax.rsqrt(jnp.mean(x * x, axis=-1, keepdims=True) + NORM_EPS) * g


def _dot(a, b):
    return jnp.dot(a.astype(BF16), b.astype(BF16), preferred_element_type=F32)


def _dot_nt(a, b):
    return lax.dot_general(a.astype(BF16), b.astype(BF16), (((1,), (1,)), ((), ())),
                           preferred_element_type=F32)


def _silu(x):
    return x / (1.0 + jnp.exp(-x))


def _rope(t, cos, sin_signed, first):
    partner = jnp.where(first, pltpu.roll(t, LANES - QK_ROPE // 2, 1), pltpu.roll(t, QK_ROPE // 2, 1))
    return t * cos + partner * sin_signed


def _mla_proj_kernel(x_ref, cos_ref, sin_ref, ng_ref, win_ref, qn_ref, wuq_ref, kvn_ref, wukv_ref,
                     q_ref, k_ref, v_ref, gate_ref):
    x = x_ref[...]
    h = _rms(x, ng_ref[...])
    proj = _dot(h, win_ref[...])
    q_lat = proj[:, :Q_LORA]
    kv_lat = proj[:, Q_LORA:Q_LORA + KV_LORA]
    gate_ref[...] = proj[:, 1024:2048].astype(gate_ref.dtype)
    k_rope = proj[:, 2048:2048 + LANES]

    q = _dot(_rms(q_lat, qn_ref[...]), wuq_ref[...]) * (QK_DIM ** -0.5 * LOG2E)
    kv = _dot(_rms(kv_lat, kvn_ref[...]), wukv_ref[...])

    cos = cos_ref[...]
    sin = sin_ref[...]
    lane = lax.broadcasted_iota(jnp.int32, cos.shape, 1)
    first = (lane % QK_ROPE) < (QK_ROPE // 2)
    sin_signed = jnp.where(first, -sin, sin)

    k_rope = _rope(k_rope, cos, sin_signed, first)[:, :QK_ROPE].astype(k_ref.dtype)
    nope_w = MLA_HEADS * QK_NOPE
    for hp in range(MLA_HEADS // 2):
        q_rope = _rope(q[:, nope_w + hp * LANES: nope_w + (hp + 1) * LANES], cos, sin_signed, first)
        for hh in range(2):
            hd = 2 * hp + hh
            q_ref[hd, :, :QK_NOPE] = q[:, hd * QK_NOPE:(hd + 1) * QK_NOPE].astype(q_ref.dtype)
            q_ref[hd, :, QK_NOPE:] = q_rope[:, hh * QK_ROPE:(hh + 1) * QK_ROPE].astype(q_ref.dtype)
    for hd in range(MLA_HEADS):
        k_ref[hd, :, :QK_NOPE] = kv[:, hd * QK_NOPE:(hd + 1) * QK_NOPE].astype(k_ref.dtype)
        k_ref[hd, :, QK_NOPE:] = k_rope
        v_ref[hd, 0] = kv[:, nope_w + hd * V_DIM: nope_w + (hd + 1) * V_DIM].T.astype(v_ref.dtype)


def _mla_proj(x, cos, sin, ng, w_in, qn, w_uq, kvn, w_ukv):
    t, d = x.shape
    tm = MLA_TILE
    const = lambda shape: pl.BlockSpec(shape, lambda i: (0,) * len(shape))
    return pl.pallas_call(
        _mla_proj_kernel,
        grid=(t // tm,),
        in_specs=[pl.BlockSpec((tm, d), lambda i: (i, 0)),
                  pl.BlockSpec((tm, LANES), lambda i: (i, 0)),
                  pl.BlockSpec((tm, LANES), lambda i: (i, 0)),
                  const(ng.shape), const(w_in.shape), const(qn.shape), const(w_uq.shape),
                  const(kvn.shape), const(w_ukv.shape)],
        out_specs=[pl.BlockSpec((MLA_HEADS, tm, QK_DIM), lambda i: (0, i, 0)),
                   pl.BlockSpec((MLA_HEADS, tm, QK_DIM), lambda i: (0, i, 0)),
                   pl.BlockSpec((MLA_HEADS, 1, V_DIM, tm), lambda i: (0, i, 0, 0)),
                   pl.BlockSpec((tm, MLA_HEADS * V_DIM), lambda i: (i, 0))],
        out_shape=[jax.ShapeDtypeStruct((MLA_HEADS, t, QK_DIM), BF16),
                   jax.ShapeDtypeStruct((MLA_HEADS, t, QK_DIM), BF16),
                   jax.ShapeDtypeStruct((MLA_HEADS, t // tm, V_DIM, tm), BF16),
                   jax.ShapeDtypeStruct((t, MLA_HEADS * V_DIM), BF16)],
        compiler_params=_params("parallel"),
        name="mla_proj",
    )(x, cos, sin, ng, w_in, qn, w_uq, kvn, w_ukv)


def _attn_kernel(q_ref, k_ref, vt_ref, gate_ref, o_ref, st0_ref, st1_ref, acc_ref, *, blk, sub, nq):
    nsub = blk // sub

    def query_block(qi, carry):
        rows = pl.ds(pl.multiple_of(qi * blk, blk), blk)
        q = q_ref[0, rows, :]

        def scores(kb, st_ref):
            start = pl.multiple_of(kb * blk, blk)
            st = _dot_nt(k_ref[0, pl.ds(start, blk), :], q)
            st_ref[...] = st
            return jnp.max(st, axis=0, keepdims=True)

        def update(kb, st_ref, st_max, m, masked):
            vt = jnp.concatenate([vt_ref[0, kb * nsub + j] for j in range(nsub)], axis=1)
            vt = jnp.concatenate([vt, jnp.ones((ONES_ROWS, blk), vt.dtype)], axis=0)
            st = st_ref[...]
            if masked:
                key = lax.broadcasted_iota(jnp.int32, st.shape, 0)
                qry = lax.broadcasted_iota(jnp.int32, st.shape, 1)
                st = jnp.where(key <= qry, st, MASK_VALUE)
                st_max = jnp.max(st, axis=0, keepdims=True)
            m_new = jnp.maximum(m, st_max)
            pt = jnp.exp2((st - m_new).astype(BF16))
            corr = jnp.exp2(m - m_new)
            acc_ref[...] = acc_ref[...] * corr + _dot(vt, pt)
            return m_new

        acc_ref[...] = jnp.zeros_like(acc_ref)
        m_init = jnp.full((1, blk), MASK_VALUE, F32)

        def body(j, c):
            mx, m = c
            mx1 = scores(2 * j + 1, st1_ref)
            m = update(2 * j, st0_ref, mx, m, False)
            mx2 = scores(2 * j + 2, st0_ref)
            return mx2, update(2 * j + 1, st1_ref, mx1, m, False)

        npairs = qi // 2
        c = lax.fori_loop(0, npairs // 2, lambda j, c: body(2 * j + 1, body(2 * j, c)),
                          (scores(0, st0_ref), m_init))
        c = lax.cond(npairs % 2 == 1, lambda c: body(npairs - 1, c), lambda c: c, c)

        def tail_even(c):
            update(qi, st0_ref, c[0], c[1], True)

        def tail_odd(c):
            mx1 = scores(qi, st1_ref)
            m = update(qi - 1, st0_ref, c[0], c[1], False)
            update(qi, st1_ref, mx1, m, True)

        lax.cond(qi % 2 == 0, tail_even, tail_odd, c)
        gate = gate_ref[rows, :].astype(F32)
        out = acc_ref[:V_DIM, :] / acc_ref[V_DIM:V_DIM + 1, :]
        o_ref[rows, :] = (out.T * _silu(gate)).astype(o_ref.dtype)
        return carry

    lax.fori_loop(0, nq, query_block, 0)


def _attention(q, k, vt, gate, bsz, seq):
    blk = ATTN_TILE
    sub = vt.shape[3]
    nv = seq // sub
    return pl.pallas_call(
        functools.partial(_attn_kernel, blk=blk, sub=sub, nq=seq // blk),
        grid=(bsz, MLA_HEADS),
        in_specs=[pl.BlockSpec((1, seq, QK_DIM), lambda b, h: (h, b, 0)),
                  pl.BlockSpec((1, seq, QK_DIM), lambda b, h: (h, b, 0)),
                  pl.BlockSpec((1, nv, V_DIM, sub), lambda b, h: (h, b, 0, 0)),
                  pl.BlockSpec((seq, V_DIM), lambda b, h: (b, h))],
        out_specs=pl.BlockSpec((seq, V_DIM), lambda b, h: (b, h)),
        out_shape=jax.ShapeDtypeStruct((bsz * seq, MLA_HEADS * V_DIM), BF16),
        scratch_shapes=[pltpu.VMEM((blk, blk), F32), pltpu.VMEM((blk, blk), F32),
                        pltpu.VMEM((V_DIM + ONES_ROWS, blk), F32)],
        compiler_params=_params("parallel", "parallel"),
        name="mla_attention",
    )(q, k, vt, gate)


def _seg_sum(x, seg_ref, segt_ref):
    s = jnp.dot(x.astype(BF16), seg_ref[...], preferred_element_type=F32)
    hi = s.astype(BF16)
    lo = (s - hi.astype(F32)).astype(BF16)
    return (jnp.dot(hi, segt_ref[...], preferred_element_type=F32)
            + jnp.dot(lo, segt_ref[...], preferred_element_type=F32))


def _finish(y, w_ref, x_ref, g_ref, o_ref, final_norm):
    x = x_ref[...] + jnp.dot(y.astype(BF16), w_ref[...], preferred_element_type=F32)
    if final_norm:
        x = _rms(x, g_ref[...])
    o_ref[...] = x


def _out_proj_kernel(y_ref, w_ref, x_ref, g_ref, o_ref, *, final_norm):
    _finish(y_ref[...], w_ref, x_ref, g_ref, o_ref, final_norm)


def _rwkv_out_kernel(y_ref, bv_ref, sg_ref, lng_ref, lnb_ref, seg_ref, segt_ref, w_ref, x_ref, g_ref,
                     o_ref, *, final_norm):
    y = y_ref[...].astype(F32)
    mean = _seg_sum(y, seg_ref, segt_ref) * (1.0 / RWKV_HEAD)
    yc = y - mean
    var = _seg_sum(yc * yc, seg_ref, segt_ref) * (1.0 / RWKV_HEAD)
    yn = yc * lax.rsqrt(var + GN_EPS) * lng_ref[...] + lnb_ref[...]
    out = (yn + bv_ref[...].astype(F32)) * sg_ref[...].astype(F32)
    _finish(out, w_ref, x_ref, g_ref, o_ref, final_norm)


def _out_proj(y, w, x, final_g, final_norm, gn=None):
    t, d = x.shape
    tm = OUT_TILE if gn is None else TOKEN_TILE
    tile = lambda a: pl.BlockSpec((tm, a.shape[1]), lambda i: (i, 0))
    const = lambda a: pl.BlockSpec(a.shape, lambda i: (0, 0))
    if gn is None:
        body, ops = _out_proj_kernel, [y]
        specs = [tile(y)]
    else:
        bv, sg, lng, lnb, seg, segt = gn
        body, ops = _rwkv_out_kernel, [y, bv, sg, lng, lnb, seg, segt]
        specs = [tile(y), tile(bv), tile(sg), const(lng), const(lnb), const(seg), const(segt)]
    return pl.pallas_call(
        functools.partial(body, final_norm=final_norm),
        grid=(t // tm,),
        in_specs=specs + [const(w), tile(x), const(final_g)],
        out_specs=pl.BlockSpec((tm, d), lambda i: (i, 0)),
        out_shape=jax.ShapeDtypeStruct((t, d), F32),
        compiler_params=_params("parallel"),
        name="out_proj" if gn is None else "rwkv_out",
    )(*ops, w, x, final_g)


def _split3(x):
    hi = x.astype(BF16)
    r1 = x - hi.astype(F32)
    mid = r1.astype(BF16)
    lo = (r1 - mid.astype(F32)).astype(BF16)
    return hi, mid, lo


def _rwkv_proj_kernel(x_ref, ng_ref, win_ref, mu_ref, w0_ref, ww2_ref, a0_ref, wa2_ref,
                      kk_ref, ka_ref, rk_ref, tri_ref, seg_ref, segt_ref,
                      rt_ref, at_ref, bt_ref, kt_ref, bh_ref, kh_ref, v_ref, wl_ref, bv_ref, sg_ref,
                      prev_ref, *, width):
    @pl.when(pl.program_id(1) == 0)
    def _():
        prev_ref[...] = jnp.zeros_like(prev_ref)

    h = _rms(x_ref[...], ng_ref[...]).astype(BF16)
    rows = h.shape[0]
    nchunks = rows // CHUNK
    row8 = lax.broadcasted_iota(jnp.int32, (8, 1), 0)
    tri = tri_ref[...]

    def project(cols):
        proj = jnp.dot(h, win_ref[:, cols], preferred_element_type=F32)
        rolled = pltpu.roll(proj, 1, 0)
        shifted = jnp.concatenate([jnp.where(row8 == 0, prev_ref[0:1, cols], rolled[:8]), rolled[8:]],
                                  axis=0)
        prev_ref[0:1, cols] = proj[rows - 1:rows, :]
        return proj + mu_ref[:, cols] * (shifted - proj)

    lora = project(slice(4 * width, 4 * width + 2 * LORA))
    lane = lax.broadcasted_iota(jnp.int32, lora.shape, 1)
    lora = jnp.where(lane < LORA, jnp.tanh(lora), lora).astype(BF16)

    for cb in range(width // COL_BLOCK):
        cs = slice(cb * COL_BLOCK, (cb + 1) * COL_BLOCK)
        r, k, v, g = (project(slice(part * width + cs.start, part * width + cs.stop))
                      for part in range(4))
        sg_ref[:, cs] = _silu(g).astype(sg_ref.dtype)

        def seg_sum(x):
            s = jnp.dot(x.astype(BF16), seg_ref[cs, :], preferred_element_type=F32)
            hi = s.astype(BF16)
            lo = (s - hi.astype(F32)).astype(BF16)
            return (jnp.dot(hi, segt_ref[:, cs], preferred_element_type=F32)
                    + jnp.dot(lo, segt_ref[:, cs], preferred_element_type=F32))

        z = w0_ref[:, cs] + jnp.dot(lora, ww2_ref[:, cs], preferred_element_type=F32)
        log_decay = (-LOG2E * EXP_M05) / (1.0 + jnp.exp(-z))
        a = 1.0 / (1.0 + jnp.exp(-(a0_ref[:, cs]
                                    + jnp.dot(lora, wa2_ref[:, cs], preferred_element_type=F32))))

        kk = k * kk_ref[:, cs]
        kk = kk * lax.rsqrt(jnp.maximum(seg_sum(kk * kk), 1e-24))
        kp = k * (1.0 + (a - 1.0) * ka_ref[:, cs])
        b = kk * a
        bv_ref[:, cs] = (seg_sum(r * kp * rk_ref[:, cs]) * v).astype(bv_ref.dtype)

        cum = sum(jnp.dot(tri, part, preferred_element_type=F32) for part in _split3(log_decay))
        last = [cum[(c + 1) * CHUNK - 1:(c + 1) * CHUNK, :] for c in range(nchunks)]
        cum_last = jnp.concatenate([jnp.broadcast_to(x, (CHUNK, COL_BLOCK)) for x in last], axis=0)
        wl_ref[:, cs] = jnp.concatenate([jnp.broadcast_to(jnp.exp2(x), (8, COL_BLOCK)) for x in last],
                                        axis=0)
        w_inv = jnp.exp2(-cum)
        w_tail = jnp.exp2(cum_last - cum)
        outs = ((rt_ref, r * jnp.exp2(cum)), (at_ref, -kk * jnp.exp2(cum - log_decay)),
                (bt_ref, b * w_inv), (kt_ref, kp * w_inv), (bh_ref, b * w_tail),
                (kh_ref, kp * w_tail), (v_ref, v))
        for ref, val in outs:
            for gj in range(COL_BLOCK // GW):
                ref[cb * (COL_BLOCK // GW) + gj] = val[:, gj * GW:(gj + 1) * GW].astype(ref.dtype)


def _rwkv_proj(x, ng, w_in, mu, w0, ww2, a0, wa2, kk, ka, rk, tri, seg, segt, bsz, seq):
    t, d = x.shape
    width = w0.shape[1]
    ngroups = width // GW
    tm = TOKEN_TILE
    ns = seq // tm
    const = lambda a: pl.BlockSpec(a.shape, lambda b, i: (0,) * a.ndim)
    consts = [ng, w_in, mu, w0, ww2, a0, wa2, kk, ka, rk, tri, seg, segt]
    gspec = pl.BlockSpec((ngroups, tm, GW), lambda b, i: (0, b * ns + i, 0))
    nspec = pl.BlockSpec((tm, width), lambda b, i: (b * ns + i, 0))
    wl_rows = tm // CHUNK * 8
    return pl.pallas_call(
        functools.partial(_rwkv_proj_kernel, width=width),
        grid=(bsz, ns),
        in_specs=[pl.BlockSpec((tm, d), lambda b, i: (b * ns + i, 0))] + [const(a) for a in consts],
        out_specs=[gspec] * 7 + [pl.BlockSpec((wl_rows, width), lambda b, i: (b * ns + i, 0)),
                                 nspec, nspec],
        out_shape=[jax.ShapeDtypeStruct((ngroups, t, GW), BF16)] * 7
        + [jax.ShapeDtypeStruct((t // CHUNK * 8, width), F32),
           jax.ShapeDtypeStruct((t, width), BF16), jax.ShapeDtypeStruct((t, width), BF16)],
        scratch_shapes=[pltpu.VMEM((8, w_in.shape[1]), F32)],
        compiler_params=_params("parallel", "arbitrary"),
        name="rwkv_proj",
    )(x, *consts)


def _block_diag(y, bd_mask):
    return jnp.where(bd_mask, jnp.concatenate([y.astype(BF16)] * GROUP_HEADS, axis=0), 0)


def _wkv_kernel(rt_ref, at_ref, bt_ref, kt_ref, bh_ref, kh_ref, v_ref, wl_ref, y_ref, state_ref, *,
                ngroups, nchunks):
    @pl.when(pl.program_id(1) == 0)
    def _():
        state_ref[...] = jnp.zeros_like(state_ref)

    t_idx = lax.broadcasted_iota(jnp.int32, (CHUNK, GW), 0)
    s_idx = lax.broadcasted_iota(jnp.int32, (CHUNK, GW), 1) % CHUNK
    incl = s_idx <= t_idx
    strict = s_idx < t_idx
    eye = (s_idx == t_idx).astype(F32)
    blk8 = strict & ((s_idx // 8) == (t_idx // 8))
    off_masks = [((s_idx // (2 * b)) == (t_idx // (2 * b))) & ((t_idx // b) % 2 == 1)
                 & ((s_idx // b) % 2 == 0) for b in (8, 16, 32)]
    bd_mask = (lax.broadcasted_iota(jnp.int32, (GW, GW), 0) // RWKV_HEAD
               == lax.broadcasted_iota(jnp.int32, (GW, GW), 1) // RWKV_HEAD)

    def bd(y):
        return _block_diag(y, bd_mask)

    def mm(x, y):
        return _dot(x, bd(y))

    gs = range(ngroups)
    chains = [(c, gi) for c in range(nchunks) for gi in gs]
    rows = [slice(c * CHUNK, (c + 1) * CHUNK) for c, _ in chains]
    r_t = [rt_ref[gi, rw, :] for (_, gi), rw in zip(chains, rows)]
    a_t = [at_ref[gi, rw, :] for (_, gi), rw in zip(chains, rows)]
    v = [v_ref[gi, rw, :] for (_, gi), rw in zip(chains, rows)]

    amat = [_dot_nt(jnp.concatenate([r_t[n], a_t[n]], axis=0),
                    jnp.concatenate([bd(bt_ref[gi, rows[n], :]), bd(kt_ref[gi, rows[n], :])], axis=0))
            for n, (_, gi) in enumerate(chains)]
    a_rb = [jnp.where(incl, x[:CHUNK, :GW], 0.0) for x in amat]
    a_rk = [jnp.where(incl, x[:CHUNK, GW:], 0.0) for x in amat]
    m_ab = [jnp.where(strict, x[CHUNK:, :GW], 0.0) for x in amat]
    a_ak = [jnp.where(strict, x[CHUNK:, GW:], 0.0) for x in amat]

    n1 = [jnp.where(blk8, x, 0.0) for x in m_ab]
    n2 = [mm(x, x) for x in n1]
    n4 = [mm(x, x) for x in n2]
    tmat = [eye + x for x in n1]
    tmat = [x + mm(x, y) for x, y in zip(tmat, n2)]
    tmat = [x + mm(x, y) for x, y in zip(tmat, n4)]
    for off in off_masks:
        z = [mm(jnp.where(off, m, 0.0), x) for m, x in zip(m_ab, tmat)]
        tmat = [x + mm(x, y) for x, y in zip(tmat, z)]

    akv = [mm(x, y) for x, y in zip(a_ak, v)]
    pq = [_dot(tmat[n], jnp.concatenate([bd(a_t[n]), bd(akv[n])], axis=1))
          for n in range(len(chains))]

    state = [state_ref[gi] for gi in gs]
    for c in range(nchunks):
        ns = [c * ngroups + gi for gi in gs]
        us = [_dot_nt(jnp.concatenate([pq[n][:, :GW].astype(BF16), r_t[n]], axis=0), state[gi])
              for gi, n in zip(gs, ns)]
        u = [us[gi][:CHUNK] + pq[n][:, GW:] for gi, n in zip(gs, ns)]
        y = [us[gi][CHUNK:]
             + _dot(jnp.concatenate([a_rb[n], a_rk[n]], axis=1),
                    jnp.concatenate([bd(u[gi]), bd(v[n])], axis=0)) for gi, n in zip(gs, ns)]
        upd = [_dot(jnp.concatenate([u[gi], v[n].astype(F32)], axis=0).T,
                    jnp.concatenate([bh_ref[gi, rows[n], :], kh_ref[gi, rows[n], :]], axis=0))
               for gi, n in zip(gs, ns)]
        wl = wl_ref[c * 8:c * 8 + 1, :]
        state = [state[gi] * wl[:, gi * GW:(gi + 1) * GW] + jnp.where(bd_mask, upd[gi], 0.0)
                 for gi in gs]
        for gi in gs:
            y_ref[rows[ns[gi]], gi * GW:(gi + 1) * GW] = y[gi].astype(y_ref.dtype)
    for gi in gs:
        state_ref[gi] = state[gi]


def _wkv(rt, at, bt, kt, bh, kh, v, wl, bsz, seq):
    ngroups, t, _ = rt.shape
    tt = WKV_TILE
    ns = seq // tt
    gspec = pl.BlockSpec((ngroups, tt, GW), lambda b, i: (0, b * ns + i, 0))
    return pl.pallas_call(
        functools.partial(_wkv_kernel, ngroups=ngroups, nchunks=tt // CHUNK),
        grid=(bsz, ns),
        in_specs=[gspec] * 7 + [pl.BlockSpec((tt // CHUNK * 8, ngroups * GW),
                                             lambda b, i: (b * ns + i, 0))],
        out_specs=pl.BlockSpec((tt, ngroups * GW), lambda b, i: (b * ns + i, 0)),
        out_shape=jax.ShapeDtypeStruct((t, ngroups * GW), BF16),
        scratch_shapes=[pltpu.VMEM((ngroups, GW, GW), F32)],
        compiler_params=_params("parallel", "arbitrary"),
        name="rwkv_wkv",
    )(rt, at, bt, kt, bh, kh, v, wl)


def _row(v):
    return v.reshape(1, -1).astype(F32)


def kernel(x, positions, norm_g, mla_w_in, mla_q_norm, mla_w_uq, mla_kv_norm, mla_w_ukv, mla_w_out, rwkv_w_in, rwkv_mu, rwkv_w0, rwkv_w_w2, rwkv_a0, rwkv_w_a2, rwkv_k_k, rwkv_k_a, rwkv_r_k, rwkv_ln_g, rwkv_ln_b, rwkv_w_out, final_g):
    bsz, seq, d = x.shape
    depth = norm_g.shape[0]
    t = bsz * seq
    xf = x.reshape(t, d)
    half = QK_ROPE // 2
    inv_freq = 1.0 / (ROPE_THETA ** (jnp.arange(half, dtype=F32) / half))
    ang = positions.reshape(t, 1).astype(F32) * inv_freq[None, :]
    cos = jnp.tile(jnp.cos(ang), (1, LANES // half))
    sin = jnp.tile(jnp.sin(ang), (1, LANES // half))
    final_row = _row(final_g)

    width = rwkv_w0.shape[1]
    tok = jnp.arange(TOKEN_TILE)
    tri = ((tok[:, None] // CHUNK == tok[None, :] // CHUNK) & (tok[None, :] <= tok[:, None])).astype(BF16)
    seg = (jnp.arange(width)[:, None] // RWKV_HEAD == jnp.arange(LANES)[None, :]).astype(BF16)
    segt = seg.T

    for i in range(depth):
        j = i // 2
        ng = _row(norm_g[i])
        last = i == depth - 1
        if i % 2 == 0:
            w_in = mla_w_in[j]
            lat = Q_LORA + KV_LORA
            w_in = jnp.concatenate(
                [w_in[:, :lat], w_in[:, lat + QK_ROPE:], w_in[:, lat:lat + QK_ROPE],
                 jnp.zeros((d, LANES - QK_ROPE), w_in.dtype)], axis=1).astype(BF16)
            w_uq = mla_w_uq[j].reshape(Q_LORA, MLA_HEADS, QK_DIM)
            w_uq = jnp.concatenate([w_uq[:, :, :QK_NOPE].reshape(Q_LORA, -1),
                                    w_uq[:, :, QK_NOPE:].reshape(Q_LORA, -1)], axis=1).astype(BF16)
            w_ukv = mla_w_ukv[j].reshape(KV_LORA, MLA_HEADS, QK_NOPE + V_DIM)
            w_ukv = jnp.concatenate([w_ukv[:, :, :QK_NOPE].reshape(KV_LORA, -1),
                                     w_ukv[:, :, QK_NOPE:].reshape(KV_LORA, -1)], axis=1).astype(BF16)
            q, k, vt, gate = _mla_proj(xf, cos, sin, ng, w_in, _row(mla_q_norm[j]), w_uq,
                                       _row(mla_kv_norm[j]), w_ukv)
            y = _attention(q, k, vt, gate, bsz, seq)
            xf = _out_proj(y, mla_w_out[j].astype(BF16), xf, final_row, last)
        else:
            zeros = jnp.zeros((LORA, width), F32)
            ww2 = jnp.concatenate([rwkv_w_w2[j], zeros], axis=0).astype(BF16)
            wa2 = jnp.concatenate([zeros, rwkv_w_a2[j]], axis=0).astype(BF16)
            *ops, wl, bv, sg = _rwkv_proj(
                xf, ng, rwkv_w_in[j].astype(BF16), _row(rwkv_mu[j]), _row(rwkv_w0[j]), ww2,
                _row(rwkv_a0[j]), wa2, _row(rwkv_k_k[j]), _row(rwkv_k_a[j]), _row(rwkv_r_k[j]),
                tri, seg, segt, bsz, seq)
            y = _wkv(*ops, wl, bsz, seq)
            gn = (bv, sg, _row(rwkv_ln_g[j]), _row(rwkv_ln_b[j]), seg, segt)
            xf = _out_proj(y, rwkv_w_out[j].astype(BF16), xf, final_row, last, gn)
    return xf.reshape(bsz, seq, d)
```

```python
import functools

import jax
import jax.numpy as jnp
from jax import lax
from jax.experimental import pallas as pl
from jax.experimental.pallas import tpu as pltpu

F32 = jnp.float32
BF16 = jnp.bfloat16

NORM_EPS = 1e-6
GN_EPS = 64e-5
ROPE_THETA = 10000.0

MLA_HEADS = 8
QK_NOPE = 128
QK_ROPE = 64
QK_DIM = QK_NOPE + QK_ROPE
V_DIM = 128
Q_LORA = 768
KV_LORA = 256

RWKV_HEAD = 64
LORA = 64

LANES = 128
CHUNK = 64
GROUP_HEADS = 2
GW = GROUP_HEADS * RWKV_HEAD

TOKEN_TILE = 512
GN_TILE = 256
TRI_ROWS = 256
MLA_TILE = 512
OUT_TILE = 512
ATTN_TILE = 512
COL_BLOCK = 256
WKV_TILE = 256
VMEM_LIMIT = 56 * 1024 * 1024
MASK_VALUE = -1e30
ONES_ROWS = 16
LOG2E = 1.4426950408889634
EXP_M05 = 0.6065306597126334


def _params(*sem):
    return pltpu.CompilerParams(dimension_semantics=sem, vmem_limit_bytes=VMEM_LIMIT)


def _rms(x, g):
    return x * lax.rsqrt(jnp.mean(x * x, axis=-1, keepdims=True) + NORM_EPS) * g


def _dot(a, b):
    return jnp.dot(a.astype(BF16), b.astype(BF16), preferred_element_type=F32)


def _dot_nt(a, b):
    return lax.dot_general(a.astype(BF16), b.astype(BF16), (((1,), (1,)), ((), ())),
                           preferred_element_type=F32)


def _silu(x):
    return x / (1.0 + jnp.exp(-x))


def _rope(t, cos, sin_signed, first):
    partner = jnp.where(first, pltpu.roll(t, LANES - QK_ROPE // 2, 1), pltpu.roll(t, QK_ROPE // 2, 1))
    return t * cos + partner * sin_signed


def _mla_proj_kernel(x_ref, cos_ref, sin_ref, ng_ref, win_ref, qn_ref, wuq_ref, kvn_ref, wukv_ref,
                     q_ref, k_ref, v_ref, gate_ref):
    x = x_ref[...]
    h = _rms(x, ng_ref[...])
    proj = _dot(h, win_ref[...])
    q_lat = proj[:, :Q_LORA]
    kv_lat = proj[:, Q_LORA:Q_LORA + KV_LORA]
    gate_ref[...] = proj[:, 1024:2048].astype(gate_ref.dtype)
    k_rope = proj[:, 2048:2048 + LANES]

    q = _dot(_rms(q_lat, qn_ref[...]), wuq_ref[...]) * (QK_DIM ** -0.5 * LOG2E)
    kv = _dot(_rms(kv_lat, kvn_ref[...]), wukv_ref[...])

    cos = cos_ref[...]
    sin = sin_ref[...]
    lane = lax.broadcasted_iota(jnp.int32, cos.shape, 1)
    first = (lane % QK_ROPE) < (QK_ROPE // 2)
    sin_signed = jnp.where(first, -sin, sin)

    k_rope = _rope(k_rope, cos, sin_signed, first)[:, :QK_ROPE].astype(k_ref.dtype)
    nope_w = MLA_HEADS * QK_NOPE
    for hp in range(MLA_HEADS // 2):
        q_rope = _rope(q[:, nope_w + hp * LANES: nope_w + (hp + 1) * LANES], cos, sin_signed, first)
        for hh in range(2):
            hd = 2 * hp + hh
            q_ref[hd, :, :QK_NOPE] = q[:, hd * QK_NOPE:(hd + 1) * QK_NOPE].astype(q_ref.dtype)
            q_ref[hd, :, QK_NOPE:] = q_rope[:, hh * QK_ROPE:(hh + 1) * QK_ROPE].astype(q_ref.dtype)
    for hd in range(MLA_HEADS):
        k_ref[hd, :, :QK_NOPE] = kv[:, hd * QK_NOPE:(hd + 1) * QK_NOPE].astype(k_ref.dtype)
        k_ref[hd, :, QK_NOPE:] = k_rope
        v_ref[hd, 0] = kv[:, nope_w + hd * V_DIM: nope_w + (hd + 1) * V_DIM].T.astype(v_ref.dtype)


def _mla_proj(x, cos, sin, ng, w_in, qn, w_uq, kvn, w_ukv):
    t, d = x.shape
    tm = MLA_TILE
    const = lambda shape: pl.BlockSpec(shape, lambda i: (0,) * len(shape))
    return pl.pallas_call(
        _mla_proj_kernel,
        grid=(t // tm,),
        in_specs=[pl.BlockSpec((tm, d), lambda i: (i, 0)),
                  pl.BlockSpec((tm, LANES), lambda i: (i, 0)),
                  pl.BlockSpec((tm, LANES), lambda i: (i, 0)),
                  const(ng.shape), const(w_in.shape), const(qn.shape), const(w_uq.shape),
                  const(kvn.shape), const(w_ukv.shape)],
        out_specs=[pl.BlockSpec((MLA_HEADS, tm, QK_DIM), lambda i: (0, i, 0)),
                   pl.BlockSpec((MLA_HEADS, tm, QK_DIM), lambda i: (0, i, 0)),
                   pl.BlockSpec((MLA_HEADS, 1, V_DIM, tm), lambda i: (0, i, 0, 0)),
                   pl.BlockSpec((tm, MLA_HEADS * V_DIM), lambda i: (i, 0))],
        out_shape=[jax.ShapeDtypeStruct((MLA_HEADS, t, QK_DIM), BF16),
                   jax.ShapeDtypeStruct((MLA_HEADS, t, QK_DIM), BF16),
                   jax.ShapeDtypeStruct((MLA_HEADS, t // tm, V_DIM, tm), BF16),
                   jax.ShapeDtypeStruct((t, MLA_HEADS * V_DIM), BF16)],
        compiler_params=_params("parallel"),
        name="mla_proj",
    )(x, cos, sin, ng, w_in, qn, w_uq, kvn, w_ukv)


def _attn_kernel(q_ref, k_ref, vt_ref, gate_ref, o_ref, st0_ref, st1_ref, acc_ref, *, blk, sub, nq):
    nsub = blk // sub

    def query_block(qi, carry):
        rows = pl.ds(pl.multiple_of(qi * blk, blk), blk)
        q = q_ref[0, rows, :]

        def scores(kb, st_ref):
            start = pl.multiple_of(kb * blk, blk)
            st = _dot_nt(k_ref[0, pl.ds(start, blk), :], q)
            st_ref[...] = st
            return jnp.max(st, axis=0, keepdims=True)

        def update(kb, st_ref, st_max, m, masked):
            vt = jnp.concatenate([vt_ref[0, kb * nsub + j] for j in range(nsub)], axis=1)
            vt = jnp.concatenate([vt, jnp.ones((ONES_ROWS, blk), vt.dtype)], axis=0)
            st = st_ref[...]
            if masked:
                key = lax.broadcasted_iota(jnp.int32, st.shape, 0)
                qry = lax.broadcasted_iota(jnp.int32, st.shape, 1)
                st = jnp.where(key <= qry, st, MASK_VALUE)
                st_max = jnp.max(st, axis=0, keepdims=True)
            m_new = jnp.maximum(m, st_max)
            pt = jnp.exp2((st - m_new).astype(BF16))
            corr = jnp.exp2(m - m_new)
            acc_ref[...] = acc_ref[...] * corr + _dot(vt, pt)
            return m_new

        acc_ref[...] = jnp.zeros_like(acc_ref)
        m_init = jnp.full((1, blk), MASK_VALUE, F32)

        def body(j, c):
            mx, m = c
            mx1 = scores(2 * j + 1, st1_ref)
            m = update(2 * j, st0_ref, mx, m, False)
            mx2 = scores(2 * j + 2, st0_ref)
            return mx2, update(2 * j + 1, st1_ref, mx1, m, False)

        npairs = qi // 2
        c = lax.fori_loop(0, npairs // 2, lambda j, c: body(2 * j + 1, body(2 * j, c)),
                          (scores(0, st0_ref), m_init))
        c = lax.cond(npairs % 2 == 1, lambda c: body(npairs - 1, c), lambda c: c, c)

        def tail_even(c):
            update(qi, st0_ref, c[0], c[1], True)

        def tail_odd(c):
            mx1 = scores(qi, st1_ref)
            m = update(qi - 1, st0_ref, c[0], c[1], False)
            update(qi, st1_ref, mx1, m, True)

        lax.cond(qi % 2 == 0, tail_even, tail_odd, c)
        gate = gate_ref[rows, :].astype(F32)
        out = acc_ref[:V_DIM, :] / acc_ref[V_DIM:V_DIM + 1, :]
        o_ref[rows, :] = (out.T * _silu(gate)).astype(o_ref.dtype)
        return carry

    lax.fori_loop(0, nq, query_block, 0)


def _attention(q, k, vt, gate, bsz, seq):
    blk = ATTN_TILE
    sub = vt.shape[3]
    nv = seq // sub
    return pl.pallas_call(
        functools.partial(_attn_kernel, blk=blk, sub=sub, nq=seq // blk),
        grid=(bsz, MLA_HEADS),
        in_specs=[pl.BlockSpec((1, seq, QK_DIM), lambda b, h: (h, b, 0)),
                  pl.BlockSpec((1, seq, QK_DIM), lambda b, h: (h, b, 0)),
                  pl.BlockSpec((1, nv, V_DIM, sub), lambda b, h: (h, b, 0, 0)),
                  pl.BlockSpec((seq, V_DIM), lambda b, h: (b, h))],
        out_specs=pl.BlockSpec((seq, V_DIM), lambda b, h: (b, h)),
        out_shape=jax.ShapeDtypeStruct((bsz * seq, MLA_HEADS * V_DIM), BF16),
        scratch_shapes=[pltpu.VMEM((blk, blk), F32), pltpu.VMEM((blk, blk), F32),
                        pltpu.VMEM((V_DIM + ONES_ROWS, blk), F32)],
        compiler_params=_params("parallel", "parallel"),
        name="mla_attention",
    )(q, k, vt, gate)


def _seg_sum(x, seg_ref, segt_ref):
    s = jnp.dot(x.astype(BF16), seg_ref[...], preferred_element_type=F32)
    hi = s.astype(BF16)
    lo = (s - hi.astype(F32)).astype(BF16)
    return (jnp.dot(hi, segt_ref[...], preferred_element_type=F32)
            + jnp.dot(lo, segt_ref[...], preferred_element_type=F32))


def _finish(y, w_ref, x_ref, g_ref, o_ref, final_norm):
    x = x_ref[...] + jnp.dot(y.astype(BF16), w_ref[...], preferred_element_type=F32)
    if final_norm:
        x = _rms(x, g_ref[...])
    o_ref[...] = x


def _out_proj_kernel(y_ref, w_ref, x_ref, g_ref, o_ref, *, final_norm):
    _finish(y_ref[...], w_ref, x_ref, g_ref, o_ref, final_norm)


def _rwkv_out_kernel(y_ref, bv_ref, sg_ref, lng_ref, lnb_ref, seg_ref, segt_ref, w_ref, x_ref, g_ref,
                     o_ref, *, final_norm):
    y = y_ref[...].astype(F32)
    mean = _seg_sum(y, seg_ref, segt_ref) * (1.0 / RWKV_HEAD)
    yc = y - mean
    var = _seg_sum(yc * yc, seg_ref, segt_ref) * (1.0 / RWKV_HEAD)
    yn = yc * lax.rsqrt(var + GN_EPS) * lng_ref[...] + lnb_ref[...]
    out = (yn + bv_ref[...].astype(F32)) * sg_ref[...].astype(F32)
    _finish(out, w_ref, x_ref, g_ref, o_ref, final_norm)


def _out_proj(y, w, x, final_g, final_norm, gn=None):
    t, d = x.shape
    tm = OUT_TILE if gn is None else GN_TILE
    tile = lambda a: pl.BlockSpec((tm, a.shape[1]), lambda i: (i, 0))
    const = lambda a: pl.BlockSpec(a.shape, lambda i: (0, 0))
    if gn is None:
        body, ops = _out_proj_kernel, [y]
        specs = [tile(y)]
    else:
        bv, sg, lng, lnb, seg, segt = gn
        body, ops = _rwkv_out_kernel, [y, bv, sg, lng, lnb, seg, segt]
        specs = [tile(y), tile(bv), tile(sg), const(lng), const(lnb), const(seg), const(segt)]
    return pl.pallas_call(
        functools.partial(body, final_norm=final_norm),
        grid=(t // tm,),
        in_specs=specs + [const(w), tile(x), const(final_g)],
        out_specs=pl.BlockSpec((tm, d), lambda i: (i, 0)),
        out_shape=jax.ShapeDtypeStruct((t, d), F32),
        compiler_params=_params("parallel"),
        name="out_proj" if gn is None else "rwkv_out",
    )(*ops, w, x, final_g)


def _split3(x):
    hi = x.astype(BF16)
    r1 = x - hi.astype(F32)
    mid = r1.astype(BF16)
    lo = (r1 - mid.astype(F32)).astype(BF16)
    return hi, mid, lo


def _rwkv_proj_kernel(x_ref, ng_ref, win_ref, mu_ref, w0_ref, ww2_ref, a0_ref, wa2_ref,
                      kk_ref, ka_ref, rk_ref, tri_ref, seg_ref, segt_ref,
                      rt_ref, at_ref, bt_ref, kt_ref, bh_ref, kh_ref, v_ref, wl_ref, bv_ref, sg_ref,
                      prev_ref, *, width):
    @pl.when(pl.program_id(1) == 0)
    def _():
        prev_ref[...] = jnp.zeros_like(prev_ref)

    h = _rms(x_ref[...], ng_ref[...]).astype(BF16)
    rows = h.shape[0]
    nchunks = rows // CHUNK
    row8 = lax.broadcasted_iota(jnp.int32, (8, 1), 0)
    tri = tri_ref[...]

    def project(cols):
        proj = jnp.dot(h, win_ref[:, cols], preferred_element_type=F32)
        rolled = pltpu.roll(proj, 1, 0)
        shifted = jnp.concatenate([jnp.where(row8 == 0, prev_ref[0:1, cols], rolled[:8]), rolled[8:]],
                                  axis=0)
        prev_ref[0:1, cols] = proj[rows - 1:rows, :]
        return proj + mu_ref[:, cols] * (shifted - proj)

    lora = project(slice(4 * width, 4 * width + 2 * LORA))
    lane = lax.broadcasted_iota(jnp.int32, lora.shape, 1)
    lora = jnp.where(lane < LORA, jnp.tanh(lora), lora).astype(BF16)

    for cb in range(width // COL_BLOCK):
        cs = slice(cb * COL_BLOCK, (cb + 1) * COL_BLOCK)
        r, k, v, g = (project(slice(part * width + cs.start, part * width + cs.stop))
                      for part in range(4))
        sg_ref[:, cs] = _silu(g).astype(sg_ref.dtype)

        def seg_sum(x):
            s = jnp.dot(x.astype(BF16), seg_ref[cs, :], preferred_element_type=F32)
            hi = s.astype(BF16)
            lo = (s - hi.astype(F32)).astype(BF16)
            return (jnp.dot(hi, segt_ref[:, cs], preferred_element_type=F32)
                    + jnp.dot(lo, segt_ref[:, cs], preferred_element_type=F32))

        z = w0_ref[:, cs] + jnp.dot(lora, ww2_ref[:, cs], preferred_element_type=F32)
        log_decay = (-LOG2E * EXP_M05) / (1.0 + jnp.exp(-z))
        a = 1.0 / (1.0 + jnp.exp(-(a0_ref[:, cs]
                                    + jnp.dot(lora, wa2_ref[:, cs], preferred_element_type=F32))))

        kk = k * kk_ref[:, cs]
        kk = kk * lax.rsqrt(jnp.maximum(seg_sum(kk * kk), 1e-24))
        kp = k * (1.0 + (a - 1.0) * ka_ref[:, cs])
        b = kk * a
        bv_ref[:, cs] = (seg_sum(r * kp * rk_ref[:, cs]) * v).astype(bv_ref.dtype)

        parts = _split3(log_decay)
        cum = jnp.concatenate(
            [sum(jnp.dot(tri, part[sb:sb + TRI_ROWS], preferred_element_type=F32) for part in parts)
             for sb in range(0, rows, TRI_ROWS)], axis=0)
        last = [cum[(c + 1) * CHUNK - 1:(c + 1) * CHUNK, :] for c in range(nchunks)]
        cum_last = jnp.concatenate([jnp.broadcast_to(x, (CHUNK, COL_BLOCK)) for x in last], axis=0)
        wl_ref[:, cs] = jnp.concatenate([jnp.broadcast_to(jnp.exp2(x), (8, COL_BLOCK)) for x in last],
                                        axis=0)
        w_inv = jnp.exp2(-cum)
        w_tail = jnp.exp2(cum_last - cum)
        outs = ((rt_ref, r * jnp.exp2(cum)), (at_ref, -kk * jnp.exp2(cum - log_decay)),
                (bt_ref, b * w_inv), (kt_ref, kp * w_inv), (bh_ref, b * w_tail),
                (kh_ref, kp * w_tail), (v_ref, v))
        for ref, val in outs:
            for gj in range(COL_BLOCK // GW):
                ref[cb * (COL_BLOCK // GW) + gj] = val[:, gj * GW:(gj + 1) * GW].astype(ref.dtype)


def _rwkv_proj(x, ng, w_in, mu, w0, ww2, a0, wa2, kk, ka, rk, tri, seg, segt, bsz, seq):
    t, d = x.shape
    width = w0.shape[1]
    ngroups = width // GW
    tm = TOKEN_TILE
    ns = seq // tm
    const = lambda a: pl.BlockSpec(a.shape, lambda b, i: (0,) * a.ndim)
    consts = [ng, w_in, mu, w0, ww2, a0, wa2, kk, ka, rk, tri, seg, segt]
    gspec = pl.BlockSpec((ngroups, tm, GW), lambda b, i: (0, b * ns + i, 0))
    nspec = pl.BlockSpec((tm, width), lambda b, i: (b * ns + i, 0))
    wl_rows = tm // CHUNK * 8
    return pl.pallas_call(
        functools.partial(_rwkv_proj_kernel, width=width),
        grid=(bsz, ns),
        in_specs=[pl.BlockSpec((tm, d), lambda b, i: (b * ns + i, 0))] + [const(a) for a in consts],
        out_specs=[gspec] * 7 + [pl.BlockSpec((wl_rows, width), lambda b, i: (b * ns + i, 0)),
                                 nspec, nspec],
        out_shape=[jax.ShapeDtypeStruct((ngroups, t, GW), BF16)] * 7
        + [jax.ShapeDtypeStruct((t // CHUNK * 8, width), F32),
           jax.ShapeDtypeStruct((t, width), BF16), jax.ShapeDtypeStruct((t, width), BF16)],
        scratch_shapes=[pltpu.VMEM((8, w_in.shape[1]), F32)],
        compiler_params=_params("parallel", "arbitrary"),
        name="rwkv_proj",
    )(x, *consts)


def _block_diag(y, bd_mask):
    return jnp.where(bd_mask, jnp.concatenate([y.astype(BF16)] * GROUP_HEADS, axis=0), 0)


def _wkv_kernel(rt_ref, at_ref, bt_ref, kt_ref, bh_ref, kh_ref, v_ref, wl_ref, y_ref, state_ref, *,
                ngroups, nchunks):
    @pl.when(pl.program_id(1) == 0)
    def _():
        state_ref[...] = jnp.zeros_like(state_ref)

    t_idx = lax.broadcasted_iota(jnp.int32, (CHUNK, GW), 0)
    s_idx = lax.broadcasted_iota(jnp.int32, (CHUNK, GW), 1) % CHUNK
    incl = s_idx <= t_idx
    strict = s_idx < t_idx
    eye = (s_idx == t_idx).astype(F32)
    blk8 = strict & ((s_idx // 8) == (t_idx // 8))
    off_masks = [((s_idx // (2 * b)) == (t_idx // (2 * b))) & ((t_idx // b) % 2 == 1)
                 & ((s_idx // b) % 2 == 0) for b in (8, 16, 32)]
    bd_mask = (lax.broadcasted_iota(jnp.int32, (GW, GW), 0) // RWKV_HEAD
               == lax.broadcasted_iota(jnp.int32, (GW, GW), 1) // RWKV_HEAD)

    def bd(y):
        return _block_diag(y, bd_mask)

    def mm(x, y):
        return _dot(x, bd(y))

    gs = range(ngroups)
    chains = [(c, gi) for c in range(nchunks) for gi in gs]
    rows = [slice(c * CHUNK, (c + 1) * CHUNK) for c, _ in chains]
    r_t = [rt_ref[gi, rw, :] for (_, gi), rw in zip(chains, rows)]
    a_t = [at_ref[gi, rw, :] for (_, gi), rw in zip(chains, rows)]
    v = [v_ref[gi, rw, :] for (_, gi), rw in zip(chains, rows)]

    amat = [_dot_nt(jnp.concatenate([r_t[n], a_t[n]], axis=0),
                    jnp.concatenate([bd(bt_ref[gi, rows[n], :]), bd(kt_ref[gi, rows[n], :])], axis=0))
            for n, (_, gi) in enumerate(chains)]
    a_rb = [jnp.where(incl, x[:CHUNK, :GW], 0.0) for x in amat]
    a_rk = [jnp.where(incl, x[:CHUNK, GW:], 0.0) for x in amat]
    m_ab = [jnp.where(strict, x[CHUNK:, :GW], 0.0) for x in amat]
    a_ak = [jnp.where(strict, x[CHUNK:, GW:], 0.0) for x in amat]

    n1 = [jnp.where(blk8, x, 0.0) for x in m_ab]
    n2 = [mm(x, x) for x in n1]
    n4 = [mm(x, x) for x in n2]
    tmat = [eye + x for x in n1]
    tmat = [x + mm(x, y) for x, y in zip(tmat, n2)]
    tmat = [x + mm(x, y) for x, y in zip(tmat, n4)]
    for off in off_masks:
        z = [mm(jnp.where(off, m, 0.0), x) for m, x in zip(m_ab, tmat)]
        tmat = [x + mm(x, y) for x, y in zip(tmat, z)]

    akv = [mm(x, y) for x, y in zip(a_ak, v)]
    pq = [_dot(tmat[n], jnp.concatenate([bd(a_t[n]), bd(akv[n])], axis=1))
          for n in range(len(chains))]

    state = [state_ref[gi] for gi in gs]
    for c in range(nchunks):
        ns = [c * ngroups + gi for gi in gs]
        us = [_dot_nt(jnp.concatenate([pq[n][:, :GW].astype(BF16), r_t[n]], axis=0), state[gi])
              for gi, n in zip(gs, ns)]
        u = [us[gi][:CHUNK] + pq[n][:, GW:] for gi, n in zip(gs, ns)]
        y = [us[gi][CHUNK:]
             + _dot(jnp.concatenate([a_rb[n], a_rk[n]], axis=1),
                    jnp.concatenate([bd(u[gi]), bd(v[n])], axis=0)) for gi, n in zip(gs, ns)]
        upd = [_dot(jnp.concatenate([u[gi], v[n].astype(F32)], axis=0).T,
                    jnp.concatenate([bh_ref[gi, rows[n], :], kh_ref[gi, rows[n], :]], axis=0))
               for gi, n in zip(gs, ns)]
        wl = wl_ref[c * 8:c * 8 + 1, :]
        state = [state[gi] * wl[:, gi * GW:(gi + 1) * GW] + jnp.where(bd_mask, upd[gi], 0.0)
                 for gi in gs]
        for gi in gs:
            y_ref[rows[ns[gi]], gi * GW:(gi + 1) * GW] = y[gi].astype(y_ref.dtype)
    for gi in gs:
        state_ref[gi] = state[gi]


def _wkv(rt, at, bt, kt, bh, kh, v, wl, bsz, seq):
    ngroups, t, _ = rt.shape
    tt = WKV_TILE
    ns = seq // tt
    gspec = pl.BlockSpec((ngroups, tt, GW), lambda b, i: (0, b * ns + i, 0))
    return pl.pallas_call(
        functools.partial(_wkv_kernel, ngroups=ngroups, nchunks=tt // CHUNK),
        grid=(bsz, ns),
        in_specs=[gspec] * 7 + [pl.BlockSpec((tt // CHUNK * 8, ngroups * GW),
                                             lambda b, i: (b * ns + i, 0))],
        out_specs=pl.BlockSpec((tt, ngroups * GW), lambda b, i: (b * ns + i, 0)),
        out_shape=jax.ShapeDtypeStruct((t, ngroups * GW), BF16),
        scratch_shapes=[pltpu.VMEM((ngroups, GW, GW), F32)],
        compiler_params=_params("parallel", "arbitrary"),
        name="rwkv_wkv",
    )(rt, at, bt, kt, bh, kh, v, wl)


def _row(v):
    return v.reshape(1, -1).astype(F32)


def kernel(x, positions, norm_g, mla_w_in, mla_q_norm, mla_w_uq, mla_kv_norm, mla_w_ukv, mla_w_out, rwkv_w_in, rwkv_mu, rwkv_w0, rwkv_w_w2, rwkv_a0, rwkv_w_a2, rwkv_k_k, rwkv_k_a, rwkv_r_k, rwkv_ln_g, rwkv_ln_b, rwkv_w_out, final_g):
    bsz, seq, d = x.shape
    depth = norm_g.shape[0]
    t = bsz * seq
    xf = x.reshape(t, d)
    half = QK_ROPE // 2
    inv_freq = 1.0 / (ROPE_THETA ** (jnp.arange(half, dtype=F32) / half))
    ang = positions.reshape(t, 1).astype(F32) * inv_freq[None, :]
    cos = jnp.tile(jnp.cos(ang), (1, LANES // half))
    sin = jnp.tile(jnp.sin(ang), (1, LANES // half))
    final_row = _row(final_g)

    width = rwkv_w0.shape[1]
    tok = jnp.arange(TRI_ROWS)
    tri = ((tok[:, None] // CHUNK == tok[None, :] // CHUNK) & (tok[None, :] <= tok[:, None])).astype(BF16)
    seg = (jnp.arange(width)[:, None] // RWKV_HEAD == jnp.arange(LANES)[None, :]).astype(BF16)
    segt = seg.T

    for i in range(depth):
        j = i // 2
        ng = _row(norm_g[i])
        last = i == depth - 1
        if i % 2 == 0:
            w_in = mla_w_in[j]
            lat = Q_LORA + KV_LORA
            w_in = jnp.concatenate(
                [w_in[:, :lat], w_in[:, lat + QK_ROPE:], w_in[:, lat:lat + QK_ROPE],
                 jnp.zeros((d, LANES - QK_ROPE), w_in.dtype)], axis=1).astype(BF16)
            w_uq = mla_w_uq[j].reshape(Q_LORA, MLA_HEADS, QK_DIM)
            w_uq = jnp.concatenate([w_uq[:, :, :QK_NOPE].reshape(Q_LORA, -1),
                                    w_uq[:, :, QK_NOPE:].reshape(Q_LORA, -1)], axis=1).astype(BF16)
            w_ukv = mla_w_ukv[j].reshape(KV_LORA, MLA_HEADS, QK_NOPE + V_DIM)
            w_ukv = jnp.concatenate([w_ukv[:, :, :QK_NOPE].reshape(KV_LORA, -1),
                                     w_ukv[:, :, QK_NOPE:].reshape(KV_LORA, -1)], axis=1).astype(BF16)
            q, k, vt, gate = _mla_proj(xf, cos, sin, ng, w_in, _row(mla_q_norm[j]), w_uq,
                                       _row(mla_kv_norm[j]), w_ukv)
            y = _attention(q, k, vt, gate, bsz, seq)
            xf = _out_proj(y, mla_w_out[j].astype(BF16), xf, final_row, last)
        else:
            zeros = jnp.zeros((LORA, width), F32)
            ww2 = jnp.concatenate([rwkv_w_w2[j], zeros], axis=0).astype(BF16)
            wa2 = jnp.concatenate([zeros, rwkv_w_a2[j]], axis=0).astype(BF16)
            *ops, wl, bv, sg = _rwkv_proj(
                xf, ng, rwkv_w_in[j].astype(BF16), _row(rwkv_mu[j]), _row(rwkv_w0[j]), ww2,
                _row(rwkv_a0[j]), wa2, _row(rwkv_k_k[j]), _row(rwkv_k_a[j]), _row(rwkv_r_k[j]),
                tri, seg, segt, bsz, seq)
            y = _wkv(*ops, wl, bsz, seq)
            gn = (bv, sg, _row(rwkv_ln_g[j]), _row(rwkv_ln_b[j]), seg, segt)
            xf = _out_proj(y, rwkv_w_out[j].astype(BF16), xf, final_row, last, gn)
    return xf.reshape(bsz, seq, d)
```

```python
import functools

import jax
import jax.numpy as jnp
from jax import lax
from jax.experimental import pallas as pl
from jax.experimental.pallas import tpu as pltpu

F32 = jnp.float32
BF16 = jnp.bfloat16

NORM_EPS = 1e-6
GN_EPS = 64e-5
KK_EPS = 1e-12
ROPE_THETA = 10000.0

MLA_HEADS = 8
QK_NOPE = 128
QK_ROPE = 64
QK_DIM = QK_NOPE + QK_ROPE
V_DIM = 128
Q_LORA = 768
KV_LORA = 256

RWKV_HEAD = 64
LORA = 64

LANES = 128
CHUNK = 64
GROUP_HEADS = 2
GW = GROUP_HEADS * RWKV_HEAD

TOKEN_TILE = 512
GN_TILE = 256
TRI_ROWS = 256
MLA_TILE = 512
OUT_TILE = 512
ATTN_TILE = 512
COL_BLOCK = 256
WKV_TILE = 256
VMEM_LIMIT = 56 * 1024 * 1024
MASK_VALUE = -1e30
ONES_ROWS = 16
LOG2E = 1.4426950408889634
EXP_M05 = 0.6065306597126334


def _params(*sem):
    return pltpu.CompilerParams(dimension_semantics=sem, vmem_limit_bytes=VMEM_LIMIT)


def _rms(x, g):
    return x * lax.rsqrt(jnp.mean(x * x, axis=-1, keepdims=True) + NORM_EPS) * g


def _dot(a, b):
    return jnp.dot(a.astype(BF16), b.astype(BF16), preferred_element_type=F32)


def _dot_nt(a, b):
    return lax.dot_general(a.astype(BF16), b.astype(BF16), (((1,), (1,)), ((), ())),
                           preferred_element_type=F32)


def _silu(x):
    return x / (1.0 + jnp.exp(-x))


def _rope(t, cos, sin_signed, first):
    partner = jnp.where(first, pltpu.roll(t, LANES - QK_ROPE // 2, 1), pltpu.roll(t, QK_ROPE // 2, 1))
    return t * cos + partner * sin_signed


def _mla_proj_kernel(x_ref, cos_ref, sin_ref, ng_ref, win_ref, qn_ref, wuq_ref, kvn_ref, wukv_ref,
                     q_ref, k_ref, v_ref, gate_ref):
    x = x_ref[...]
    h = _rms(x, ng_ref[...])
    proj = _dot(h, win_ref[...])
    q_lat = proj[:, :Q_LORA]
    kv_lat = proj[:, Q_LORA:Q_LORA + KV_LORA]
    lat = Q_LORA + KV_LORA
    gate_w = MLA_HEADS * V_DIM
    gate_ref[...] = proj[:, lat:lat + gate_w].astype(gate_ref.dtype)
    k_rope = proj[:, lat + gate_w:lat + gate_w + LANES]

    q = _dot(_rms(q_lat, qn_ref[...]), wuq_ref[...]) * (QK_DIM ** -0.5 * LOG2E)
    kv = _dot(_rms(kv_lat, kvn_ref[...]), wukv_ref[...])

    cos = cos_ref[...]
    sin = sin_ref[...]
    lane = lax.broadcasted_iota(jnp.int32, cos.shape, 1)
    first = (lane % QK_ROPE) < (QK_ROPE // 2)
    sin_signed = jnp.where(first, -sin, sin)

    k_rope = _rope(k_rope, cos, sin_signed, first)[:, :QK_ROPE].astype(k_ref.dtype)
    nope_w = MLA_HEADS * QK_NOPE
    for hp in range(MLA_HEADS // 2):
        q_rope = _rope(q[:, nope_w + hp * LANES: nope_w + (hp + 1) * LANES], cos, sin_signed, first)
        for hh in range(2):
            hd = 2 * hp + hh
            q_ref[hd, :, :QK_NOPE] = q[:, hd * QK_NOPE:(hd + 1) * QK_NOPE].astype(q_ref.dtype)
            q_ref[hd, :, QK_NOPE:] = q_rope[:, hh * QK_ROPE:(hh + 1) * QK_ROPE].astype(q_ref.dtype)
    for hd in range(MLA_HEADS):
        k_ref[hd, :, :QK_NOPE] = kv[:, hd * QK_NOPE:(hd + 1) * QK_NOPE].astype(k_ref.dtype)
        k_ref[hd, :, QK_NOPE:] = k_rope
        v_ref[hd, 0] = kv[:, nope_w + hd * V_DIM: nope_w + (hd + 1) * V_DIM].T.astype(v_ref.dtype)


def _mla_proj(x, cos, sin, ng, w_in, qn, w_uq, kvn, w_ukv):
    t, d = x.shape
    tm = MLA_TILE
    const = lambda shape: pl.BlockSpec(shape, lambda i: (0,) * len(shape))
    return pl.pallas_call(
        _mla_proj_kernel,
        grid=(t // tm,),
        in_specs=[pl.BlockSpec((tm, d), lambda i: (i, 0)),
                  pl.BlockSpec((tm, LANES), lambda i: (i, 0)),
                  pl.BlockSpec((tm, LANES), lambda i: (i, 0)),
                  const(ng.shape), const(w_in.shape), const(qn.shape), const(w_uq.shape),
                  const(kvn.shape), const(w_ukv.shape)],
        out_specs=[pl.BlockSpec((MLA_HEADS, tm, QK_DIM), lambda i: (0, i, 0)),
                   pl.BlockSpec((MLA_HEADS, tm, QK_DIM), lambda i: (0, i, 0)),
                   pl.BlockSpec((MLA_HEADS, 1, V_DIM, tm), lambda i: (0, i, 0, 0)),
                   pl.BlockSpec((tm, MLA_HEADS * V_DIM), lambda i: (i, 0))],
        out_shape=[jax.ShapeDtypeStruct((MLA_HEADS, t, QK_DIM), BF16),
                   jax.ShapeDtypeStruct((MLA_HEADS, t, QK_DIM), BF16),
                   jax.ShapeDtypeStruct((MLA_HEADS, t // tm, V_DIM, tm), BF16),
                   jax.ShapeDtypeStruct((t, MLA_HEADS * V_DIM), BF16)],
        compiler_params=_params("parallel"),
        name="mla_proj",
    )(x, cos, sin, ng, w_in, qn, w_uq, kvn, w_ukv)


def _attn_kernel(q_ref, k_ref, vt_ref, gate_ref, o_ref, st0_ref, st1_ref, acc_ref, *, blk, sub, nq):
    nsub = blk // sub

    def query_block(qi, carry):
        rows = pl.ds(pl.multiple_of(qi * blk, blk), blk)
        q = q_ref[0, rows, :]

        def scores(kb, st_ref):
            start = pl.multiple_of(kb * blk, blk)
            st = _dot_nt(k_ref[0, pl.ds(start, blk), :], q)
            st_ref[...] = st
            return jnp.max(st, axis=0, keepdims=True)

        def update(kb, st_ref, st_max, m, masked):
            vt = jnp.concatenate([vt_ref[0, kb * nsub + j] for j in range(nsub)], axis=1)
            vt = jnp.concatenate([vt, jnp.ones((ONES_ROWS, blk), vt.dtype)], axis=0)
            st = st_ref[...]
            if masked:
                key = lax.broadcasted_iota(jnp.int32, st.shape, 0)
                qry = lax.broadcasted_iota(jnp.int32, st.shape, 1)
                st = jnp.where(key <= qry, st, MASK_VALUE)
                st_max = jnp.max(st, axis=0, keepdims=True)
            m_new = jnp.maximum(m, st_max)
            pt = jnp.exp2((st - m_new).astype(BF16))
            corr = jnp.exp2(m - m_new)
            acc_ref[...] = acc_ref[...] * corr + _dot(vt, pt)
            return m_new

        acc_ref[...] = jnp.zeros_like(acc_ref)
        m_init = jnp.full((1, blk), MASK_VALUE, F32)

        def body(j, c):
            mx, m = c
            mx1 = scores(2 * j + 1, st1_ref)
            m = update(2 * j, st0_ref, mx, m, False)
            mx2 = scores(2 * j + 2, st0_ref)
            return mx2, update(2 * j + 1, st1_ref, mx1, m, False)

        npairs = qi // 2
        c = lax.fori_loop(0, npairs // 2, lambda j, c: body(2 * j + 1, body(2 * j, c)),
                          (scores(0, st0_ref), m_init))
        c = lax.cond(npairs % 2 == 1, lambda c: body(npairs - 1, c), lambda c: c, c)

        def tail_even(c):
            update(qi, st0_ref, c[0], c[1], True)

        def tail_odd(c):
            mx1 = scores(qi, st1_ref)
            m = update(qi - 1, st0_ref, c[0], c[1], False)
            update(qi, st1_ref, mx1, m, True)

        lax.cond(qi % 2 == 0, tail_even, tail_odd, c)
        gate = gate_ref[rows, :].astype(F32)
        out = acc_ref[:V_DIM, :] / acc_ref[V_DIM:V_DIM + 1, :]
        o_ref[rows, :] = (out.T * _silu(gate)).astype(o_ref.dtype)
        return carry

    lax.fori_loop(0, nq, query_block, 0)


def _attention(q, k, vt, gate, bsz, seq):
    blk = ATTN_TILE
    sub = vt.shape[3]
    nv = seq // sub
    return pl.pallas_call(
        functools.partial(_attn_kernel, blk=blk, sub=sub, nq=seq // blk),
        grid=(bsz, MLA_HEADS),
        in_specs=[pl.BlockSpec((1, seq, QK_DIM), lambda b, h: (h, b, 0)),
                  pl.BlockSpec((1, seq, QK_DIM), lambda b, h: (h, b, 0)),
                  pl.BlockSpec((1, nv, V_DIM, sub), lambda b, h: (h, b, 0, 0)),
                  pl.BlockSpec((seq, V_DIM), lambda b, h: (b, h))],
        out_specs=pl.BlockSpec((seq, V_DIM), lambda b, h: (b, h)),
        out_shape=jax.ShapeDtypeStruct((bsz * seq, MLA_HEADS * V_DIM), BF16),
        scratch_shapes=[pltpu.VMEM((blk, blk), F32), pltpu.VMEM((blk, blk), F32),
                        pltpu.VMEM((V_DIM + ONES_ROWS, blk), F32)],
        compiler_params=_params("parallel", "parallel"),
        name="mla_attention",
    )(q, k, vt, gate)


def _seg_sum(x, seg, segt):
    s = jnp.dot(x.astype(BF16), seg, preferred_element_type=F32)
    hi = s.astype(BF16)
    lo = (s - hi.astype(F32)).astype(BF16)
    return (jnp.dot(hi, segt, preferred_element_type=F32)
            + jnp.dot(lo, segt, preferred_element_type=F32))


def _finish(y, w_ref, x_ref, g_ref, o_ref, final_norm):
    x = x_ref[...] + jnp.dot(y.astype(BF16), w_ref[...], preferred_element_type=F32)
    if final_norm:
        x = _rms(x, g_ref[...])
    o_ref[...] = x


def _out_proj_kernel(y_ref, w_ref, x_ref, g_ref, o_ref, *, final_norm):
    _finish(y_ref[...], w_ref, x_ref, g_ref, o_ref, final_norm)


def _rwkv_out_kernel(y_ref, bv_ref, sg_ref, lng_ref, lnb_ref, seg_ref, segt_ref, w_ref, x_ref, g_ref,
                     o_ref, *, final_norm):
    y = y_ref[...].astype(F32)
    mean = _seg_sum(y, seg_ref[...], segt_ref[...]) * (1.0 / RWKV_HEAD)
    yc = y - mean
    var = _seg_sum(yc * yc, seg_ref[...], segt_ref[...]) * (1.0 / RWKV_HEAD)
    yn = yc * lax.rsqrt(var + GN_EPS) * lng_ref[...] + lnb_ref[...]
    out = (yn + bv_ref[...].astype(F32)) * sg_ref[...].astype(F32)
    _finish(out, w_ref, x_ref, g_ref, o_ref, final_norm)


def _out_proj(y, w, x, final_g, final_norm, gn=None):
    t, d = x.shape
    tm = OUT_TILE if gn is None else GN_TILE
    tile = lambda a: pl.BlockSpec((tm, a.shape[1]), lambda i: (i, 0))
    const = lambda a: pl.BlockSpec(a.shape, lambda i: (0, 0))
    if gn is None:
        body, ops = _out_proj_kernel, [y]
        specs = [tile(y)]
    else:
        bv, sg, lng, lnb, seg, segt = gn
        body, ops = _rwkv_out_kernel, [y, bv, sg, lng, lnb, seg, segt]
        specs = [tile(y), tile(bv), tile(sg), const(lng), const(lnb), const(seg), const(segt)]
    return pl.pallas_call(
        functools.partial(body, final_norm=final_norm),
        grid=(t // tm,),
        in_specs=specs + [const(w), tile(x), const(final_g)],
        out_specs=pl.BlockSpec((tm, d), lambda i: (i, 0)),
        out_shape=jax.ShapeDtypeStruct((t, d), F32),
        compiler_params=_params("parallel"),
        name="out_proj" if gn is None else "rwkv_out",
    )(*ops, w, x, final_g)


def _split3(x):
    hi = x.astype(BF16)
    r1 = x - hi.astype(F32)
    mid = r1.astype(BF16)
    lo = (r1 - mid.astype(F32)).astype(BF16)
    return hi, mid, lo


def _rwkv_proj_kernel(x_ref, ng_ref, win_ref, mu_ref, w0_ref, ww2_ref, a0_ref, wa2_ref,
                      kk_ref, ka_ref, rk_ref, tri_ref, seg_ref, segt_ref,
                      rt_ref, at_ref, bt_ref, kt_ref, bh_ref, kh_ref, v_ref, wl_ref, bv_ref, sg_ref,
                      prev_ref, *, width):
    @pl.when(pl.program_id(1) == 0)
    def _():
        prev_ref[...] = jnp.zeros_like(prev_ref)

    h = _rms(x_ref[...], ng_ref[...]).astype(BF16)
    rows = h.shape[0]
    nchunks = rows // CHUNK
    row8 = lax.broadcasted_iota(jnp.int32, (8, 1), 0)
    tri = tri_ref[...]

    def project(cols):
        proj = jnp.dot(h, win_ref[:, cols], preferred_element_type=F32)
        rolled = pltpu.roll(proj, 1, 0)
        shifted = jnp.concatenate([jnp.where(row8 == 0, prev_ref[0:1, cols], rolled[:8]), rolled[8:]],
                                  axis=0)
        prev_ref[0:1, cols] = proj[rows - 1:rows, :]
        return proj + mu_ref[:, cols] * (shifted - proj)

    lora = project(slice(4 * width, 4 * width + 2 * LORA))
    lane = lax.broadcasted_iota(jnp.int32, lora.shape, 1)
    lora = jnp.where(lane < LORA, jnp.tanh(lora), lora).astype(BF16)

    for cb in range(width // COL_BLOCK):
        cs = slice(cb * COL_BLOCK, (cb + 1) * COL_BLOCK)
        r, k, v, g = (project(slice(part * width + cs.start, part * width + cs.stop))
                      for part in range(4))
        sg_ref[:, cs] = _silu(g).astype(sg_ref.dtype)

        seg_sum = functools.partial(_seg_sum, seg=seg_ref[cs, :], segt=segt_ref[:, cs])

        z = w0_ref[:, cs] + jnp.dot(lora, ww2_ref[:, cs], preferred_element_type=F32)
        log_decay = (-LOG2E * EXP_M05) / (1.0 + jnp.exp(-z))
        a = 1.0 / (1.0 + jnp.exp(-(a0_ref[:, cs]
                                    + jnp.dot(lora, wa2_ref[:, cs], preferred_element_type=F32))))

        kk = k * kk_ref[:, cs]
        kk = kk * lax.rsqrt(jnp.maximum(seg_sum(kk * kk), KK_EPS ** 2))
        kp = k * (1.0 + (a - 1.0) * ka_ref[:, cs])
        b = kk * a
        bv_ref[:, cs] = (seg_sum(r * kp * rk_ref[:, cs]) * v).astype(bv_ref.dtype)

        parts = _split3(log_decay)
        cum = jnp.concatenate(
            [sum(jnp.dot(tri, part[sb:sb + TRI_ROWS], preferred_element_type=F32) for part in parts)
             for sb in range(0, rows, TRI_ROWS)], axis=0)
        last = [cum[(c + 1) * CHUNK - 1:(c + 1) * CHUNK, :] for c in range(nchunks)]
        cum_last = jnp.concatenate([jnp.broadcast_to(x, (CHUNK, COL_BLOCK)) for x in last], axis=0)
        wl_ref[:, cs] = jnp.concatenate([jnp.broadcast_to(jnp.exp2(x), (8, COL_BLOCK)) for x in last],
                                        axis=0)
        w_inv = jnp.exp2(-cum)
        w_tail = jnp.exp2(cum_last - cum)
        outs = ((rt_ref, r * jnp.exp2(cum)), (at_ref, -kk * jnp.exp2(cum - log_decay)),
                (bt_ref, b * w_inv), (kt_ref, kp * w_inv), (bh_ref, b * w_tail),
                (kh_ref, kp * w_tail), (v_ref, v))
        for ref, val in outs:
            for gj in range(COL_BLOCK // GW):
                ref[cb * (COL_BLOCK // GW) + gj] = val[:, gj * GW:(gj + 1) * GW].astype(ref.dtype)


def _rwkv_proj(x, ng, w_in, mu, w0, ww2, a0, wa2, kk, ka, rk, tri, seg, segt, bsz, seq):
    t, d = x.shape
    width = w0.shape[1]
    ngroups = width // GW
    tm = TOKEN_TILE
    ns = seq // tm
    const = lambda a: pl.BlockSpec(a.shape, lambda b, i: (0,) * a.ndim)
    consts = [ng, w_in, mu, w0, ww2, a0, wa2, kk, ka, rk, tri, seg, segt]
    gspec = pl.BlockSpec((ngroups, tm, GW), lambda b, i: (0, b * ns + i, 0))
    nspec = pl.BlockSpec((tm, width), lambda b, i: (b * ns + i, 0))
    wl_rows = tm // CHUNK * 8
    return pl.pallas_call(
        functools.partial(_rwkv_proj_kernel, width=width),
        grid=(bsz, ns),
        in_specs=[pl.BlockSpec((tm, d), lambda b, i: (b * ns + i, 0))] + [const(a) for a in consts],
        out_specs=[gspec] * 7 + [pl.BlockSpec((wl_rows, width), lambda b, i: (b * ns + i, 0)),
                                 nspec, nspec],
        out_shape=[jax.ShapeDtypeStruct((ngroups, t, GW), BF16)] * 7
        + [jax.ShapeDtypeStruct((t // CHUNK * 8, width), F32),
           jax.ShapeDtypeStruct((t, width), BF16), jax.ShapeDtypeStruct((t, width), BF16)],
        scratch_shapes=[pltpu.VMEM((8, w_in.shape[1]), F32)],
        compiler_params=_params("parallel", "arbitrary"),
        name="rwkv_proj",
    )(x, *consts)


def _block_diag(y, bd_mask):
    return jnp.where(bd_mask, jnp.concatenate([y.astype(BF16)] * GROUP_HEADS, axis=0), 0)


def _wkv_kernel(rt_ref, at_ref, bt_ref, kt_ref, bh_ref, kh_ref, v_ref, wl_ref, y_ref, state_ref, *,
                ngroups, nchunks):
    @pl.when(pl.program_id(1) == 0)
    def _():
        state_ref[...] = jnp.zeros_like(state_ref)

    t_idx = lax.broadcasted_iota(jnp.int32, (CHUNK, GW), 0)
    s_idx = lax.broadcasted_iota(jnp.int32, (CHUNK, GW), 1) % CHUNK
    incl = s_idx <= t_idx
    strict = s_idx < t_idx
    eye = (s_idx == t_idx).astype(F32)
    blk8 = strict & ((s_idx // 8) == (t_idx // 8))
    off_masks = [((s_idx // (2 * b)) == (t_idx // (2 * b))) & ((t_idx // b) % 2 == 1)
                 & ((s_idx // b) % 2 == 0) for b in (8, 16, 32)]
    bd_mask = (lax.broadcasted_iota(jnp.int32, (GW, GW), 0) // RWKV_HEAD
               == lax.broadcasted_iota(jnp.int32, (GW, GW), 1) // RWKV_HEAD)

    def bd(y):
        return _block_diag(y, bd_mask)

    def mm(x, y):
        return _dot(x, bd(y))

    gs = range(ngroups)
    chains = [(c, gi) for c in range(nchunks) for gi in gs]
    rows = [slice(c * CHUNK, (c + 1) * CHUNK) for c, _ in chains]
    r_t = [rt_ref[gi, rw, :] for (_, gi), rw in zip(chains, rows)]
    a_t = [at_ref[gi, rw, :] for (_, gi), rw in zip(chains, rows)]
    v = [v_ref[gi, rw, :] for (_, gi), rw in zip(chains, rows)]

    amat = [_dot_nt(jnp.concatenate([r_t[n], a_t[n]], axis=0),
                    jnp.concatenate([bd(bt_ref[gi, rows[n], :]), bd(kt_ref[gi, rows[n], :])], axis=0))
            for n, (_, gi) in enumerate(chains)]
    a_rb = [jnp.where(incl, x[:CHUNK, :GW], 0.0) for x in amat]
    a_rk = [jnp.where(incl, x[:CHUNK, GW:], 0.0) for x in amat]
    m_ab = [jnp.where(strict, x[CHUNK:, :GW], 0.0) for x in amat]
    a_ak = [jnp.where(strict, x[CHUNK:, GW:], 0.0) for x in amat]

    n1 = [jnp.where(blk8, x, 0.0) for x in m_ab]
    n2 = [mm(x, x) for x in n1]
    n4 = [mm(x, x) for x in n2]
    tmat = [eye + x for x in n1]
    tmat = [x + mm(x, y) for x, y in zip(tmat, n2)]
    tmat = [x + mm(x, y) for x, y in zip(tmat, n4)]
    for off in off_masks:
        z = [mm(jnp.where(off, m, 0.0), x) for m, x in zip(m_ab, tmat)]
        tmat = [x + mm(x, y) for x, y in zip(tmat, z)]

    akv = [mm(x, y) for x, y in zip(a_ak, v)]
    pq = [_dot(tmat[n], jnp.concatenate([bd(a_t[n]), bd(akv[n])], axis=1))
          for n in range(len(chains))]

    state = [state_ref[gi] for gi in gs]
    for c in range(nchunks):
        ns = [c * ngroups + gi for gi in gs]
        us = [_dot_nt(jnp.concatenate([pq[n][:, :GW].astype(BF16), r_t[n]], axis=0), state[gi])
              for gi, n in zip(gs, ns)]
        u = [us[gi][:CHUNK] + pq[n][:, GW:] for gi, n in zip(gs, ns)]
        y = [us[gi][CHUNK:]
             + _dot(jnp.concatenate([a_rb[n], a_rk[n]], axis=1),
                    jnp.concatenate([bd(u[gi]), bd(v[n])], axis=0)) for gi, n in zip(gs, ns)]
        upd = [_dot(jnp.concatenate([u[gi], v[n].astype(F32)], axis=0).T,
                    jnp.concatenate([bh_ref[gi, rows[n], :], kh_ref[gi, rows[n], :]], axis=0))
               for gi, n in zip(gs, ns)]
        wl = wl_ref[c * 8:c * 8 + 1, :]
        state = [state[gi] * wl[:, gi * GW:(gi + 1) * GW] + jnp.where(bd_mask, upd[gi], 0.0)
                 for gi in gs]
        for gi in gs:
            y_ref[rows[ns[gi]], gi * GW:(gi + 1) * GW] = y[gi].astype(y_ref.dtype)
    for gi in gs:
        state_ref[gi] = state[gi]


def _wkv(rt, at, bt, kt, bh, kh, v, wl, bsz, seq):
    ngroups, t, _ = rt.shape
    tt = WKV_TILE
    ns = seq // tt
    gspec = pl.BlockSpec((ngroups, tt, GW), lambda b, i: (0, b * ns + i, 0))
    return pl.pallas_call(
        functools.partial(_wkv_kernel, ngroups=ngroups, nchunks=tt // CHUNK),
        grid=(bsz, ns),
        in_specs=[gspec] * 7 + [pl.BlockSpec((tt // CHUNK * 8, ngroups * GW),
                                             lambda b, i: (b * ns + i, 0))],
        out_specs=pl.BlockSpec((tt, ngroups * GW), lambda b, i: (b * ns + i, 0)),
        out_shape=jax.ShapeDtypeStruct((t, ngroups * GW), BF16),
        scratch_shapes=[pltpu.VMEM((ngroups, GW, GW), F32)],
        compiler_params=_params("parallel", "arbitrary"),
        name="rwkv_wkv",
    )(rt, at, bt, kt, bh, kh, v, wl)


def _row(v):
    return v.reshape(1, -1).astype(F32)


def kernel(x, positions, norm_g, mla_w_in, mla_q_norm, mla_w_uq, mla_kv_norm, mla_w_ukv, mla_w_out, rwkv_w_in, rwkv_mu, rwkv_w0, rwkv_w_w2, rwkv_a0, rwkv_w_a2, rwkv_k_k, rwkv_k_a, rwkv_r_k, rwkv_ln_g, rwkv_ln_b, rwkv_w_out, final_g):
    bsz, seq, d = x.shape
    depth = norm_g.shape[0]
    t = bsz * seq
    xf = x.reshape(t, d)
    half = QK_ROPE // 2
    inv_freq = 1.0 / (ROPE_THETA ** (jnp.arange(half, dtype=F32) / half))
    ang = positions.reshape(t, 1).astype(F32) * inv_freq[None, :]
    cos = jnp.tile(jnp.cos(ang), (1, LANES // half))
    sin = jnp.tile(jnp.sin(ang), (1, LANES // half))
    final_row = _row(final_g)

    width = rwkv_w0.shape[1]
    tok = jnp.arange(TRI_ROWS)
    tri = ((tok[:, None] // CHUNK == tok[None, :] // CHUNK) & (tok[None, :] <= tok[:, None])).astype(BF16)
    seg = (jnp.arange(width)[:, None] // RWKV_HEAD == jnp.arange(LANES)[None, :]).astype(BF16)
    segt = seg.T

    for i in range(depth):
        j = i // 2
        ng = _row(norm_g[i])
        last = i == depth - 1
        if i % 2 == 0:
            w_in = mla_w_in[j]
            lat = Q_LORA + KV_LORA
            w_in = jnp.concatenate(
                [w_in[:, :lat], w_in[:, lat + QK_ROPE:], w_in[:, lat:lat + QK_ROPE],
                 jnp.zeros((d, LANES - QK_ROPE), w_in.dtype)], axis=1).astype(BF16)
            w_uq = mla_w_uq[j].reshape(Q_LORA, MLA_HEADS, QK_DIM)
            w_uq = jnp.concatenate([w_uq[:, :, :QK_NOPE].reshape(Q_LORA, -1),
                                    w_uq[:, :, QK_NOPE:].reshape(Q_LORA, -1)], axis=1).astype(BF16)
            w_ukv = mla_w_ukv[j].reshape(KV_LORA, MLA_HEADS, QK_NOPE + V_DIM)
            w_ukv = jnp.concatenate([w_ukv[:, :, :QK_NOPE].reshape(KV_LORA, -1),
                                     w_ukv[:, :, QK_NOPE:].reshape(KV_LORA, -1)], axis=1).astype(BF16)
            q, k, vt, gate = _mla_proj(xf, cos, sin, ng, w_in, _row(mla_q_norm[j]), w_uq,
                                       _row(mla_kv_norm[j]), w_ukv)
            y = _attention(q, k, vt, gate, bsz, seq)
            xf = _out_proj(y, mla_w_out[j].astype(BF16), xf, final_row, last)
        else:
            zeros = jnp.zeros((LORA, width), F32)
            ww2 = jnp.concatenate([rwkv_w_w2[j], zeros], axis=0).astype(BF16)
            wa2 = jnp.concatenate([zeros, rwkv_w_a2[j]], axis=0).astype(BF16)
            *ops, wl, bv, sg = _rwkv_proj(
                xf, ng, rwkv_w_in[j].astype(BF16), _row(rwkv_mu[j]), _row(rwkv_w0[j]), ww2,
                _row(rwkv_a0[j]), wa2, _row(rwkv_k_k[j]), _row(rwkv_k_a[j]), _row(rwkv_r_k[j]),
                tri, seg, segt, bsz, seq)
            y = _wkv(*ops, wl, bsz, seq)
            gn = (bv, sg, _row(rwkv_ln_g[j]), _row(rwkv_ln_b[j]), seg, segt)
            xf = _out_proj(y, rwkv_w_out[j].astype(BF16), xf, final_row, last, gn)
    return xf.reshape(bsz, seq, d)
```

```python
import functools

import jax
import jax.numpy as jnp
from jax import lax
from jax.experimental import pallas as pl
from jax.experimental.pallas import tpu as pltpu

F32 = jnp.float32
BF16 = jnp.bfloat16

NORM_EPS = 1e-6
GN_EPS = 64e-5
KK_EPS = 1e-12
ROPE_THETA = 10000.0

MLA_HEADS = 8
QK_NOPE = 128
QK_ROPE = 64
QK_DIM = QK_NOPE + QK_ROPE
V_DIM = 128
Q_LORA = 768
KV_LORA = 256

RWKV_HEAD = 64
LORA = 64

LANES = 128
CHUNK = 64
GROUP_HEADS = 2
GW = GROUP_HEADS * RWKV_HEAD

TOKEN_TILE = 512
GN_TILE = 256
TRI_ROWS = 256
MLA_TILE = 512
OUT_TILE = 512
ATTN_TILE = 1024
COL_BLOCK = 256
WKV_TILE = 256
VMEM_LIMIT = 56 * 1024 * 1024
MASK_VALUE = -1e30
ONES_ROWS = 16
LOG2E = 1.4426950408889634
EXP_M05 = 0.6065306597126334


def _params(*sem):
    return pltpu.CompilerParams(dimension_semantics=sem, vmem_limit_bytes=VMEM_LIMIT)


def _rms(x, g):
    return x * lax.rsqrt(jnp.mean(x * x, axis=-1, keepdims=True) + NORM_EPS) * g


def _dot(a, b):
    return jnp.dot(a.astype(BF16), b.astype(BF16), preferred_element_type=F32)


def _dot_nt(a, b):
    return lax.dot_general(a.astype(BF16), b.astype(BF16), (((1,), (1,)), ((), ())),
                           preferred_element_type=F32)


def _silu(x):
    return x / (1.0 + jnp.exp(-x))


def _rope(t, cos, sin_signed, first):
    partner = jnp.where(first, pltpu.roll(t, LANES - QK_ROPE // 2, 1), pltpu.roll(t, QK_ROPE // 2, 1))
    return t * cos + partner * sin_signed


def _mla_proj_kernel(x_ref, cos_ref, sin_ref, ng_ref, win_ref, qn_ref, wuq_ref, kvn_ref, wukv_ref,
                     q_ref, k_ref, v_ref, gate_ref):
    x = x_ref[...]
    h = _rms(x, ng_ref[...])
    proj = _dot(h, win_ref[...])
    q_lat = proj[:, :Q_LORA]
    kv_lat = proj[:, Q_LORA:Q_LORA + KV_LORA]
    lat = Q_LORA + KV_LORA
    gate_w = MLA_HEADS * V_DIM
    gate_ref[...] = proj[:, lat:lat + gate_w].astype(gate_ref.dtype)
    k_rope = proj[:, lat + gate_w:lat + gate_w + LANES]

    q = _dot(_rms(q_lat, qn_ref[...]), wuq_ref[...]) * (QK_DIM ** -0.5 * LOG2E)
    kv = _dot(_rms(kv_lat, kvn_ref[...]), wukv_ref[...])

    cos = cos_ref[...]
    sin = sin_ref[...]
    lane = lax.broadcasted_iota(jnp.int32, cos.shape, 1)
    first = (lane % QK_ROPE) < (QK_ROPE // 2)
    sin_signed = jnp.where(first, -sin, sin)

    k_rope = _rope(k_rope, cos, sin_signed, first)[:, :QK_ROPE].astype(k_ref.dtype)
    nope_w = MLA_HEADS * QK_NOPE
    for hp in range(MLA_HEADS // 2):
        q_rope = _rope(q[:, nope_w + hp * LANES: nope_w + (hp + 1) * LANES], cos, sin_signed, first)
        for hh in range(2):
            hd = 2 * hp + hh
            q_ref[hd, :, :QK_NOPE] = q[:, hd * QK_NOPE:(hd + 1) * QK_NOPE].astype(q_ref.dtype)
            q_ref[hd, :, QK_NOPE:] = q_rope[:, hh * QK_ROPE:(hh + 1) * QK_ROPE].astype(q_ref.dtype)
    for hd in range(MLA_HEADS):
        k_ref[hd, :, :QK_NOPE] = kv[:, hd * QK_NOPE:(hd + 1) * QK_NOPE].astype(k_ref.dtype)
        k_ref[hd, :, QK_NOPE:] = k_rope
        v_ref[hd, 0] = kv[:, nope_w + hd * V_DIM: nope_w + (hd + 1) * V_DIM].T.astype(v_ref.dtype)


def _mla_proj(x, cos, sin, ng, w_in, qn, w_uq, kvn, w_ukv):
    t, d = x.shape
    tm = MLA_TILE
    const = lambda shape: pl.BlockSpec(shape, lambda i: (0,) * len(shape))
    return pl.pallas_call(
        _mla_proj_kernel,
        grid=(t // tm,),
        in_specs=[pl.BlockSpec((tm, d), lambda i: (i, 0)),
                  pl.BlockSpec((tm, LANES), lambda i: (i, 0)),
                  pl.BlockSpec((tm, LANES), lambda i: (i, 0)),
                  const(ng.shape), const(w_in.shape), const(qn.shape), const(w_uq.shape),
                  const(kvn.shape), const(w_ukv.shape)],
        out_specs=[pl.BlockSpec((MLA_HEADS, tm, QK_DIM), lambda i: (0, i, 0)),
                   pl.BlockSpec((MLA_HEADS, tm, QK_DIM), lambda i: (0, i, 0)),
                   pl.BlockSpec((MLA_HEADS, 1, V_DIM, tm), lambda i: (0, i, 0, 0)),
                   pl.BlockSpec((tm, MLA_HEADS * V_DIM), lambda i: (i, 0))],
        out_shape=[jax.ShapeDtypeStruct((MLA_HEADS, t, QK_DIM), BF16),
                   jax.ShapeDtypeStruct((MLA_HEADS, t, QK_DIM), BF16),
                   jax.ShapeDtypeStruct((MLA_HEADS, t // tm, V_DIM, tm), BF16),
                   jax.ShapeDtypeStruct((t, MLA_HEADS * V_DIM), BF16)],
        compiler_params=_params("parallel"),
        name="mla_proj",
    )(x, cos, sin, ng, w_in, qn, w_uq, kvn, w_ukv)


def _attn_kernel(q_ref, k_ref, vt_ref, gate_ref, o_ref, st0_ref, st1_ref, acc_ref, *, blk, sub, nq):
    nsub = blk // sub

    def query_block(qi, carry):
        rows = pl.ds(pl.multiple_of(qi * blk, blk), blk)
        q = q_ref[0, rows, :]

        def scores(kb, st_ref):
            start = pl.multiple_of(kb * blk, blk)
            st = _dot_nt(k_ref[0, pl.ds(start, blk), :], q)
            st_ref[...] = st
            return jnp.max(st, axis=0, keepdims=True)

        def update(kb, st_ref, st_max, m, masked):
            vt = jnp.concatenate([vt_ref[0, kb * nsub + j] for j in range(nsub)], axis=1)
            vt = jnp.concatenate([vt, jnp.ones((ONES_ROWS, blk), vt.dtype)], axis=0)
            st = st_ref[...]
            if masked:
                key = lax.broadcasted_iota(jnp.int32, st.shape, 0)
                qry = lax.broadcasted_iota(jnp.int32, st.shape, 1)
                st = jnp.where(key <= qry, st, MASK_VALUE)
                st_max = jnp.max(st, axis=0, keepdims=True)
            m_new = jnp.maximum(m, st_max)
            pt = jnp.exp2((st - m_new).astype(BF16))
            corr = jnp.exp2(m - m_new)
            acc_ref[...] = acc_ref[...] * corr + _dot(vt, pt)
            return m_new

        acc_ref[...] = jnp.zeros_like(acc_ref)
        m_init = jnp.full((1, blk), MASK_VALUE, F32)

        def body(j, c):
            mx, m = c
            mx1 = scores(2 * j + 1, st1_ref)
            m = update(2 * j, st0_ref, mx, m, False)
            mx2 = scores(2 * j + 2, st0_ref)
            return mx2, update(2 * j + 1, st1_ref, mx1, m, False)

        npairs = qi // 2
        c = lax.fori_loop(0, npairs // 2, lambda j, c: body(2 * j + 1, body(2 * j, c)),
                          (scores(0, st0_ref), m_init))
        c = lax.cond(npairs % 2 == 1, lambda c: body(npairs - 1, c), lambda c: c, c)

        def tail_even(c):
            update(qi, st0_ref, c[0], c[1], True)

        def tail_odd(c):
            mx1 = scores(qi, st1_ref)
            m = update(qi - 1, st0_ref, c[0], c[1], False)
            update(qi, st1_ref, mx1, m, True)

        lax.cond(qi % 2 == 0, tail_even, tail_odd, c)
        gate = gate_ref[rows, :].astype(F32)
        out = acc_ref[:V_DIM, :] / acc_ref[V_DIM:V_DIM + 1, :]
        o_ref[rows, :] = (out.T * _silu(gate)).astype(o_ref.dtype)
        return carry

    lax.fori_loop(0, nq, query_block, 0)


def _attention(q, k, vt, gate, bsz, seq):
    blk = ATTN_TILE
    sub = vt.shape[3]
    nv = seq // sub
    return pl.pallas_call(
        functools.partial(_attn_kernel, blk=blk, sub=sub, nq=seq // blk),
        grid=(bsz, MLA_HEADS),
        in_specs=[pl.BlockSpec((1, seq, QK_DIM), lambda b, h: (h, b, 0)),
                  pl.BlockSpec((1, seq, QK_DIM), lambda b, h: (h, b, 0)),
                  pl.BlockSpec((1, nv, V_DIM, sub), lambda b, h: (h, b, 0, 0)),
                  pl.BlockSpec((seq, V_DIM), lambda b, h: (b, h))],
        out_specs=pl.BlockSpec((seq, V_DIM), lambda b, h: (b, h)),
        out_shape=jax.ShapeDtypeStruct((bsz * seq, MLA_HEADS * V_DIM), BF16),
        scratch_shapes=[pltpu.VMEM((blk, blk), F32), pltpu.VMEM((blk, blk), F32),
                        pltpu.VMEM((V_DIM + ONES_ROWS, blk), F32)],
        compiler_params=_params("parallel", "parallel"),
        name="mla_attention",
    )(q, k, vt, gate)


def _seg_sum(x, seg, segt):
    s = jnp.dot(x.astype(BF16), seg, preferred_element_type=F32)
    hi = s.astype(BF16)
    lo = (s - hi.astype(F32)).astype(BF16)
    return (jnp.dot(hi, segt, preferred_element_type=F32)
            + jnp.dot(lo, segt, preferred_element_type=F32))


def _finish(y, w_ref, x_ref, g_ref, o_ref, final_norm):
    x = x_ref[...] + jnp.dot(y.astype(BF16), w_ref[...], preferred_element_type=F32)
    if final_norm:
        x = _rms(x, g_ref[...])
    o_ref[...] = x


def _out_proj_kernel(y_ref, w_ref, x_ref, g_ref, o_ref, *, final_norm):
    _finish(y_ref[...], w_ref, x_ref, g_ref, o_ref, final_norm)


def _rwkv_out_kernel(y_ref, bv_ref, sg_ref, lng_ref, lnb_ref, seg_ref, segt_ref, w_ref, x_ref, g_ref,
                     o_ref, *, final_norm):
    y = y_ref[...].astype(F32)
    mean = _seg_sum(y, seg_ref[...], segt_ref[...]) * (1.0 / RWKV_HEAD)
    yc = y - mean
    var = _seg_sum(yc * yc, seg_ref[...], segt_ref[...]) * (1.0 / RWKV_HEAD)
    yn = yc * lax.rsqrt(var + GN_EPS) * lng_ref[...] + lnb_ref[...]
    out = (yn + bv_ref[...].astype(F32)) * sg_ref[...].astype(F32)
    _finish(out, w_ref, x_ref, g_ref, o_ref, final_norm)


def _out_proj(y, w, x, final_g, final_norm, gn=None):
    t, d = x.shape
    tm = OUT_TILE if gn is None else GN_TILE
    tile = lambda a: pl.BlockSpec((tm, a.shape[1]), lambda i: (i, 0))
    const = lambda a: pl.BlockSpec(a.shape, lambda i: (0, 0))
    if gn is None:
        body, ops = _out_proj_kernel, [y]
        specs = [tile(y)]
    else:
        bv, sg, lng, lnb, seg, segt = gn
        body, ops = _rwkv_out_kernel, [y, bv, sg, lng, lnb, seg, segt]
        specs = [tile(y), tile(bv), tile(sg), const(lng), const(lnb), const(seg), const(segt)]
    return pl.pallas_call(
        functools.partial(body, final_norm=final_norm),
        grid=(t // tm,),
        in_specs=specs + [const(w), tile(x), const(final_g)],
        out_specs=pl.BlockSpec((tm, d), lambda i: (i, 0)),
        out_shape=jax.ShapeDtypeStruct((t, d), F32),
        compiler_params=_params("parallel"),
        name="out_proj" if gn is None else "rwkv_out",
    )(*ops, w, x, final_g)


def _split3(x):
    hi = x.astype(BF16)
    r1 = x - hi.astype(F32)
    mid = r1.astype(BF16)
    lo = (r1 - mid.astype(F32)).astype(BF16)
    return hi, mid, lo


def _rwkv_proj_kernel(x_ref, ng_ref, win_ref, mu_ref, w0_ref, ww2_ref, a0_ref, wa2_ref,
                      kk_ref, ka_ref, rk_ref, tri_ref, seg_ref, segt_ref,
                      rt_ref, at_ref, bt_ref, kt_ref, bh_ref, kh_ref, v_ref, wl_ref, bv_ref, sg_ref,
                      prev_ref, *, width):
    @pl.when(pl.program_id(1) == 0)
    def _():
        prev_ref[...] = jnp.zeros_like(prev_ref)

    h = _rms(x_ref[...], ng_ref[...]).astype(BF16)
    rows = h.shape[0]
    nchunks = rows // CHUNK
    row8 = lax.broadcasted_iota(jnp.int32, (8, 1), 0)
    tri = tri_ref[...]

    def project(cols):
        proj = jnp.dot(h, win_ref[:, cols], preferred_element_type=F32)
        rolled = pltpu.roll(proj, 1, 0)
        shifted = jnp.concatenate([jnp.where(row8 == 0, prev_ref[0:1, cols], rolled[:8]), rolled[8:]],
                                  axis=0)
        prev_ref[0:1, cols] = proj[rows - 1:rows, :]
        return proj + mu_ref[:, cols] * (shifted - proj)

    lora = project(slice(4 * width, 4 * width + 2 * LORA))
    lane = lax.broadcasted_iota(jnp.int32, lora.shape, 1)
    lora = jnp.where(lane < LORA, jnp.tanh(lora), lora).astype(BF16)

    for cb in range(width // COL_BLOCK):
        cs = slice(cb * COL_BLOCK, (cb + 1) * COL_BLOCK)
        r, k, v, g = (project(slice(part * width + cs.start, part * width + cs.stop))
                      for part in range(4))
        sg_ref[:, cs] = _silu(g).astype(sg_ref.dtype)

        seg_sum = functools.partial(_seg_sum, seg=seg_ref[cs, :], segt=segt_ref[:, cs])

        z = w0_ref[:, cs] + jnp.dot(lora, ww2_ref[:, cs], preferred_element_type=F32)
        log_decay = (-LOG2E * EXP_M05) / (1.0 + jnp.exp(-z))
        a = 1.0 / (1.0 + jnp.exp(-(a0_ref[:, cs]
                                    + jnp.dot(lora, wa2_ref[:, cs], preferred_element_type=F32))))

        kk = k * kk_ref[:, cs]
        kk = kk * lax.rsqrt(jnp.maximum(seg_sum(kk * kk), KK_EPS ** 2))
        kp = k * (1.0 + (a - 1.0) * ka_ref[:, cs])
        b = kk * a
        bv_ref[:, cs] = (seg_sum(r * kp * rk_ref[:, cs]) * v).astype(bv_ref.dtype)

        parts = _split3(log_decay)
        cum = jnp.concatenate(
            [sum(jnp.dot(tri, part[sb:sb + TRI_ROWS], preferred_element_type=F32) for part in parts)
             for sb in range(0, rows, TRI_ROWS)], axis=0)
        last = [cum[(c + 1) * CHUNK - 1:(c + 1) * CHUNK, :] for c in range(nchunks)]
        cum_last = jnp.concatenate([jnp.broadcast_to(x, (CHUNK, COL_BLOCK)) for x in last], axis=0)
        wl_ref[:, cs] = jnp.concatenate([jnp.broadcast_to(jnp.exp2(x), (8, COL_BLOCK)) for x in last],
                                        axis=0)
        w_inv = jnp.exp2(-cum)
        w_tail = jnp.exp2(cum_last - cum)
        outs = ((rt_ref, r * jnp.exp2(cum)), (at_ref, -kk * jnp.exp2(cum - log_decay)),
                (bt_ref, b * w_inv), (kt_ref, kp * w_inv), (bh_ref, b * w_tail),
                (kh_ref, kp * w_tail), (v_ref, v))
        for ref, val in outs:
            for gj in range(COL_BLOCK // GW):
                ref[cb * (COL_BLOCK // GW) + gj] = val[:, gj * GW:(gj + 1) * GW].astype(ref.dtype)


def _rwkv_proj(x, ng, w_in, mu, w0, ww2, a0, wa2, kk, ka, rk, tri, seg, segt, bsz, seq):
    t, d = x.shape
    width = w0.shape[1]
    ngroups = width // GW
    tm = TOKEN_TILE
    ns = seq // tm
    const = lambda a: pl.BlockSpec(a.shape, lambda b, i: (0,) * a.ndim)
    consts = [ng, w_in, mu, w0, ww2, a0, wa2, kk, ka, rk, tri, seg, segt]
    gspec = pl.BlockSpec((ngroups, tm, GW), lambda b, i: (0, b * ns + i, 0))
    nspec = pl.BlockSpec((tm, width), lambda b, i: (b * ns + i, 0))
    wl_rows = tm // CHUNK * 8
    return pl.pallas_call(
        functools.partial(_rwkv_proj_kernel, width=width),
        grid=(bsz, ns),
        in_specs=[pl.BlockSpec((tm, d), lambda b, i: (b * ns + i, 0))] + [const(a) for a in consts],
        out_specs=[gspec] * 7 + [pl.BlockSpec((wl_rows, width), lambda b, i: (b * ns + i, 0)),
                                 nspec, nspec],
        out_shape=[jax.ShapeDtypeStruct((ngroups, t, GW), BF16)] * 7
        + [jax.ShapeDtypeStruct((t // CHUNK * 8, width), F32),
           jax.ShapeDtypeStruct((t, width), BF16), jax.ShapeDtypeStruct((t, width), BF16)],
        scratch_shapes=[pltpu.VMEM((8, w_in.shape[1]), F32)],
        compiler_params=_params("parallel", "arbitrary"),
        name="rwkv_proj",
    )(x, *consts)


def _block_diag(y, bd_mask):
    return jnp.where(bd_mask, jnp.concatenate([y.astype(BF16)] * GROUP_HEADS, axis=0), 0)


def _wkv_kernel(rt_ref, at_ref, bt_ref, kt_ref, bh_ref, kh_ref, v_ref, wl_ref, y_ref, state_ref, *,
                ngroups, nchunks):
    @pl.when(pl.program_id(1) == 0)
    def _():
        state_ref[...] = jnp.zeros_like(state_ref)

    t_idx = lax.broadcasted_iota(jnp.int32, (CHUNK, GW), 0)
    s_idx = lax.broadcasted_iota(jnp.int32, (CHUNK, GW), 1) % CHUNK
    incl = s_idx <= t_idx
    strict = s_idx < t_idx
    eye = (s_idx == t_idx).astype(F32)
    blk8 = strict & ((s_idx // 8) == (t_idx // 8))
    off_masks = [((s_idx // (2 * b)) == (t_idx // (2 * b))) & ((t_idx // b) % 2 == 1)
                 & ((s_idx // b) % 2 == 0) for b in (8, 16, 32)]
    bd_mask = (lax.broadcasted_iota(jnp.int32, (GW, GW), 0) // RWKV_HEAD
               == lax.broadcasted_iota(jnp.int32, (GW, GW), 1) // RWKV_HEAD)

    def bd(y):
        return _block_diag(y, bd_mask)

    def mm(x, y):
        return _dot(x, bd(y))

    gs = range(ngroups)
    chains = [(c, gi) for c in range(nchunks) for gi in gs]
    rows = [slice(c * CHUNK, (c + 1) * CHUNK) for c, _ in chains]
    r_t = [rt_ref[gi, rw, :] for (_, gi), rw in zip(chains, rows)]
    a_t = [at_ref[gi, rw, :] for (_, gi), rw in zip(chains, rows)]
    v = [v_ref[gi, rw, :] for (_, gi), rw in zip(chains, rows)]

    amat = [_dot_nt(jnp.concatenate([r_t[n], a_t[n]], axis=0),
                    jnp.concatenate([bd(bt_ref[gi, rows[n], :]), bd(kt_ref[gi, rows[n], :])], axis=0))
            for n, (_, gi) in enumerate(chains)]
    a_rb = [jnp.where(incl, x[:CHUNK, :GW], 0.0) for x in amat]
    a_rk = [jnp.where(incl, x[:CHUNK, GW:], 0.0) for x in amat]
    m_ab = [jnp.where(strict, x[CHUNK:, :GW], 0.0) for x in amat]
    a_ak = [jnp.where(strict, x[CHUNK:, GW:], 0.0) for x in amat]

    n1 = [jnp.where(blk8, x, 0.0) for x in m_ab]
    n2 = [mm(x, x) for x in n1]
    n4 = [mm(x, x) for x in n2]
    tmat = [eye + x for x in n1]
    tmat = [x + mm(x, y) for x, y in zip(tmat, n2)]
    tmat = [x + mm(x, y) for x, y in zip(tmat, n4)]
    for off in off_masks:
        z = [mm(jnp.where(off, m, 0.0), x) for m, x in zip(m_ab, tmat)]
        tmat = [x + mm(x, y) for x, y in zip(tmat, z)]

    akv = [mm(x, y) for x, y in zip(a_ak, v)]
    pq = [_dot(tmat[n], jnp.concatenate([bd(a_t[n]), bd(akv[n])], axis=1))
          for n in range(len(chains))]

    state = [state_ref[gi] for gi in gs]
    for c in range(nchunks):
        ns = [c * ngroups + gi for gi in gs]
        us = [_dot_nt(jnp.concatenate([pq[n][:, :GW].astype(BF16), r_t[n]], axis=0), state[gi])
              for gi, n in zip(gs, ns)]
        u = [us[gi][:CHUNK] + pq[n][:, GW:] for gi, n in zip(gs, ns)]
        y = [us[gi][CHUNK:]
             + _dot(jnp.concatenate([a_rb[n], a_rk[n]], axis=1),
                    jnp.concatenate([bd(u[gi]), bd(v[n])], axis=0)) for gi, n in zip(gs, ns)]
        upd = [_dot(jnp.concatenate([u[gi], v[n].astype(F32)], axis=0).T,
                    jnp.concatenate([bh_ref[gi, rows[n], :], kh_ref[gi, rows[n], :]], axis=0))
               for gi, n in zip(gs, ns)]
        wl = wl_ref[c * 8:c * 8 + 1, :]
        state = [state[gi] * wl[:, gi * GW:(gi + 1) * GW] + jnp.where(bd_mask, upd[gi], 0.0)
                 for gi in gs]
        for gi in gs:
            y_ref[rows[ns[gi]], gi * GW:(gi + 1) * GW] = y[gi].astype(y_ref.dtype)
    for gi in gs:
        state_ref[gi] = state[gi]


def _wkv(rt, at, bt, kt, bh, kh, v, wl, bsz, seq):
    ngroups, t, _ = rt.shape
    tt = WKV_TILE
    ns = seq // tt
    gspec = pl.BlockSpec((ngroups, tt, GW), lambda b, i: (0, b * ns + i, 0))
    return pl.pallas_call(
        functools.partial(_wkv_kernel, ngroups=ngroups, nchunks=tt // CHUNK),
        grid=(bsz, ns),
        in_specs=[gspec] * 7 + [pl.BlockSpec((tt // CHUNK * 8, ngroups * GW),
                                             lambda b, i: (b * ns + i, 0))],
        out_specs=pl.BlockSpec((tt, ngroups * GW), lambda b, i: (b * ns + i, 0)),
        out_shape=jax.ShapeDtypeStruct((t, ngroups * GW), BF16),
        scratch_shapes=[pltpu.VMEM((ngroups, GW, GW), F32)],
        compiler_params=_params("parallel", "arbitrary"),
        name="rwkv_wkv",
    )(rt, at, bt, kt, bh, kh, v, wl)


def _row(v):
    return v.reshape(1, -1).astype(F32)


def kernel(x, positions, norm_g, mla_w_in, mla_q_norm, mla_w_uq, mla_kv_norm, mla_w_ukv, mla_w_out, rwkv_w_in, rwkv_mu, rwkv_w0, rwkv_w_w2, rwkv_a0, rwkv_w_a2, rwkv_k_k, rwkv_k_a, rwkv_r_k, rwkv_ln_g, rwkv_ln_b, rwkv_w_out, final_g):
    bsz, seq, d = x.shape
    depth = norm_g.shape[0]
    t = bsz * seq
    xf = x.reshape(t, d)
    half = QK_ROPE // 2
    inv_freq = 1.0 / (ROPE_THETA ** (jnp.arange(half, dtype=F32) / half))
    ang = positions.reshape(t, 1).astype(F32) * inv_freq[None, :]
    cos = jnp.tile(jnp.cos(ang), (1, LANES // half))
    sin = jnp.tile(jnp.sin(ang), (1, LANES // half))
    final_row = _row(final_g)

    width = rwkv_w0.shape[1]
    tok = jnp.arange(TRI_ROWS)
    tri = ((tok[:, None] // CHUNK == tok[None, :] // CHUNK) & (tok[None, :] <= tok[:, None])).astype(BF16)
    seg = (jnp.arange(width)[:, None] // RWKV_HEAD == jnp.arange(LANES)[None, :]).astype(BF16)
    segt = seg.T

    for i in range(depth):
        j = i // 2
        ng = _row(norm_g[i])
        last = i == depth - 1
        if i % 2 == 0:
            w_in = mla_w_in[j]
            lat = Q_LORA + KV_LORA
            w_in = jnp.concatenate(
                [w_in[:, :lat], w_in[:, lat + QK_ROPE:], w_in[:, lat:lat + QK_ROPE],
                 jnp.zeros((d, LANES - QK_ROPE), w_in.dtype)], axis=1).astype(BF16)
            w_uq = mla_w_uq[j].reshape(Q_LORA, MLA_HEADS, QK_DIM)
            w_uq = jnp.concatenate([w_uq[:, :, :QK_NOPE].reshape(Q_LORA, -1),
                                    w_uq[:, :, QK_NOPE:].reshape(Q_LORA, -1)], axis=1).astype(BF16)
            w_ukv = mla_w_ukv[j].reshape(KV_LORA, MLA_HEADS, QK_NOPE + V_DIM)
            w_ukv = jnp.concatenate([w_ukv[:, :, :QK_NOPE].reshape(KV_LORA, -1),
                                     w_ukv[:, :, QK_NOPE:].reshape(KV_LORA, -1)], axis=1).astype(BF16)
            q, k, vt, gate = _mla_proj(xf, cos, sin, ng, w_in, _row(mla_q_norm[j]), w_uq,
                                       _row(mla_kv_norm[j]), w_ukv)
            y = _attention(q, k, vt, gate, bsz, seq)
            xf = _out_proj(y, mla_w_out[j].astype(BF16), xf, final_row, last)
        else:
            zeros = jnp.zeros((LORA, width), F32)
            ww2 = jnp.concatenate([rwkv_w_w2[j], zeros], axis=0).astype(BF16)
            wa2 = jnp.concatenate([zeros, rwkv_w_a2[j]], axis=0).astype(BF16)
            *ops, wl, bv, sg = _rwkv_proj(
                xf, ng, rwkv_w_in[j].astype(BF16), _row(rwkv_mu[j]), _row(rwkv_w0[j]), ww2,
                _row(rwkv_a0[j]), wa2, _row(rwkv_k_k[j]), _row(rwkv_k_a[j]), _row(rwkv_r_k[j]),
                tri, seg, segt, bsz, seq)
            y = _wkv(*ops, wl, bsz, seq)
            gn = (bv, sg, _row(rwkv_ln_g[j]), _row(rwkv_ln_b[j]), seg, segt)
            xf = _out_proj(y, rwkv_w_out[j].astype(BF16), xf, final_row, last, gn)
    return xf.reshape(bsz, seq, d)
```

```python
import functools

import jax
import jax.numpy as jnp
from jax import lax
from jax.experimental import pallas as pl
from jax.experimental.pallas import tpu as pltpu

F32 = jnp.float32
BF16 = jnp.bfloat16

NORM_EPS = 1e-6
GN_EPS = 64e-5
KK_EPS = 1e-12
ROPE_THETA = 10000.0

MLA_HEADS = 8
QK_NOPE = 128
QK_ROPE = 64
QK_DIM = QK_NOPE + QK_ROPE
V_DIM = 128
Q_LORA = 768
KV_LORA = 256

RWKV_HEAD = 64
LORA = 64

LANES = 128
CHUNK = 64
GROUP_HEADS = 2
GW = GROUP_HEADS * RWKV_HEAD

TOKEN_TILE = 512
GN_TILE = 256
TRI_ROWS = 256
MLA_TILE = 1024
OUT_TILE = 1024
ATTN_TILE = 1024
COL_BLOCK = 256
WKV_TILE = 512
VMEM_LIMIT = 56 * 1024 * 1024
MASK_VALUE = -1e30
ONES_ROWS = 16
LOG2E = 1.4426950408889634
EXP_M05 = 0.6065306597126334


def _params(*sem):
    return pltpu.CompilerParams(dimension_semantics=sem, vmem_limit_bytes=VMEM_LIMIT)


def _rms(x, g):
    return x * lax.rsqrt(jnp.mean(x * x, axis=-1, keepdims=True) + NORM_EPS) * g


def _dot(a, b):
    return jnp.dot(a.astype(BF16), b.astype(BF16), preferred_element_type=F32)


def _dot_nt(a, b):
    return lax.dot_general(a.astype(BF16), b.astype(BF16), (((1,), (1,)), ((), ())),
                           preferred_element_type=F32)


def _silu(x):
    return x / (1.0 + jnp.exp(-x))


def _rope(t, cos, sin_signed, first):
    partner = jnp.where(first, pltpu.roll(t, LANES - QK_ROPE // 2, 1), pltpu.roll(t, QK_ROPE // 2, 1))
    return t * cos + partner * sin_signed


def _mla_proj_kernel(x_ref, cos_ref, sin_ref, ng_ref, win_ref, qn_ref, wuq_ref, kvn_ref, wukv_ref,
                     q_ref, k_ref, v_ref, gate_ref):
    x = x_ref[...]
    h = _rms(x, ng_ref[...])
    proj = _dot(h, win_ref[...])
    q_lat = proj[:, :Q_LORA]
    kv_lat = proj[:, Q_LORA:Q_LORA + KV_LORA]
    lat = Q_LORA + KV_LORA
    gate_w = MLA_HEADS * V_DIM
    gate_ref[...] = proj[:, lat:lat + gate_w].astype(gate_ref.dtype)
    k_rope = proj[:, lat + gate_w:lat + gate_w + LANES]

    q = _dot(_rms(q_lat, qn_ref[...]), wuq_ref[...]) * (QK_DIM ** -0.5 * LOG2E)
    kv = _dot(_rms(kv_lat, kvn_ref[...]), wukv_ref[...])

    cos = cos_ref[...]
    sin = sin_ref[...]
    lane = lax.broadcasted_iota(jnp.int32, cos.shape, 1)
    first = (lane % QK_ROPE) < (QK_ROPE // 2)
    sin_signed = jnp.where(first, -sin, sin)

    k_rope = _rope(k_rope, cos, sin_signed, first)[:, :QK_ROPE].astype(k_ref.dtype)
    nope_w = MLA_HEADS * QK_NOPE
    for hp in range(MLA_HEADS // 2):
        q_rope = _rope(q[:, nope_w + hp * LANES: nope_w + (hp + 1) * LANES], cos, sin_signed, first)
        for hh in range(2):
            hd = 2 * hp + hh
            q_ref[hd, :, :QK_NOPE] = q[:, hd * QK_NOPE:(hd + 1) * QK_NOPE].astype(q_ref.dtype)
            q_ref[hd, :, QK_NOPE:] = q_rope[:, hh * QK_ROPE:(hh + 1) * QK_ROPE].astype(q_ref.dtype)
    for hd in range(MLA_HEADS):
        k_ref[hd, :, :QK_NOPE] = kv[:, hd * QK_NOPE:(hd + 1) * QK_NOPE].astype(k_ref.dtype)
        k_ref[hd, :, QK_NOPE:] = k_rope
        v_ref[hd, 0] = kv[:, nope_w + hd * V_DIM: nope_w + (hd + 1) * V_DIM].T.astype(v_ref.dtype)


def _mla_proj(x, cos, sin, ng, w_in, qn, w_uq, kvn, w_ukv):
    t, d = x.shape
    tm = MLA_TILE
    const = lambda shape: pl.BlockSpec(shape, lambda i: (0,) * len(shape))
    return pl.pallas_call(
        _mla_proj_kernel,
        grid=(t // tm,),
        in_specs=[pl.BlockSpec((tm, d), lambda i: (i, 0)),
                  pl.BlockSpec((tm, LANES), lambda i: (i, 0)),
                  pl.BlockSpec((tm, LANES), lambda i: (i, 0)),
                  const(ng.shape), const(w_in.shape), const(qn.shape), const(w_uq.shape),
                  const(kvn.shape), const(w_ukv.shape)],
        out_specs=[pl.BlockSpec((MLA_HEADS, tm, QK_DIM), lambda i: (0, i, 0)),
                   pl.BlockSpec((MLA_HEADS, tm, QK_DIM), lambda i: (0, i, 0)),
                   pl.BlockSpec((MLA_HEADS, 1, V_DIM, tm), lambda i: (0, i, 0, 0)),
                   pl.BlockSpec((tm, MLA_HEADS * V_DIM), lambda i: (i, 0))],
        out_shape=[jax.ShapeDtypeStruct((MLA_HEADS, t, QK_DIM), BF16),
                   jax.ShapeDtypeStruct((MLA_HEADS, t, QK_DIM), BF16),
                   jax.ShapeDtypeStruct((MLA_HEADS, t // tm, V_DIM, tm), BF16),
                   jax.ShapeDtypeStruct((t, MLA_HEADS * V_DIM), BF16)],
        compiler_params=_params("parallel"),
        name="mla_proj",
    )(x, cos, sin, ng, w_in, qn, w_uq, kvn, w_ukv)


def _attn_kernel(q_ref, k_ref, vt_ref, gate_ref, o_ref, st0_ref, st1_ref, acc_ref, *, blk, sub, nq):
    nsub = blk // sub

    def query_block(qi, carry):
        rows = pl.ds(pl.multiple_of(qi * blk, blk), blk)
        q = q_ref[0, rows, :]

        def scores(kb, st_ref):
            start = pl.multiple_of(kb * blk, blk)
            st = _dot_nt(k_ref[0, pl.ds(start, blk), :], q)
            st_ref[...] = st
            return jnp.max(st, axis=0, keepdims=True)

        def update(kb, st_ref, st_max, m, masked):
            vt = jnp.concatenate([vt_ref[0, kb * nsub + j] for j in range(nsub)], axis=1)
            vt = jnp.concatenate([vt, jnp.ones((ONES_ROWS, blk), vt.dtype)], axis=0)
            st = st_ref[...]
            if masked:
                key = lax.broadcasted_iota(jnp.int32, st.shape, 0)
                qry = lax.broadcasted_iota(jnp.int32, st.shape, 1)
                st = jnp.where(key <= qry, st, MASK_VALUE)
                st_max = jnp.max(st, axis=0, keepdims=True)
            m_new = jnp.maximum(m, st_max)
            pt = jnp.exp2((st - m_new).astype(BF16))
            corr = jnp.exp2(m - m_new)
            acc_ref[...] = acc_ref[...] * corr + _dot(vt, pt)
            return m_new

        acc_ref[...] = jnp.zeros_like(acc_ref)
        m_init = jnp.full((1, blk), MASK_VALUE, F32)

        def body(j, c):
            mx, m = c
            mx1 = scores(2 * j + 1, st1_ref)
            m = update(2 * j, st0_ref, mx, m, False)
            mx2 = scores(2 * j + 2, st0_ref)
            return mx2, update(2 * j + 1, st1_ref, mx1, m, False)

        npairs = qi // 2
        c = lax.fori_loop(0, npairs // 2, lambda j, c: body(2 * j + 1, body(2 * j, c)),
                          (scores(0, st0_ref), m_init))
        c = lax.cond(npairs % 2 == 1, lambda c: body(npairs - 1, c), lambda c: c, c)

        def tail_even(c):
            update(qi, st0_ref, c[0], c[1], True)

        def tail_odd(c):
            mx1 = scores(qi, st1_ref)
            m = update(qi - 1, st0_ref, c[0], c[1], False)
            update(qi, st1_ref, mx1, m, True)

        lax.cond(qi % 2 == 0, tail_even, tail_odd, c)
        gate = gate_ref[rows, :].astype(F32)
        out = acc_ref[:V_DIM, :] / acc_ref[V_DIM:V_DIM + 1, :]
        o_ref[rows, :] = (out.T * _silu(gate)).astype(o_ref.dtype)
        return carry

    lax.fori_loop(0, nq, query_block, 0)


def _attention(q, k, vt, gate, bsz, seq):
    blk = ATTN_TILE
    sub = vt.shape[3]
    nv = seq // sub
    return pl.pallas_call(
        functools.partial(_attn_kernel, blk=blk, sub=sub, nq=seq // blk),
        grid=(bsz, MLA_HEADS),
        in_specs=[pl.BlockSpec((1, seq, QK_DIM), lambda b, h: (h, b, 0)),
                  pl.BlockSpec((1, seq, QK_DIM), lambda b, h: (h, b, 0)),
                  pl.BlockSpec((1, nv, V_DIM, sub), lambda b, h: (h, b, 0, 0)),
                  pl.BlockSpec((seq, V_DIM), lambda b, h: (b, h))],
        out_specs=pl.BlockSpec((seq, V_DIM), lambda b, h: (b, h)),
        out_shape=jax.ShapeDtypeStruct((bsz * seq, MLA_HEADS * V_DIM), BF16),
        scratch_shapes=[pltpu.VMEM((blk, blk), F32), pltpu.VMEM((blk, blk), F32),
                        pltpu.VMEM((V_DIM + ONES_ROWS, blk), F32)],
        compiler_params=_params("parallel", "parallel"),
        name="mla_attention",
    )(q, k, vt, gate)


def _seg_sum(x, seg, segt):
    s = jnp.dot(x.astype(BF16), seg, preferred_element_type=F32)
    hi = s.astype(BF16)
    lo = (s - hi.astype(F32)).astype(BF16)
    return (jnp.dot(hi, segt, preferred_element_type=F32)
            + jnp.dot(lo, segt, preferred_element_type=F32))


def _finish(y, w_ref, x_ref, g_ref, o_ref, final_norm):
    x = x_ref[...] + jnp.dot(y.astype(BF16), w_ref[...], preferred_element_type=F32)
    if final_norm:
        x = _rms(x, g_ref[...])
    o_ref[...] = x


def _out_proj_kernel(y_ref, w_ref, x_ref, g_ref, o_ref, *, final_norm):
    _finish(y_ref[...], w_ref, x_ref, g_ref, o_ref, final_norm)


def _rwkv_out_kernel(y_ref, bv_ref, sg_ref, lng_ref, lnb_ref, seg_ref, segt_ref, w_ref, x_ref, g_ref,
                     o_ref, *, final_norm):
    y = y_ref[...].astype(F32)
    mean = _seg_sum(y, seg_ref[...], segt_ref[...]) * (1.0 / RWKV_HEAD)
    yc = y - mean
    var = _seg_sum(yc * yc, seg_ref[...], segt_ref[...]) * (1.0 / RWKV_HEAD)
    yn = yc * lax.rsqrt(var + GN_EPS) * lng_ref[...] + lnb_ref[...]
    out = (yn + bv_ref[...].astype(F32)) * sg_ref[...].astype(F32)
    _finish(out, w_ref, x_ref, g_ref, o_ref, final_norm)


def _out_proj(y, w, x, final_g, final_norm, gn=None):
    t, d = x.shape
    tm = OUT_TILE if gn is None else GN_TILE
    tile = lambda a: pl.BlockSpec((tm, a.shape[1]), lambda i: (i, 0))
    const = lambda a: pl.BlockSpec(a.shape, lambda i: (0, 0))
    if gn is None:
        body, ops = _out_proj_kernel, [y]
        specs = [tile(y)]
    else:
        bv, sg, lng, lnb, seg, segt = gn
        body, ops = _rwkv_out_kernel, [y, bv, sg, lng, lnb, seg, segt]
        specs = [tile(y), tile(bv), tile(sg), const(lng), const(lnb), const(seg), const(segt)]
    return pl.pallas_call(
        functools.partial(body, final_norm=final_norm),
        grid=(t // tm,),
        in_specs=specs + [const(w), tile(x), const(final_g)],
        out_specs=pl.BlockSpec((tm, d), lambda i: (i, 0)),
        out_shape=jax.ShapeDtypeStruct((t, d), F32),
        compiler_params=_params("parallel"),
        name="out_proj" if gn is None else "rwkv_out",
    )(*ops, w, x, final_g)


def _split3(x):
    hi = x.astype(BF16)
    r1 = x - hi.astype(F32)
    mid = r1.astype(BF16)
    lo = (r1 - mid.astype(F32)).astype(BF16)
    return hi, mid, lo


def _rwkv_proj_kernel(x_ref, ng_ref, win_ref, mu_ref, w0_ref, ww2_ref, a0_ref, wa2_ref,
                      kk_ref, ka_ref, rk_ref, tri_ref, seg_ref, segt_ref,
                      rt_ref, at_ref, bt_ref, kt_ref, bh_ref, kh_ref, v_ref, wl_ref, bv_ref, sg_ref,
                      prev_ref, *, width):
    @pl.when(pl.program_id(1) == 0)
    def _():
        prev_ref[...] = jnp.zeros_like(prev_ref)

    h = _rms(x_ref[...], ng_ref[...]).astype(BF16)
    rows = h.shape[0]
    nchunks = rows // CHUNK
    row8 = lax.broadcasted_iota(jnp.int32, (8, 1), 0)
    tri = tri_ref[...]

    def project(cols):
        proj = jnp.dot(h, win_ref[:, cols], preferred_element_type=F32)
        rolled = pltpu.roll(proj, 1, 0)
        shifted = jnp.concatenate([jnp.where(row8 == 0, prev_ref[0:1, cols], rolled[:8]), rolled[8:]],
                                  axis=0)
        prev_ref[0:1, cols] = proj[rows - 1:rows, :]
        return proj + mu_ref[:, cols] * (shifted - proj)

    lora = project(slice(4 * width, 4 * width + 2 * LORA))
    lane = lax.broadcasted_iota(jnp.int32, lora.shape, 1)
    lora = jnp.where(lane < LORA, jnp.tanh(lora), lora).astype(BF16)

    for cb in range(width // COL_BLOCK):
        cs = slice(cb * COL_BLOCK, (cb + 1) * COL_BLOCK)
        r, k, v, g = (project(slice(part * width + cs.start, part * width + cs.stop))
                      for part in range(4))
        sg_ref[:, cs] = _silu(g).astype(sg_ref.dtype)

        seg_sum = functools.partial(_seg_sum, seg=seg_ref[cs, :], segt=segt_ref[:, cs])

        z = w0_ref[:, cs] + jnp.dot(lora, ww2_ref[:, cs], preferred_element_type=F32)
        log_decay = (-LOG2E * EXP_M05) / (1.0 + jnp.exp(-z))
        a = 1.0 / (1.0 + jnp.exp(-(a0_ref[:, cs]
                                    + jnp.dot(lora, wa2_ref[:, cs], preferred_element_type=F32))))

        kk = k * kk_ref[:, cs]
        kk = kk * lax.rsqrt(jnp.maximum(seg_sum(kk * kk), KK_EPS ** 2))
        kp = k * (1.0 + (a - 1.0) * ka_ref[:, cs])
        b = kk * a
        bv_ref[:, cs] = (seg_sum(r * kp * rk_ref[:, cs]) * v).astype(bv_ref.dtype)

        parts = _split3(log_decay)
        cum = jnp.concatenate(
            [sum(jnp.dot(tri, part[sb:sb + TRI_ROWS], preferred_element_type=F32) for part in parts)
             for sb in range(0, rows, TRI_ROWS)], axis=0)
        last = [cum[(c + 1) * CHUNK - 1:(c + 1) * CHUNK, :] for c in range(nchunks)]
        cum_last = jnp.concatenate([jnp.broadcast_to(x, (CHUNK, COL_BLOCK)) for x in last], axis=0)
        wl_ref[:, cs] = jnp.concatenate([jnp.broadcast_to(jnp.exp2(x), (8, COL_BLOCK)) for x in last],
                                        axis=0)
        w_inv = jnp.exp2(-cum)
        w_tail = jnp.exp2(cum_last - cum)
        outs = ((rt_ref, r * jnp.exp2(cum)), (at_ref, -kk * jnp.exp2(cum - log_decay)),
                (bt_ref, b * w_inv), (kt_ref, kp * w_inv), (bh_ref, b * w_tail),
                (kh_ref, kp * w_tail), (v_ref, v))
        for ref, val in outs:
            for gj in range(COL_BLOCK // GW):
                ref[cb * (COL_BLOCK // GW) + gj] = val[:, gj * GW:(gj + 1) * GW].astype(ref.dtype)


def _rwkv_proj(x, ng, w_in, mu, w0, ww2, a0, wa2, kk, ka, rk, tri, seg, segt, bsz, seq):
    t, d = x.shape
    width = w0.shape[1]
    ngroups = width // GW
    tm = TOKEN_TILE
    ns = seq // tm
    const = lambda a: pl.BlockSpec(a.shape, lambda b, i: (0,) * a.ndim)
    consts = [ng, w_in, mu, w0, ww2, a0, wa2, kk, ka, rk, tri, seg, segt]
    gspec = pl.BlockSpec((ngroups, tm, GW), lambda b, i: (0, b * ns + i, 0))
    nspec = pl.BlockSpec((tm, width), lambda b, i: (b * ns + i, 0))
    wl_rows = tm // CHUNK * 8
    return pl.pallas_call(
        functools.partial(_rwkv_proj_kernel, width=width),
        grid=(bsz, ns),
        in_specs=[pl.BlockSpec((tm, d), lambda b, i: (b * ns + i, 0))] + [const(a) for a in consts],
        out_specs=[gspec] * 7 + [pl.BlockSpec((wl_rows, width), lambda b, i: (b * ns + i, 0)),
                                 nspec, nspec],
        out_shape=[jax.ShapeDtypeStruct((ngroups, t, GW), BF16)] * 7
        + [jax.ShapeDtypeStruct((t // CHUNK * 8, width), F32),
           jax.ShapeDtypeStruct((t, width), BF16), jax.ShapeDtypeStruct((t, width), BF16)],
        scratch_shapes=[pltpu.VMEM((8, w_in.shape[1]), F32)],
        compiler_params=_params("parallel", "arbitrary"),
        name="rwkv_proj",
    )(x, *consts)


def _block_diag(y, bd_mask):
    return jnp.where(bd_mask, jnp.concatenate([y.astype(BF16)] * GROUP_HEADS, axis=0), 0)


def _wkv_kernel(rt_ref, at_ref, bt_ref, kt_ref, bh_ref, kh_ref, v_ref, wl_ref, y_ref, state_ref, *,
                ngroups, nchunks):
    @pl.when(pl.program_id(1) == 0)
    def _():
        state_ref[...] = jnp.zeros_like(state_ref)

    t_idx = lax.broadcasted_iota(jnp.int32, (CHUNK, GW), 0)
    s_idx = lax.broadcasted_iota(jnp.int32, (CHUNK, GW), 1) % CHUNK
    incl = s_idx <= t_idx
    strict = s_idx < t_idx
    eye = (s_idx == t_idx).astype(F32)
    blk8 = strict & ((s_idx // 8) == (t_idx // 8))
    off_masks = [((s_idx // (2 * b)) == (t_idx // (2 * b))) & ((t_idx // b) % 2 == 1)
                 & ((s_idx // b) % 2 == 0) for b in (8, 16, 32)]
    bd_mask = (lax.broadcasted_iota(jnp.int32, (GW, GW), 0) // RWKV_HEAD
               == lax.broadcasted_iota(jnp.int32, (GW, GW), 1) // RWKV_HEAD)

    def bd(y):
        return _block_diag(y, bd_mask)

    def mm(x, y):
        return _dot(x, bd(y))

    gs = range(ngroups)
    chains = [(c, gi) for c in range(nchunks) for gi in gs]
    rows = [slice(c * CHUNK, (c + 1) * CHUNK) for c, _ in chains]
    r_t = [rt_ref[gi, rw, :] for (_, gi), rw in zip(chains, rows)]
    a_t = [at_ref[gi, rw, :] for (_, gi), rw in zip(chains, rows)]
    v = [v_ref[gi, rw, :] for (_, gi), rw in zip(chains, rows)]

    amat = [_dot_nt(jnp.concatenate([r_t[n], a_t[n]], axis=0),
                    jnp.concatenate([bd(bt_ref[gi, rows[n], :]), bd(kt_ref[gi, rows[n], :])], axis=0))
            for n, (_, gi) in enumerate(chains)]
    a_rb = [jnp.where(incl, x[:CHUNK, :GW], 0.0) for x in amat]
    a_rk = [jnp.where(incl, x[:CHUNK, GW:], 0.0) for x in amat]
    m_ab = [jnp.where(strict, x[CHUNK:, :GW], 0.0) for x in amat]
    a_ak = [jnp.where(strict, x[CHUNK:, GW:], 0.0) for x in amat]

    n1 = [jnp.where(blk8, x, 0.0) for x in m_ab]
    n2 = [mm(x, x) for x in n1]
    n4 = [mm(x, x) for x in n2]
    tmat = [eye + x for x in n1]
    tmat = [x + mm(x, y) for x, y in zip(tmat, n2)]
    tmat = [x + mm(x, y) for x, y in zip(tmat, n4)]
    for off in off_masks:
        z = [mm(jnp.where(off, m, 0.0), x) for m, x in zip(m_ab, tmat)]
        tmat = [x + mm(x, y) for x, y in zip(tmat, z)]

    akv = [mm(x, y) for x, y in zip(a_ak, v)]
    pq = [_dot(tmat[n], jnp.concatenate([bd(a_t[n]), bd(akv[n])], axis=1))
          for n in range(len(chains))]

    state = [state_ref[gi] for gi in gs]
    for c in range(nchunks):
        ns = [c * ngroups + gi for gi in gs]
        us = [_dot_nt(jnp.concatenate([pq[n][:, :GW].astype(BF16), r_t[n]], axis=0), state[gi])
              for gi, n in zip(gs, ns)]
        u = [us[gi][:CHUNK] + pq[n][:, GW:] for gi, n in zip(gs, ns)]
        y = [us[gi][CHUNK:]
             + _dot(jnp.concatenate([a_rb[n], a_rk[n]], axis=1),
                    jnp.concatenate([bd(u[gi]), bd(v[n])], axis=0)) for gi, n in zip(gs, ns)]
        upd = [_dot(jnp.concatenate([u[gi], v[n].astype(F32)], axis=0).T,
                    jnp.concatenate([bh_ref[gi, rows[n], :], kh_ref[gi, rows[n], :]], axis=0))
               for gi, n in zip(gs, ns)]
        wl = wl_ref[c * 8:c * 8 + 1, :]
        state = [state[gi] * wl[:, gi * GW:(gi + 1) * GW] + jnp.where(bd_mask, upd[gi], 0.0)
                 for gi in gs]
        for gi in gs:
            y_ref[rows[ns[gi]], gi * GW:(gi + 1) * GW] = y[gi].astype(y_ref.dtype)
    for gi in gs:
        state_ref[gi] = state[gi]


def _wkv(rt, at, bt, kt, bh, kh, v, wl, bsz, seq):
    ngroups, t, _ = rt.shape
    tt = WKV_TILE
    ns = seq // tt
    gspec = pl.BlockSpec((ngroups, tt, GW), lambda b, i: (0, b * ns + i, 0))
    return pl.pallas_call(
        functools.partial(_wkv_kernel, ngroups=ngroups, nchunks=tt // CHUNK),
        grid=(bsz, ns),
        in_specs=[gspec] * 7 + [pl.BlockSpec((tt // CHUNK * 8, ngroups * GW),
                                             lambda b, i: (b * ns + i, 0))],
        out_specs=pl.BlockSpec((tt, ngroups * GW), lambda b, i: (b * ns + i, 0)),
        out_shape=jax.ShapeDtypeStruct((t, ngroups * GW), BF16),
        scratch_shapes=[pltpu.VMEM((ngroups, GW, GW), F32)],
        compiler_params=_params("parallel", "arbitrary"),
        name="rwkv_wkv",
    )(rt, at, bt, kt, bh, kh, v, wl)


def _row(v):
    return v.reshape(1, -1).astype(F32)


def kernel(x, positions, norm_g, mla_w_in, mla_q_norm, mla_w_uq, mla_kv_norm, mla_w_ukv, mla_w_out, rwkv_w_in, rwkv_mu, rwkv_w0, rwkv_w_w2, rwkv_a0, rwkv_w_a2, rwkv_k_k, rwkv_k_a, rwkv_r_k, rwkv_ln_g, rwkv_ln_b, rwkv_w_out, final_g):
    bsz, seq, d = x.shape
    depth = norm_g.shape[0]
    t = bsz * seq
    xf = x.reshape(t, d)
    half = QK_ROPE // 2
    inv_freq = 1.0 / (ROPE_THETA ** (jnp.arange(half, dtype=F32) / half))
    ang = positions.reshape(t, 1).astype(F32) * inv_freq[None, :]
    cos = jnp.tile(jnp.cos(ang), (1, LANES // half))
    sin = jnp.tile(jnp.sin(ang), (1, LANES // half))
    final_row = _row(final_g)

    width = rwkv_w0.shape[1]
    tok = jnp.arange(TRI_ROWS)
    tri = ((tok[:, None] // CHUNK == tok[None, :] // CHUNK) & (tok[None, :] <= tok[:, None])).astype(BF16)
    seg = (jnp.arange(width)[:, None] // RWKV_HEAD == jnp.arange(LANES)[None, :]).astype(BF16)
    segt = seg.T

    for i in range(depth):
        j = i // 2
        ng = _row(norm_g[i])
        last = i == depth - 1
        if i % 2 == 0:
            w_in = mla_w_in[j]
            lat = Q_LORA + KV_LORA
            w_in = jnp.concatenate(
                [w_in[:, :lat], w_in[:, lat + QK_ROPE:], w_in[:, lat:lat + QK_ROPE],
                 jnp.zeros((d, LANES - QK_ROPE), w_in.dtype)], axis=1).astype(BF16)
            w_uq = mla_w_uq[j].reshape(Q_LORA, MLA_HEADS, QK_DIM)
            w_uq = jnp.concatenate([w_uq[:, :, :QK_NOPE].reshape(Q_LORA, -1),
                                    w_uq[:, :, QK_NOPE:].reshape(Q_LORA, -1)], axis=1).astype(BF16)
            w_ukv = mla_w_ukv[j].reshape(KV_LORA, MLA_HEADS, QK_NOPE + V_DIM)
            w_ukv = jnp.concatenate([w_ukv[:, :, :QK_NOPE].reshape(KV_LORA, -1),
                                     w_ukv[:, :, QK_NOPE:].reshape(KV_LORA, -1)], axis=1).astype(BF16)
            q, k, vt, gate = _mla_proj(xf, cos, sin, ng, w_in, _row(mla_q_norm[j]), w_uq,
                                       _row(mla_kv_norm[j]), w_ukv)
            y = _attention(q, k, vt, gate, bsz, seq)
            xf = _out_proj(y, mla_w_out[j].astype(BF16), xf, final_row, last)
        else:
            zeros = jnp.zeros((LORA, width), F32)
            ww2 = jnp.concatenate([rwkv_w_w2[j], zeros], axis=0).astype(BF16)
            wa2 = jnp.concatenate([zeros, rwkv_w_a2[j]], axis=0).astype(BF16)
            *ops, wl, bv, sg = _rwkv_proj(
                xf, ng, rwkv_w_in[j].astype(BF16), _row(rwkv_mu[j]), _row(rwkv_w0[j]), ww2,
                _row(rwkv_a0[j]), wa2, _row(rwkv_k_k[j]), _row(rwkv_k_a[j]), _row(rwkv_r_k[j]),
                tri, seg, segt, bsz, seq)
            y = _wkv(*ops, wl, bsz, seq)
            gn = (bv, sg, _row(rwkv_ln_g[j]), _row(rwkv_ln_b[j]), seg, segt)
            xf = _out_proj(y, rwkv_w_out[j].astype(BF16), xf, final_row, last, gn)
    return xf.reshape(bsz, seq, d)
```

```python
import functools

import jax
import jax.numpy as jnp
from jax import lax
from jax.experimental import pallas as pl
from jax.experimental.pallas import tpu as pltpu

F32 = jnp.float32
BF16 = jnp.bfloat16

NORM_EPS = 1e-6
GN_EPS = 64e-5
KK_EPS = 1e-12
ROPE_THETA = 10000.0

MLA_HEADS = 8
QK_NOPE = 128
QK_ROPE = 64
QK_DIM = QK_NOPE + QK_ROPE
V_DIM = 128
Q_LORA = 768
KV_LORA = 256

RWKV_HEAD = 64
LORA = 64

LANES = 128
CHUNK = 64
GROUP_HEADS = 2
GW = GROUP_HEADS * RWKV_HEAD

TOKEN_TILE = 512
GN_TILE = 256
TRI_ROWS = 256
MLA_TILE = 1024
OUT_TILE = 1024
ATTN_TILE = 1024
COL_BLOCK = 256
WKV_TILE = 512
VMEM_LIMIT = 56 * 1024 * 1024
MASK_VALUE = -1e30
ONES_ROWS = 16
LOG2E = 1.4426950408889634
EXP_M05 = 0.6065306597126334


def _params(*sem):
    return pltpu.CompilerParams(dimension_semantics=sem, vmem_limit_bytes=VMEM_LIMIT)


def _rms(x, g):
    return x * lax.rsqrt(jnp.mean(x * x, axis=-1, keepdims=True) + NORM_EPS) * g


def _dot(a, b):
    return jnp.dot(a.astype(BF16), b.astype(BF16), preferred_element_type=F32)


def _dot_nt(a, b):
    return lax.dot_general(a.astype(BF16), b.astype(BF16), (((1,), (1,)), ((), ())),
                           preferred_element_type=F32)


def _silu(x):
    return x / (1.0 + jnp.exp(-x))


def _rope(t, cos, sin_signed, first):
    partner = jnp.where(first, pltpu.roll(t, LANES - QK_ROPE // 2, 1), pltpu.roll(t, QK_ROPE // 2, 1))
    return t * cos + partner * sin_signed


def _mla_proj_kernel(x_ref, cos_ref, sin_ref, ng_ref, win_ref, qn_ref, wuq_ref, kvn_ref, wukv_ref,
                     q_ref, k_ref, v_ref, gate_ref):
    x = x_ref[...]
    h = _rms(x, ng_ref[...])
    proj = _dot(h, win_ref[...])
    q_lat = proj[:, :Q_LORA]
    kv_lat = proj[:, Q_LORA:Q_LORA + KV_LORA]
    lat = Q_LORA + KV_LORA
    gate_w = MLA_HEADS * V_DIM
    gate_ref[...] = proj[:, lat:lat + gate_w].astype(gate_ref.dtype)
    k_rope = proj[:, lat + gate_w:lat + gate_w + LANES]

    q = _dot(_rms(q_lat, qn_ref[...]), wuq_ref[...]) * (QK_DIM ** -0.5 * LOG2E)
    kv = _dot(_rms(kv_lat, kvn_ref[...]), wukv_ref[...])

    cos = cos_ref[...]
    sin = sin_ref[...]
    lane = lax.broadcasted_iota(jnp.int32, cos.shape, 1)
    first = (lane % QK_ROPE) < (QK_ROPE // 2)
    sin_signed = jnp.where(first, -sin, sin)

    k_rope = _rope(k_rope, cos, sin_signed, first)[:, :QK_ROPE].astype(k_ref.dtype)
    nope_w = MLA_HEADS * QK_NOPE
    for hp in range(MLA_HEADS // 2):
        q_rope = _rope(q[:, nope_w + hp * LANES: nope_w + (hp + 1) * LANES], cos, sin_signed, first)
        for hh in range(2):
            hd = 2 * hp + hh
            q_ref[hd, :, :QK_NOPE] = q[:, hd * QK_NOPE:(hd + 1) * QK_NOPE].astype(q_ref.dtype)
            q_ref[hd, :, QK_NOPE:] = q_rope[:, hh * QK_ROPE:(hh + 1) * QK_ROPE].astype(q_ref.dtype)
    for hd in range(MLA_HEADS):
        k_ref[hd, :, :QK_NOPE] = kv[:, hd * QK_NOPE:(hd + 1) * QK_NOPE].astype(k_ref.dtype)
        k_ref[hd, :, QK_NOPE:] = k_rope
        v_ref[hd, 0] = kv[:, nope_w + hd * V_DIM: nope_w + (hd + 1) * V_DIM].T.astype(v_ref.dtype)


def _mla_proj(x, cos, sin, ng, w_in, qn, w_uq, kvn, w_ukv):
    t, d = x.shape
    tm = MLA_TILE
    const = lambda shape: pl.BlockSpec(shape, lambda i: (0,) * len(shape))
    return pl.pallas_call(
        _mla_proj_kernel,
        grid=(t // tm,),
        in_specs=[pl.BlockSpec((tm, d), lambda i: (i, 0)),
                  pl.BlockSpec((tm, LANES), lambda i: (i, 0)),
                  pl.BlockSpec((tm, LANES), lambda i: (i, 0)),
                  const(ng.shape), const(w_in.shape), const(qn.shape), const(w_uq.shape),
                  const(kvn.shape), const(w_ukv.shape)],
        out_specs=[pl.BlockSpec((MLA_HEADS, tm, QK_DIM), lambda i: (0, i, 0)),
                   pl.BlockSpec((MLA_HEADS, tm, QK_DIM), lambda i: (0, i, 0)),
                   pl.BlockSpec((MLA_HEADS, 1, V_DIM, tm), lambda i: (0, i, 0, 0)),
                   pl.BlockSpec((tm, MLA_HEADS * V_DIM), lambda i: (i, 0))],
        out_shape=[jax.ShapeDtypeStruct((MLA_HEADS, t, QK_DIM), BF16),
                   jax.ShapeDtypeStruct((MLA_HEADS, t, QK_DIM), BF16),
                   jax.ShapeDtypeStruct((MLA_HEADS, t // tm, V_DIM, tm), BF16),
                   jax.ShapeDtypeStruct((t, MLA_HEADS * V_DIM), BF16)],
        compiler_params=_params("parallel"),
        name="mla_proj",
    )(x, cos, sin, ng, w_in, qn, w_uq, kvn, w_ukv)


def _attn_kernel(q_ref, k_ref, vt_ref, gate_ref, o_ref, st0_ref, st1_ref, st2_ref, acc_ref, *,
                 blk, sub, nq):
    nsub = blk // sub

    def block0_scores(q_rows):
        st = _dot_nt(k_ref[0, pl.ds(0, blk), :], q_ref[0, q_rows, :])
        st2_ref[...] = st
        return jnp.max(st, axis=0, keepdims=True)

    def query_block(qi, mx0):
        rows = pl.ds(pl.multiple_of(qi * blk, blk), blk)
        q = q_ref[0, rows, :]

        def scores(kb, st_ref):
            start = pl.multiple_of(kb * blk, blk)
            st = _dot_nt(k_ref[0, pl.ds(start, blk), :], q)
            st_ref[...] = st
            return jnp.max(st, axis=0, keepdims=True)

        def update(kb, st_ref, st_max, m, masked):
            vt = jnp.concatenate([vt_ref[0, kb * nsub + j] for j in range(nsub)], axis=1)
            vt = jnp.concatenate([vt, jnp.ones((ONES_ROWS, blk), vt.dtype)], axis=0)
            st = st_ref[...]
            if masked:
                key = lax.broadcasted_iota(jnp.int32, st.shape, 0)
                qry = lax.broadcasted_iota(jnp.int32, st.shape, 1)
                st = jnp.where(key <= qry, st, MASK_VALUE)
                st_max = jnp.max(st, axis=0, keepdims=True)
            m_new = jnp.maximum(m, st_max)
            pt = jnp.exp2((st - m_new).astype(BF16))
            corr = jnp.exp2(m - m_new)
            acc_ref[...] = acc_ref[...] * corr + _dot(vt, pt)
            return m_new

        acc_ref[...] = jnp.zeros_like(acc_ref)
        m_init = jnp.full((1, blk), MASK_VALUE, F32)

        def finish(st_ref, mx, m):
            update(qi, st_ref, mx, m, True)
            nxt = pl.ds(pl.multiple_of(jnp.minimum(qi + 1, nq - 1) * blk, blk), blk)
            return block0_scores(nxt)

        def body(j, c):
            mx, m = c
            mx1 = scores(2 * j + 2, st1_ref)
            m = update(2 * j + 1, st0_ref, mx, m, False)
            mx2 = scores(2 * j + 3, st0_ref)
            return mx2, update(2 * j + 2, st1_ref, mx1, m, False)

        def with_prefix(_):
            mx = scores(1, st0_ref)
            c = (mx, update(0, st2_ref, mx0, m_init, False))
            npairs = (qi - 1) // 2
            c = lax.fori_loop(0, npairs // 2, lambda j, c: body(2 * j + 1, body(2 * j, c)), c)
            c = lax.cond(npairs % 2 == 1, lambda c: body(npairs - 1, c), lambda c: c, c)

            def tail_odd(c):
                mx1 = scores(qi, st1_ref)
                return finish(st1_ref, mx1, update(qi - 1, st0_ref, c[0], c[1], False))

            return lax.cond((qi - 1) % 2 == 0, lambda c: finish(st0_ref, c[0], c[1]), tail_odd, c)

        mx_next = lax.cond(qi == 0, lambda _: finish(st2_ref, mx0, m_init), with_prefix, 0)
        gate = gate_ref[rows, :].astype(F32)
        out = acc_ref[:V_DIM, :] / acc_ref[V_DIM:V_DIM + 1, :]
        o_ref[rows, :] = (out.T * _silu(gate)).astype(o_ref.dtype)
        return mx_next

    lax.fori_loop(0, nq, query_block, block0_scores(pl.ds(0, blk)))


def _attention(q, k, vt, gate, bsz, seq):
    blk = ATTN_TILE
    sub = vt.shape[3]
    nv = seq // sub
    return pl.pallas_call(
        functools.partial(_attn_kernel, blk=blk, sub=sub, nq=seq // blk),
        grid=(bsz, MLA_HEADS),
        in_specs=[pl.BlockSpec((1, seq, QK_DIM), lambda b, h: (h, b, 0)),
                  pl.BlockSpec((1, seq, QK_DIM), lambda b, h: (h, b, 0)),
                  pl.BlockSpec((1, nv, V_DIM, sub), lambda b, h: (h, b, 0, 0)),
                  pl.BlockSpec((seq, V_DIM), lambda b, h: (b, h))],
        out_specs=pl.BlockSpec((seq, V_DIM), lambda b, h: (b, h)),
        out_shape=jax.ShapeDtypeStruct((bsz * seq, MLA_HEADS * V_DIM), BF16),
        scratch_shapes=[pltpu.VMEM((blk, blk), F32)] * 3 + [pltpu.VMEM((V_DIM + ONES_ROWS, blk), F32)],
        compiler_params=_params("parallel", "parallel"),
        name="mla_attention",
    )(q, k, vt, gate)


def _seg_sum(x, seg, segt):
    s = jnp.dot(x.astype(BF16), seg, preferred_element_type=F32)
    hi = s.astype(BF16)
    lo = (s - hi.astype(F32)).astype(BF16)
    return (jnp.dot(hi, segt, preferred_element_type=F32)
            + jnp.dot(lo, segt, preferred_element_type=F32))


def _finish(y, w_ref, x_ref, g_ref, o_ref, final_norm):
    x = x_ref[...] + jnp.dot(y.astype(BF16), w_ref[...], preferred_element_type=F32)
    if final_norm:
        x = _rms(x, g_ref[...])
    o_ref[...] = x


def _out_proj_kernel(y_ref, w_ref, x_ref, g_ref, o_ref, *, final_norm):
    _finish(y_ref[...], w_ref, x_ref, g_ref, o_ref, final_norm)


def _rwkv_out_kernel(y_ref, bv_ref, sg_ref, lng_ref, lnb_ref, seg_ref, segt_ref, w_ref, x_ref, g_ref,
                     o_ref, *, final_norm):
    y = y_ref[...].astype(F32)
    mean = _seg_sum(y, seg_ref[...], segt_ref[...]) * (1.0 / RWKV_HEAD)
    yc = y - mean
    var = _seg_sum(yc * yc, seg_ref[...], segt_ref[...]) * (1.0 / RWKV_HEAD)
    yn = yc * lax.rsqrt(var + GN_EPS) * lng_ref[...] + lnb_ref[...]
    out = (yn + bv_ref[...].astype(F32)) * sg_ref[...].astype(F32)
    _finish(out, w_ref, x_ref, g_ref, o_ref, final_norm)


def _out_proj(y, w, x, final_g, final_norm, gn=None):
    t, d = x.shape
    tm = OUT_TILE if gn is None else GN_TILE
    tile = lambda a: pl.BlockSpec((tm, a.shape[1]), lambda i: (i, 0))
    const = lambda a: pl.BlockSpec(a.shape, lambda i: (0, 0))
    if gn is None:
        body, ops = _out_proj_kernel, [y]
        specs = [tile(y)]
    else:
        bv, sg, lng, lnb, seg, segt = gn
        body, ops = _rwkv_out_kernel, [y, bv, sg, lng, lnb, seg, segt]
        specs = [tile(y), tile(bv), tile(sg), const(lng), const(lnb), const(seg), const(segt)]
    return pl.pallas_call(
        functools.partial(body, final_norm=final_norm),
        grid=(t // tm,),
        in_specs=specs + [const(w), tile(x), const(final_g)],
        out_specs=pl.BlockSpec((tm, d), lambda i: (i, 0)),
        out_shape=jax.ShapeDtypeStruct((t, d), F32),
        compiler_params=_params("parallel"),
        name="out_proj" if gn is None else "rwkv_out",
    )(*ops, w, x, final_g)


def _split3(x):
    hi = x.astype(BF16)
    r1 = x - hi.astype(F32)
    mid = r1.astype(BF16)
    lo = (r1 - mid.astype(F32)).astype(BF16)
    return hi, mid, lo


def _rwkv_proj_kernel(x_ref, ng_ref, win_ref, mu_ref, w0_ref, ww2_ref, a0_ref, wa2_ref,
                      kk_ref, ka_ref, rk_ref, tri_ref, seg_ref, segt_ref,
                      rt_ref, at_ref, bt_ref, kt_ref, bh_ref, kh_ref, v_ref, wl_ref, bv_ref, sg_ref,
                      prev_ref, *, width):
    @pl.when(pl.program_id(1) == 0)
    def _():
        prev_ref[...] = jnp.zeros_like(prev_ref)

    h = _rms(x_ref[...], ng_ref[...]).astype(BF16)
    rows = h.shape[0]
    nchunks = rows // CHUNK
    row8 = lax.broadcasted_iota(jnp.int32, (8, 1), 0)
    tri = tri_ref[...]

    def project(cols):
        proj = jnp.dot(h, win_ref[:, cols], preferred_element_type=F32)
        rolled = pltpu.roll(proj, 1, 0)
        shifted = jnp.concatenate([jnp.where(row8 == 0, prev_ref[0:1, cols], rolled[:8]), rolled[8:]],
                                  axis=0)
        prev_ref[0:1, cols] = proj[rows - 1:rows, :]
        return proj + mu_ref[:, cols] * (shifted - proj)

    lora = project(slice(4 * width, 4 * width + 2 * LORA))
    lane = lax.broadcasted_iota(jnp.int32, lora.shape, 1)
    lora = jnp.where(lane < LORA, jnp.tanh(lora), lora).astype(BF16)

    for cb in range(width // COL_BLOCK):
        cs = slice(cb * COL_BLOCK, (cb + 1) * COL_BLOCK)
        r, k, v, g = (project(slice(part * width + cs.start, part * width + cs.stop))
                      for part in range(4))
        sg_ref[:, cs] = _silu(g).astype(sg_ref.dtype)

        seg_sum = functools.partial(_seg_sum, seg=seg_ref[cs, :], segt=segt_ref[:, cs])

        z = w0_ref[:, cs] + jnp.dot(lora, ww2_ref[:, cs], preferred_element_type=F32)
        log_decay = (-LOG2E * EXP_M05) / (1.0 + jnp.exp(-z))
        a = 1.0 / (1.0 + jnp.exp(-(a0_ref[:, cs]
                                    + jnp.dot(lora, wa2_ref[:, cs], preferred_element_type=F32))))

        kk = k * kk_ref[:, cs]
        kk = kk * lax.rsqrt(jnp.maximum(seg_sum(kk * kk), KK_EPS ** 2))
        kp = k * (1.0 + (a - 1.0) * ka_ref[:, cs])
        b = kk * a
        bv_ref[:, cs] = (seg_sum(r * kp * rk_ref[:, cs]) * v).astype(bv_ref.dtype)

        parts = _split3(log_decay)
        cum = jnp.concatenate(
            [sum(jnp.dot(tri, part[sb:sb + TRI_ROWS], preferred_element_type=F32) for part in parts)
             for sb in range(0, rows, TRI_ROWS)], axis=0)
        last = [cum[(c + 1) * CHUNK - 1:(c + 1) * CHUNK, :] for c in range(nchunks)]
        cum_last = jnp.concatenate([jnp.broadcast_to(x, (CHUNK, COL_BLOCK)) for x in last], axis=0)
        wl_ref[:, cs] = jnp.concatenate([jnp.broadcast_to(jnp.exp2(x), (8, COL_BLOCK)) for x in last],
                                        axis=0)
        w_inv = jnp.exp2(-cum)
        w_tail = jnp.exp2(cum_last - cum)
        outs = ((rt_ref, r * jnp.exp2(cum)), (at_ref, -kk * jnp.exp2(cum - log_decay)),
                (bt_ref, b * w_inv), (kt_ref, kp * w_inv), (bh_ref, b * w_tail),
                (kh_ref, kp * w_tail), (v_ref, v))
        for ref, val in outs:
            for gj in range(COL_BLOCK // GW):
                ref[cb * (COL_BLOCK // GW) + gj] = val[:, gj * GW:(gj + 1) * GW].astype(ref.dtype)


def _rwkv_proj(x, ng, w_in, mu, w0, ww2, a0, wa2, kk, ka, rk, tri, seg, segt, bsz, seq):
    t, d = x.shape
    width = w0.shape[1]
    ngroups = width // GW
    tm = TOKEN_TILE
    ns = seq // tm
    const = lambda a: pl.BlockSpec(a.shape, lambda b, i: (0,) * a.ndim)
    consts = [ng, w_in, mu, w0, ww2, a0, wa2, kk, ka, rk, tri, seg, segt]
    gspec = pl.BlockSpec((ngroups, tm, GW), lambda b, i: (0, b * ns + i, 0))
    nspec = pl.BlockSpec((tm, width), lambda b, i: (b * ns + i, 0))
    wl_rows = tm // CHUNK * 8
    return pl.pallas_call(
        functools.partial(_rwkv_proj_kernel, width=width),
        grid=(bsz, ns),
        in_specs=[pl.BlockSpec((tm, d), lambda b, i: (b * ns + i, 0))] + [const(a) for a in consts],
        out_specs=[gspec] * 7 + [pl.BlockSpec((wl_rows, width), lambda b, i: (b * ns + i, 0)),
                                 nspec, nspec],
        out_shape=[jax.ShapeDtypeStruct((ngroups, t, GW), BF16)] * 7
        + [jax.ShapeDtypeStruct((t // CHUNK * 8, width), F32),
           jax.ShapeDtypeStruct((t, width), BF16), jax.ShapeDtypeStruct((t, width), BF16)],
        scratch_shapes=[pltpu.VMEM((8, w_in.shape[1]), F32)],
        compiler_params=_params("parallel", "arbitrary"),
        name="rwkv_proj",
    )(x, *consts)


def _block_diag(y, bd_mask):
    return jnp.where(bd_mask, jnp.concatenate([y.astype(BF16)] * GROUP_HEADS, axis=0), 0)


def _wkv_kernel(rt_ref, at_ref, bt_ref, kt_ref, bh_ref, kh_ref, v_ref, wl_ref, y_ref, state_ref, *,
                ngroups, nchunks):
    @pl.when(pl.program_id(1) == 0)
    def _():
        state_ref[...] = jnp.zeros_like(state_ref)

    t_idx = lax.broadcasted_iota(jnp.int32, (CHUNK, GW), 0)
    s_idx = lax.broadcasted_iota(jnp.int32, (CHUNK, GW), 1) % CHUNK
    incl = s_idx <= t_idx
    strict = s_idx < t_idx
    eye = (s_idx == t_idx).astype(F32)
    blk8 = strict & ((s_idx // 8) == (t_idx // 8))
    off_masks = [((s_idx // (2 * b)) == (t_idx // (2 * b))) & ((t_idx // b) % 2 == 1)
                 & ((s_idx // b) % 2 == 0) for b in (8, 16, 32)]
    bd_mask = (lax.broadcasted_iota(jnp.int32, (GW, GW), 0) // RWKV_HEAD
               == lax.broadcasted_iota(jnp.int32, (GW, GW), 1) // RWKV_HEAD)

    def bd(y):
        return _block_diag(y, bd_mask)

    def mm(x, y):
        return _dot(x, bd(y))

    gs = range(ngroups)
    chains = [(c, gi) for c in range(nchunks) for gi in gs]
    rows = [slice(c * CHUNK, (c + 1) * CHUNK) for c, _ in chains]
    r_t = [rt_ref[gi, rw, :] for (_, gi), rw in zip(chains, rows)]
    a_t = [at_ref[gi, rw, :] for (_, gi), rw in zip(chains, rows)]
    v = [v_ref[gi, rw, :] for (_, gi), rw in zip(chains, rows)]

    amat = [_dot_nt(jnp.concatenate([r_t[n], a_t[n]], axis=0),
                    jnp.concatenate([bd(bt_ref[gi, rows[n], :]), bd(kt_ref[gi, rows[n], :])], axis=0))
            for n, (_, gi) in enumerate(chains)]
    a_rb = [jnp.where(incl, x[:CHUNK, :GW], 0.0) for x in amat]
    a_rk = [jnp.where(incl, x[:CHUNK, GW:], 0.0) for x in amat]
    m_ab = [jnp.where(strict, x[CHUNK:, :GW], 0.0) for x in amat]
    a_ak = [jnp.where(strict, x[CHUNK:, GW:], 0.0) for x in amat]

    n1 = [jnp.where(blk8, x, 0.0) for x in m_ab]
    n2 = [mm(x, x) for x in n1]
    n4 = [mm(x, x) for x in n2]
    tmat = [eye + x for x in n1]
    tmat = [x + mm(x, y) for x, y in zip(tmat, n2)]
    tmat = [x + mm(x, y) for x, y in zip(tmat, n4)]
    for off in off_masks:
        z = [mm(jnp.where(off, m, 0.0), x) for m, x in zip(m_ab, tmat)]
        tmat = [x + mm(x, y) for x, y in zip(tmat, z)]

    akv = [mm(x, y) for x, y in zip(a_ak, v)]
    pq = [_dot(tmat[n], jnp.concatenate([bd(a_t[n]), bd(akv[n])], axis=1))
          for n in range(len(chains))]

    state = [state_ref[gi] for gi in gs]
    for c in range(nchunks):
        ns = [c * ngroups + gi for gi in gs]
        us = [_dot_nt(jnp.concatenate([pq[n][:, :GW].astype(BF16), r_t[n]], axis=0), state[gi])
              for gi, n in zip(gs, ns)]
        u = [us[gi][:CHUNK] + pq[n][:, GW:] for gi, n in zip(gs, ns)]
        y = [us[gi][CHUNK:]
             + _dot(jnp.concatenate([a_rb[n], a_rk[n]], axis=1),
                    jnp.concatenate([bd(u[gi]), bd(v[n])], axis=0)) for gi, n in zip(gs, ns)]
        upd = [_dot(jnp.concatenate([u[gi], v[n].astype(F32)], axis=0).T,
                    jnp.concatenate([bh_ref[gi, rows[n], :], kh_ref[gi, rows[n], :]], axis=0))
               for gi, n in zip(gs, ns)]
        wl = wl_ref[c * 8:c * 8 + 1, :]
        state = [state[gi] * wl[:, gi * GW:(gi + 1) * GW] + jnp.where(bd_mask, upd[gi], 0.0)
                 for gi in gs]
        for gi in gs:
            y_ref[rows[ns[gi]], gi * GW:(gi + 1) * GW] = y[gi].astype(y_ref.dtype)
    for gi in gs:
        state_ref[gi] = state[gi]


def _wkv(rt, at, bt, kt, bh, kh, v, wl, bsz, seq):
    ngroups, t, _ = rt.shape
    tt = WKV_TILE
    ns = seq // tt
    gspec = pl.BlockSpec((ngroups, tt, GW), lambda b, i: (0, b * ns + i, 0))
    return pl.pallas_call(
        functools.partial(_wkv_kernel, ngroups=ngroups, nchunks=tt // CHUNK),
        grid=(bsz, ns),
        in_specs=[gspec] * 7 + [pl.BlockSpec((tt // CHUNK * 8, ngroups * GW),
                                             lambda b, i: (b * ns + i, 0))],
        out_specs=pl.BlockSpec((tt, ngroups * GW), lambda b, i: (b * ns + i, 0)),
        out_shape=jax.ShapeDtypeStruct((t, ngroups * GW), BF16),
        scratch_shapes=[pltpu.VMEM((ngroups, GW, GW), F32)],
        compiler_params=_params("parallel", "arbitrary"),
        name="rwkv_wkv",
    )(rt, at, bt, kt, bh, kh, v, wl)


def _row(v):
    return v.reshape(1, -1).astype(F32)


def kernel(x, positions, norm_g, mla_w_in, mla_q_norm, mla_w_uq, mla_kv_norm, mla_w_ukv, mla_w_out, rwkv_w_in, rwkv_mu, rwkv_w0, rwkv_w_w2, rwkv_a0, rwkv_w_a2, rwkv_k_k, rwkv_k_a, rwkv_r_k, rwkv_ln_g, rwkv_ln_b, rwkv_w_out, final_g):
    bsz, seq, d = x.shape
    depth = norm_g.shape[0]
    t = bsz * seq
    xf = x.reshape(t, d)
    half = QK_ROPE // 2
    inv_freq = 1.0 / (ROPE_THETA ** (jnp.arange(half, dtype=F32) / half))
    ang = positions.reshape(t, 1).astype(F32) * inv_freq[None, :]
    cos = jnp.tile(jnp.cos(ang), (1, LANES // half))
    sin = jnp.tile(jnp.sin(ang), (1, LANES // half))
    final_row = _row(final_g)

    width = rwkv_w0.shape[1]
    tok = jnp.arange(TRI_ROWS)
    tri = ((tok[:, None] // CHUNK == tok[None, :] // CHUNK) & (tok[None, :] <= tok[:, None])).astype(BF16)
    seg = (jnp.arange(width)[:, None] // RWKV_HEAD == jnp.arange(LANES)[None, :]).astype(BF16)
    segt = seg.T

    for i in range(depth):
        j = i // 2
        ng = _row(norm_g[i])
        last = i == depth - 1
        if i % 2 == 0:
            w_in = mla_w_in[j]
            lat = Q_LORA + KV_LORA
            w_in = jnp.concatenate(
                [w_in[:, :lat], w_in[:, lat + QK_ROPE:], w_in[:, lat:lat + QK_ROPE],
                 jnp.zeros((d, LANES - QK_ROPE), w_in.dtype)], axis=1).astype(BF16)
            w_uq = mla_w_uq[j].reshape(Q_LORA, MLA_HEADS, QK_DIM)
            w_uq = jnp.concatenate([w_uq[:, :, :QK_NOPE].reshape(Q_LORA, -1),
                                    w_uq[:, :, QK_NOPE:].reshape(Q_LORA, -1)], axis=1).astype(BF16)
            w_ukv = mla_w_ukv[j].reshape(KV_LORA, MLA_HEADS, QK_NOPE + V_DIM)
            w_ukv = jnp.concatenate([w_ukv[:, :, :QK_NOPE].reshape(KV_LORA, -1),
                                     w_ukv[:, :, QK_NOPE:].reshape(KV_LORA, -1)], axis=1).astype(BF16)
            q, k, vt, gate = _mla_proj(xf, cos, sin, ng, w_in, _row(mla_q_norm[j]), w_uq,
                                       _row(mla_kv_norm[j]), w_ukv)
            y = _attention(q, k, vt, gate, bsz, seq)
            xf = _out_proj(y, mla_w_out[j].astype(BF16), xf, final_row, last)
        else:
            zeros = jnp.zeros((LORA, width), F32)
            ww2 = jnp.concatenate([rwkv_w_w2[j], zeros], axis=0).astype(BF16)
            wa2 = jnp.concatenate([zeros, rwkv_w_a2[j]], axis=0).astype(BF16)
            *ops, wl, bv, sg = _rwkv_proj(
                xf, ng, rwkv_w_in[j].astype(BF16), _row(rwkv_mu[j]), _row(rwkv_w0[j]), ww2,
                _row(rwkv_a0[j]), wa2, _row(rwkv_k_k[j]), _row(rwkv_k_a[j]), _row(rwkv_r_k[j]),
                tri, seg, segt, bsz, seq)
            y = _wkv(*ops, wl, bsz, seq)
            gn = (bv, sg, _row(rwkv_ln_g[j]), _row(rwkv_ln_b[j]), seg, segt)
            xf = _out_proj(y, rwkv_w_out[j].astype(BF16), xf, final_row, last, gn)
    return xf.reshape(bsz, seq, d)
```

```python
import functools

import jax
import jax.numpy as jnp
from jax import lax
from jax.experimental import pallas as pl
from jax.experimental.pallas import tpu as pltpu

F32 = jnp.float32
BF16 = jnp.bfloat16

NORM_EPS = 1e-6
GN_EPS = 64e-5
KK_EPS = 1e-12
ROPE_THETA = 10000.0

MLA_HEADS = 8
QK_NOPE = 128
QK_ROPE = 64
QK_DIM = QK_NOPE + QK_ROPE
V_DIM = 128
Q_LORA = 768
KV_LORA = 256

RWKV_HEAD = 64
LORA = 64

LANES = 128
CHUNK = 64
GROUP_HEADS = 2
GW = GROUP_HEADS * RWKV_HEAD

TOKEN_TILE = 512
GN_TILE = 256
TRI_ROWS = 256
MLA_TILE = 1024
OUT_TILE = 1024
ATTN_TILE = 1024
COL_BLOCK = 256
WKV_TILE = 512
VMEM_LIMIT = 56 * 1024 * 1024
MASK_VALUE = -1e30
ONES_ROWS = 16
LOG2E = 1.4426950408889634
EXP_M05 = 0.6065306597126334


def _params(*sem):
    return pltpu.CompilerParams(dimension_semantics=sem, vmem_limit_bytes=VMEM_LIMIT)


def _rms(x, g):
    return x * lax.rsqrt(jnp.mean(x * x, axis=-1, keepdims=True) + NORM_EPS) * g


def _dot(a, b):
    return jnp.dot(a.astype(BF16), b.astype(BF16), preferred_element_type=F32)


def _dot_nt(a, b):
    return lax.dot_general(a.astype(BF16), b.astype(BF16), (((1,), (1,)), ((), ())),
                           preferred_element_type=F32)


def _silu(x):
    return x / (1.0 + jnp.exp(-x))


def _rope(t, cos, sin_signed, first):
    partner = jnp.where(first, pltpu.roll(t, LANES - QK_ROPE // 2, 1), pltpu.roll(t, QK_ROPE // 2, 1))
    return t * cos + partner * sin_signed


def _mla_proj_kernel(x_ref, cos_ref, sin_ref, ng_ref, win_ref, qn_ref, wuq_ref, kvn_ref, wukv_ref,
                     q_ref, k_ref, v_ref, gate_ref):
    x = x_ref[...]
    h = _rms(x, ng_ref[...])
    proj = _dot(h, win_ref[...])
    q_lat = proj[:, :Q_LORA]
    kv_lat = proj[:, Q_LORA:Q_LORA + KV_LORA]
    lat = Q_LORA + KV_LORA
    gate_w = MLA_HEADS * V_DIM
    gate_ref[...] = proj[:, lat:lat + gate_w].astype(gate_ref.dtype)
    k_rope = proj[:, lat + gate_w:lat + gate_w + LANES]

    q = _dot(_rms(q_lat, qn_ref[...]), wuq_ref[...]) * (QK_DIM ** -0.5 * LOG2E)
    kv = _dot(_rms(kv_lat, kvn_ref[...]), wukv_ref[...])

    cos = cos_ref[...]
    sin = sin_ref[...]
    lane = lax.broadcasted_iota(jnp.int32, cos.shape, 1)
    first = (lane % QK_ROPE) < (QK_ROPE // 2)
    sin_signed = jnp.where(first, -sin, sin)

    k_rope = _rope(k_rope, cos, sin_signed, first)[:, :QK_ROPE].astype(k_ref.dtype)
    nope_w = MLA_HEADS * QK_NOPE
    for hp in range(MLA_HEADS // 2):
        q_rope = _rope(q[:, nope_w + hp * LANES: nope_w + (hp + 1) * LANES], cos, sin_signed, first)
        for hh in range(2):
            hd = 2 * hp + hh
            q_ref[hd, :, :QK_NOPE] = q[:, hd * QK_NOPE:(hd + 1) * QK_NOPE].astype(q_ref.dtype)
            q_ref[hd, :, QK_NOPE:] = q_rope[:, hh * QK_ROPE:(hh + 1) * QK_ROPE].astype(q_ref.dtype)
    for hd in range(MLA_HEADS):
        k_ref[hd, :, :QK_NOPE] = kv[:, hd * QK_NOPE:(hd + 1) * QK_NOPE].astype(k_ref.dtype)
        k_ref[hd, :, QK_NOPE:] = k_rope
        v_ref[hd, 0] = kv[:, nope_w + hd * V_DIM: nope_w + (hd + 1) * V_DIM].T.astype(v_ref.dtype)


def _mla_proj(x, cos, sin, ng, w_in, qn, w_uq, kvn, w_ukv):
    t, d = x.shape
    tm = MLA_TILE
    const = lambda shape: pl.BlockSpec(shape, lambda i: (0,) * len(shape))
    return pl.pallas_call(
        _mla_proj_kernel,
        grid=(t // tm,),
        in_specs=[pl.BlockSpec((tm, d), lambda i: (i, 0)),
                  pl.BlockSpec((tm, LANES), lambda i: (i, 0)),
                  pl.BlockSpec((tm, LANES), lambda i: (i, 0)),
                  const(ng.shape), const(w_in.shape), const(qn.shape), const(w_uq.shape),
                  const(kvn.shape), const(w_ukv.shape)],
        out_specs=[pl.BlockSpec((MLA_HEADS, tm, QK_DIM), lambda i: (0, i, 0)),
                   pl.BlockSpec((MLA_HEADS, tm, QK_DIM), lambda i: (0, i, 0)),
                   pl.BlockSpec((MLA_HEADS, 1, V_DIM, tm), lambda i: (0, i, 0, 0)),
                   pl.BlockSpec((tm, MLA_HEADS * V_DIM), lambda i: (i, 0))],
        out_shape=[jax.ShapeDtypeStruct((MLA_HEADS, t, QK_DIM), BF16),
                   jax.ShapeDtypeStruct((MLA_HEADS, t, QK_DIM), BF16),
                   jax.ShapeDtypeStruct((MLA_HEADS, t // tm, V_DIM, tm), BF16),
                   jax.ShapeDtypeStruct((t, MLA_HEADS * V_DIM), BF16)],
        compiler_params=_params("parallel"),
        name="mla_proj",
    )(x, cos, sin, ng, w_in, qn, w_uq, kvn, w_ukv)


def _attn_kernel(q_ref, k_ref, vt_ref, gate_ref, o_ref, st0_ref, st1_ref, st2_ref, acc_ref, *,
                 blk, sub, nq):
    nsub = blk // sub

    def block0_scores(q_rows):
        st = _dot_nt(k_ref[0, pl.ds(0, blk), :], q_ref[0, q_rows, :])
        st2_ref[...] = st
        return jnp.max(st, axis=0, keepdims=True)

    def query_block(qi, mx0):
        rows = pl.ds(pl.multiple_of(qi * blk, blk), blk)
        q = q_ref[0, rows, :]

        def scores(kb, st_ref):
            start = pl.multiple_of(kb * blk, blk)
            st = _dot_nt(k_ref[0, pl.ds(start, blk), :], q)
            st_ref[...] = st
            return jnp.max(st, axis=0, keepdims=True)

        def update(kb, st_ref, st_max, m, masked):
            vt = jnp.concatenate([vt_ref[0, kb * nsub + j] for j in range(nsub)], axis=1)
            vt = jnp.concatenate([vt, jnp.ones((ONES_ROWS, blk), vt.dtype)], axis=0)
            st = st_ref[...]
            if masked:
                key = lax.broadcasted_iota(jnp.int32, st.shape, 0)
                qry = lax.broadcasted_iota(jnp.int32, st.shape, 1)
                st = jnp.where(key <= qry, st, MASK_VALUE)
                st_max = jnp.max(st, axis=0, keepdims=True)
            m_new = jnp.maximum(m, st_max)
            pt = jnp.exp2((st - m_new).astype(BF16))
            corr = jnp.exp2(m - m_new)
            acc_ref[...] = acc_ref[...] * corr + _dot(vt, pt)
            return m_new

        acc_ref[...] = jnp.zeros_like(acc_ref)
        m_init = jnp.full((1, blk), MASK_VALUE, F32)

        def finish(st_ref, mx, m):
            update(qi, st_ref, mx, m, True)
            nxt = pl.ds(pl.multiple_of(jnp.minimum(qi + 1, nq - 1) * blk, blk), blk)
            return block0_scores(nxt)

        def body(j, c):
            mx, m = c
            mx1 = scores(2 * j + 2, st1_ref)
            m = update(2 * j + 1, st0_ref, mx, m, False)
            mx2 = scores(2 * j + 3, st0_ref)
            return mx2, update(2 * j + 2, st1_ref, mx1, m, False)

        def with_prefix(_):
            mx = scores(1, st0_ref)
            c = (mx, update(0, st2_ref, mx0, m_init, False))
            npairs = (qi - 1) // 2
            c = lax.fori_loop(0, npairs // 2, lambda j, c: body(2 * j + 1, body(2 * j, c)), c)
            c = lax.cond(npairs % 2 == 1, lambda c: body(npairs - 1, c), lambda c: c, c)

            def tail_odd(c):
                mx1 = scores(qi, st1_ref)
                return finish(st1_ref, mx1, update(qi - 1, st0_ref, c[0], c[1], False))

            return lax.cond((qi - 1) % 2 == 0, lambda c: finish(st0_ref, c[0], c[1]), tail_odd, c)

        mx_next = lax.cond(qi == 0, lambda _: finish(st2_ref, mx0, m_init), with_prefix, 0)
        gate = gate_ref[rows, :].astype(F32)
        out = acc_ref[:V_DIM, :] / acc_ref[V_DIM:V_DIM + 1, :]
        o_ref[rows, :] = (out.T * _silu(gate)).astype(o_ref.dtype)
        return mx_next

    lax.fori_loop(0, nq, query_block, block0_scores(pl.ds(0, blk)))


def _attention(q, k, vt, gate, bsz, seq):
    blk = ATTN_TILE
    sub = vt.shape[3]
    nv = seq // sub
    return pl.pallas_call(
        functools.partial(_attn_kernel, blk=blk, sub=sub, nq=seq // blk),
        grid=(bsz, MLA_HEADS),
        in_specs=[pl.BlockSpec((1, seq, QK_DIM), lambda b, h: (h, b, 0)),
                  pl.BlockSpec((1, seq, QK_DIM), lambda b, h: (h, b, 0)),
                  pl.BlockSpec((1, nv, V_DIM, sub), lambda b, h: (h, b, 0, 0)),
                  pl.BlockSpec((seq, V_DIM), lambda b, h: (b, h))],
        out_specs=pl.BlockSpec((seq, V_DIM), lambda b, h: (b, h)),
        out_shape=jax.ShapeDtypeStruct((bsz * seq, MLA_HEADS * V_DIM), BF16),
        scratch_shapes=[pltpu.VMEM((blk, blk), F32)] * 3 + [pltpu.VMEM((V_DIM + ONES_ROWS, blk), F32)],
        compiler_params=_params("parallel", "parallel"),
        name="mla_attention",
    )(q, k, vt, gate)


def _seg_sum(x, seg, segt):
    s = jnp.dot(x.astype(BF16), seg, preferred_element_type=F32)
    hi = s.astype(BF16)
    lo = (s - hi.astype(F32)).astype(BF16)
    return (jnp.dot(hi, segt, preferred_element_type=F32)
            + jnp.dot(lo, segt, preferred_element_type=F32))


def _finish(y, w_ref, x_ref, g_ref, o_ref, final_norm):
    x = x_ref[...] + jnp.dot(y.astype(BF16), w_ref[...], preferred_element_type=F32)
    if final_norm:
        x = _rms(x, g_ref[...])
    o_ref[...] = x


def _out_proj_kernel(y_ref, w_ref, x_ref, g_ref, o_ref, *, final_norm):
    _finish(y_ref[...], w_ref, x_ref, g_ref, o_ref, final_norm)


def _rwkv_out_kernel(y_ref, bv_ref, sg_ref, lng_ref, lnb_ref, seg_ref, segt_ref, w_ref, x_ref, g_ref,
                     o_ref, *, final_norm):
    y = y_ref[...].astype(F32)
    mean = _seg_sum(y, seg_ref[...], segt_ref[...]) * (1.0 / RWKV_HEAD)
    yc = y - mean
    var = _seg_sum(yc * yc, seg_ref[...], segt_ref[...]) * (1.0 / RWKV_HEAD)
    yn = yc * lax.rsqrt(var + GN_EPS) * lng_ref[...] + lnb_ref[...]
    out = (yn + bv_ref[...].astype(F32)) * sg_ref[...].astype(F32)
    _finish(out, w_ref, x_ref, g_ref, o_ref, final_norm)


def _out_proj(y, w, x, final_g, final_norm, gn=None):
    t, d = x.shape
    tm = OUT_TILE if gn is None else GN_TILE
    tile = lambda a: pl.BlockSpec((tm, a.shape[1]), lambda i: (i, 0))
    const = lambda a: pl.BlockSpec(a.shape, lambda i: (0, 0))
    if gn is None:
        body, ops = _out_proj_kernel, [y]
        specs = [tile(y)]
    else:
        bv, sg, lng, lnb, seg, segt = gn
        body, ops = _rwkv_out_kernel, [y, bv, sg, lng, lnb, seg, segt]
        specs = [tile(y), tile(bv), tile(sg), const(lng), const(lnb), const(seg), const(segt)]
    return pl.pallas_call(
        functools.partial(body, final_norm=final_norm),
        grid=(t // tm,),
        in_specs=specs + [const(w), tile(x), const(final_g)],
        out_specs=pl.BlockSpec((tm, d), lambda i: (i, 0)),
        out_shape=jax.ShapeDtypeStruct((t, d), F32),
        compiler_params=_params("parallel"),
        name="out_proj" if gn is None else "rwkv_out",
    )(*ops, w, x, final_g)


def _split3(x):
    hi = x.astype(BF16)
    r1 = x - hi.astype(F32)
    mid = r1.astype(BF16)
    lo = (r1 - mid.astype(F32)).astype(BF16)
    return hi, mid, lo


def _rwkv_proj_kernel(x_ref, ng_ref, win_ref, mu_ref, w0_ref, ww2_ref, a0_ref, wa2_ref,
                      kk_ref, ka_ref, rk_ref, tri_ref, seg_ref, segt_ref,
                      rt_ref, at_ref, bt_ref, kt_ref, bh_ref, kh_ref, v_ref, wl_ref, bv_ref, sg_ref,
                      prev_ref, *, width):
    @pl.when(pl.program_id(1) == 0)
    def _():
        prev_ref[...] = jnp.zeros_like(prev_ref)

    h = _rms(x_ref[...], ng_ref[...]).astype(BF16)
    rows = h.shape[0]
    nchunks = rows // CHUNK
    row8 = lax.broadcasted_iota(jnp.int32, (8, 1), 0)
    tri = tri_ref[...]

    def project(cols):
        proj = jnp.dot(h, win_ref[:, cols], preferred_element_type=F32)
        rolled = pltpu.roll(proj, 1, 0)
        shifted = jnp.concatenate([jnp.where(row8 == 0, prev_ref[0:1, cols], rolled[:8]), rolled[8:]],
                                  axis=0)
        prev_ref[0:1, cols] = proj[rows - 1:rows, :]
        return proj + mu_ref[:, cols] * (shifted - proj)

    lora = project(slice(4 * width, 4 * width + 2 * LORA))
    lane = lax.broadcasted_iota(jnp.int32, lora.shape, 1)
    lora = jnp.where(lane < LORA, jnp.tanh(lora), lora).astype(BF16)

    for cb in range(width // COL_BLOCK):
        cs = slice(cb * COL_BLOCK, (cb + 1) * COL_BLOCK)
        r, k, v, g = (project(slice(part * width + cs.start, part * width + cs.stop))
                      for part in range(4))
        sg_ref[:, cs] = _silu(g).astype(sg_ref.dtype)

        seg_sum = functools.partial(_seg_sum, seg=seg_ref[cs, :], segt=segt_ref[:, cs])

        z = w0_ref[:, cs] + jnp.dot(lora, ww2_ref[:, cs], preferred_element_type=F32)
        log_decay = (-LOG2E * EXP_M05) / (1.0 + jnp.exp(-z))
        a = 1.0 / (1.0 + jnp.exp(-(a0_ref[:, cs]
                                    + jnp.dot(lora, wa2_ref[:, cs], preferred_element_type=F32))))

        kk = k * kk_ref[:, cs]
        kk = kk * lax.rsqrt(jnp.maximum(seg_sum(kk * kk), KK_EPS ** 2))
        kp = k * (1.0 + (a - 1.0) * ka_ref[:, cs])
        b = kk * a
        bv_ref[:, cs] = (seg_sum(r * kp * rk_ref[:, cs]) * v).astype(bv_ref.dtype)

        parts = _split3(log_decay)
        cum = jnp.concatenate(
            [sum(jnp.dot(tri, part[sb:sb + TRI_ROWS], preferred_element_type=F32) for part in parts)
             for sb in range(0, rows, TRI_ROWS)], axis=0)
        last = [cum[(c + 1) * CHUNK - 1:(c + 1) * CHUNK, :] for c in range(nchunks)]
        cum_last = jnp.concatenate([jnp.broadcast_to(x, (CHUNK, COL_BLOCK)) for x in last], axis=0)
        wl_ref[:, cs] = jnp.concatenate([jnp.broadcast_to(jnp.exp2(x), (8, COL_BLOCK)) for x in last],
                                        axis=0)
        w_inv = jnp.exp2(-cum)
        w_tail = jnp.exp2(cum_last - cum)
        outs = ((rt_ref, r * jnp.exp2(cum)), (at_ref, -kk * jnp.exp2(cum - log_decay)),
                (bt_ref, b * w_inv), (kt_ref, kp * w_inv), (bh_ref, b * w_tail),
                (kh_ref, kp * w_tail), (v_ref, v))
        for ref, val in outs:
            for gj in range(COL_BLOCK // GW):
                ref[cb * (COL_BLOCK // GW) + gj] = val[:, gj * GW:(gj + 1) * GW].astype(ref.dtype)


def _rwkv_proj(x, ng, w_in, mu, w0, ww2, a0, wa2, kk, ka, rk, tri, seg, segt, bsz, seq):
    t, d = x.shape
    width = w0.shape[1]
    ngroups = width // GW
    tm = TOKEN_TILE
    ns = seq // tm
    const = lambda a: pl.BlockSpec(a.shape, lambda b, i: (0,) * a.ndim)
    consts = [ng, w_in, mu, w0, ww2, a0, wa2, kk, ka, rk, tri, seg, segt]
    gspec = pl.BlockSpec((ngroups, tm, GW), lambda b, i: (0, b * ns + i, 0))
    nspec = pl.BlockSpec((tm, width), lambda b, i: (b * ns + i, 0))
    wl_rows = tm // CHUNK * 8
    return pl.pallas_call(
        functools.partial(_rwkv_proj_kernel, width=width),
        grid=(bsz, ns),
        in_specs=[pl.BlockSpec((tm, d), lambda b, i: (b * ns + i, 0))] + [const(a) for a in consts],
        out_specs=[gspec] * 7 + [pl.BlockSpec((wl_rows, width), lambda b, i: (b * ns + i, 0)),
                                 nspec, nspec],
        out_shape=[jax.ShapeDtypeStruct((ngroups, t, GW), BF16)] * 7
        + [jax.ShapeDtypeStruct((t // CHUNK * 8, width), F32),
           jax.ShapeDtypeStruct((t, width), BF16), jax.ShapeDtypeStruct((t, width), BF16)],
        scratch_shapes=[pltpu.VMEM((8, w_in.shape[1]), F32)],
        compiler_params=_params("parallel", "arbitrary"),
        name="rwkv_proj",
    )(x, *consts)


def _block_diag(y, bd_mask):
    return jnp.where(bd_mask, jnp.concatenate([y.astype(BF16)] * GROUP_HEADS, axis=0), 0)


def _wkv_kernel(rt_ref, at_ref, bt_ref, kt_ref, bh_ref, kh_ref, v_ref, wl_ref, y_ref, state_ref, *,
                ngroups, nchunks):
    @pl.when(pl.program_id(1) == 0)
    def _():
        state_ref[...] = jnp.zeros_like(state_ref)

    t_idx = lax.broadcasted_iota(jnp.int32, (CHUNK, GW), 0)
    s_idx = lax.broadcasted_iota(jnp.int32, (CHUNK, GW), 1) % CHUNK
    incl = s_idx <= t_idx
    strict = s_idx < t_idx
    eye = (s_idx == t_idx).astype(F32)
    blk8 = strict & ((s_idx // 8) == (t_idx // 8))
    off_masks = [((s_idx // (2 * b)) == (t_idx // (2 * b))) & ((t_idx // b) % 2 == 1)
                 & ((s_idx // b) % 2 == 0) for b in (8, 16, 32)]
    bd_mask = (lax.broadcasted_iota(jnp.int32, (GW, GW), 0) // RWKV_HEAD
               == lax.broadcasted_iota(jnp.int32, (GW, GW), 1) // RWKV_HEAD)

    def bd(y):
        return _block_diag(y, bd_mask)

    def mm(x, y):
        return _dot(x, bd(y))

    gs = range(ngroups)
    chains = [(c, gi) for c in range(nchunks) for gi in gs]
    rows = [slice(c * CHUNK, (c + 1) * CHUNK) for c, _ in chains]
    r_t = [rt_ref[gi, rw, :] for (_, gi), rw in zip(chains, rows)]
    a_t = [at_ref[gi, rw, :] for (_, gi), rw in zip(chains, rows)]
    v = [v_ref[gi, rw, :] for (_, gi), rw in zip(chains, rows)]

    amat = [_dot_nt(jnp.concatenate([r_t[n], a_t[n]], axis=0),
                    jnp.concatenate([bd(bt_ref[gi, rows[n], :]), bd(kt_ref[gi, rows[n], :])], axis=0))
            for n, (_, gi) in enumerate(chains)]
    a_rb = [jnp.where(incl, x[:CHUNK, :GW], 0.0) for x in amat]
    a_rk = [jnp.where(incl, x[:CHUNK, GW:], 0.0) for x in amat]
    m_ab = [jnp.where(strict, x[CHUNK:, :GW], 0.0) for x in amat]
    a_ak = [jnp.where(strict, x[CHUNK:, GW:], 0.0) for x in amat]

    n1 = [jnp.where(blk8, x, 0.0) for x in m_ab]
    n2 = [mm(x, x) for x in n1]
    tmat = [eye + x for x in n1]
    both = [mm(jnp.concatenate([x, t], axis=0), x) for x, t in zip(n2, tmat)]
    n4 = [x[:CHUNK] for x in both]
    tmat = [t + x[CHUNK:] for t, x in zip(tmat, both)]
    tmat = [x + mm(x, y) for x, y in zip(tmat, n4)]
    for off in off_masks:
        z = [mm(jnp.where(off, m, 0.0), x) for m, x in zip(m_ab, tmat)]
        tmat = [x + mm(x, y) for x, y in zip(tmat, z)]

    akv = [mm(x, y) for x, y in zip(a_ak, v)]
    pq = [_dot(tmat[n], jnp.concatenate([bd(a_t[n]), bd(akv[n])], axis=1))
          for n in range(len(chains))]

    state = [state_ref[gi] for gi in gs]
    for c in range(nchunks):
        ns = [c * ngroups + gi for gi in gs]
        us = [_dot_nt(jnp.concatenate([pq[n][:, :GW].astype(BF16), r_t[n]], axis=0), state[gi])
              for gi, n in zip(gs, ns)]
        u = [us[gi][:CHUNK] + pq[n][:, GW:] for gi, n in zip(gs, ns)]
        y = [us[gi][CHUNK:]
             + _dot(jnp.concatenate([a_rb[n], a_rk[n]], axis=1),
                    jnp.concatenate([bd(u[gi]), bd(v[n])], axis=0)) for gi, n in zip(gs, ns)]
        upd = [_dot(jnp.concatenate([u[gi], v[n].astype(F32)], axis=0).T,
                    jnp.concatenate([bh_ref[gi, rows[n], :], kh_ref[gi, rows[n], :]], axis=0))
               for gi, n in zip(gs, ns)]
        wl = wl_ref[c * 8:c * 8 + 1, :]
        state = [state[gi] * wl[:, gi * GW:(gi + 1) * GW] + jnp.where(bd_mask, upd[gi], 0.0)
                 for gi in gs]
        for gi in gs:
            y_ref[rows[ns[gi]], gi * GW:(gi + 1) * GW] = y[gi].astype(y_ref.dtype)
    for gi in gs:
        state_ref[gi] = state[gi]


def _wkv(rt, at, bt, kt, bh, kh, v, wl, bsz, seq):
    ngroups, t, _ = rt.shape
    tt = WKV_TILE
    ns = seq // tt
    gspec = pl.BlockSpec((ngroups, tt, GW), lambda b, i: (0, b * ns + i, 0))
    return pl.pallas_call(
        functools.partial(_wkv_kernel, ngroups=ngroups, nchunks=tt // CHUNK),
        grid=(bsz, ns),
        in_specs=[gspec] * 7 + [pl.BlockSpec((tt // CHUNK * 8, ngroups * GW),
                                             lambda b, i: (b * ns + i, 0))],
        out_specs=pl.BlockSpec((tt, ngroups * GW), lambda b, i: (b * ns + i, 0)),
        out_shape=jax.ShapeDtypeStruct((t, ngroups * GW), BF16),
        scratch_shapes=[pltpu.VMEM((ngroups, GW, GW), F32)],
        compiler_params=_params("parallel", "arbitrary"),
        name="rwkv_wkv",
    )(rt, at, bt, kt, bh, kh, v, wl)


def _row(v):
    return v.reshape(1, -1).astype(F32)


def kernel(x, positions, norm_g, mla_w_in, mla_q_norm, mla_w_uq, mla_kv_norm, mla_w_ukv, mla_w_out, rwkv_w_in, rwkv_mu, rwkv_w0, rwkv_w_w2, rwkv_a0, rwkv_w_a2, rwkv_k_k, rwkv_k_a, rwkv_r_k, rwkv_ln_g, rwkv_ln_b, rwkv_w_out, final_g):
    bsz, seq, d = x.shape
    depth = norm_g.shape[0]
    t = bsz * seq
    xf = x.reshape(t, d)
    half = QK_ROPE // 2
    inv_freq = 1.0 / (ROPE_THETA ** (jnp.arange(half, dtype=F32) / half))
    ang = positions.reshape(t, 1).astype(F32) * inv_freq[None, :]
    cos = jnp.tile(jnp.cos(ang), (1, LANES // half))
    sin = jnp.tile(jnp.sin(ang), (1, LANES // half))
    final_row = _row(final_g)

    width = rwkv_w0.shape[1]
    tok = jnp.arange(TRI_ROWS)
    tri = ((tok[:, None] // CHUNK == tok[None, :] // CHUNK) & (tok[None, :] <= tok[:, None])).astype(BF16)
    seg = (jnp.arange(width)[:, None] // RWKV_HEAD == jnp.arange(LANES)[None, :]).astype(BF16)
    segt = seg.T

    for i in range(depth):
        j = i // 2
        ng = _row(norm_g[i])
        last = i == depth - 1
        if i % 2 == 0:
            w_in = mla_w_in[j]
            lat = Q_LORA + KV_LORA
            w_in = jnp.concatenate(
                [w_in[:, :lat], w_in[:, lat + QK_ROPE:], w_in[:, lat:lat + QK_ROPE],
                 jnp.zeros((d, LANES - QK_ROPE), w_in.dtype)], axis=1).astype(BF16)
            w_uq = mla_w_uq[j].reshape(Q_LORA, MLA_HEADS, QK_DIM)
            w_uq = jnp.concatenate([w_uq[:, :, :QK_NOPE].reshape(Q_LORA, -1),
                                    w_uq[:, :, QK_NOPE:].reshape(Q_LORA, -1)], axis=1).astype(BF16)
            w_ukv = mla_w_ukv[j].reshape(KV_LORA, MLA_HEADS, QK_NOPE + V_DIM)
            w_ukv = jnp.concatenate([w_ukv[:, :, :QK_NOPE].reshape(KV_LORA, -1),
                                     w_ukv[:, :, QK_NOPE:].reshape(KV_LORA, -1)], axis=1).astype(BF16)
            q, k, vt, gate = _mla_proj(xf, cos, sin, ng, w_in, _row(mla_q_norm[j]), w_uq,
                                       _row(mla_kv_norm[j]), w_ukv)
            y = _attention(q, k, vt, gate, bsz, seq)
            xf = _out_proj(y, mla_w_out[j].astype(BF16), xf, final_row, last)
        else:
            zeros = jnp.zeros((LORA, width), F32)
            ww2 = jnp.concatenate([rwkv_w_w2[j], zeros], axis=0).astype(BF16)
            wa2 = jnp.concatenate([zeros, rwkv_w_a2[j]], axis=0).astype(BF16)
            *ops, wl, bv, sg = _rwkv_proj(
                xf, ng, rwkv_w_in[j].astype(BF16), _row(rwkv_mu[j]), _row(rwkv_w0[j]), ww2,
                _row(rwkv_a0[j]), wa2, _row(rwkv_k_k[j]), _row(rwkv_k_a[j]), _row(rwkv_r_k[j]),
                tri, seg, segt, bsz, seq)
            y = _wkv(*ops, wl, bsz, seq)
            gn = (bv, sg, _row(rwkv_ln_g[j]), _row(rwkv_ln_b[j]), seg, segt)
            xf = _out_proj(y, rwkv_w_out[j].astype(BF16), xf, final_row, last, gn)
    return xf.reshape(bsz, seq, d)
```
